```python
import jax
import jax.numpy as jnp
from jax import lax
import numpy as np

D_MODEL = 1024
BATCH = 8
SEQ = 2048
DEPTH = 2

GRID_W = 64
CTX_LEN = 256
N_EVEN = (DEPTH + 1) // 2
N_ODD = DEPTH // 2
HEAD_DIM = 64
BLK = 128
A_HEADS = 4
A_DIM = 128
A_WIDTH = A_HEADS * A_DIM
A_CHUNK = 64
B_HEADS = 8
B_KV = 2
B_WIN = 128
C_HEADS = 8
C_KV = 2
D_HEADS = 8
NA_ROWS = 8
NA_COLS = 16
MIX_WIDTH = A_WIDTH + B_HEADS * HEAD_DIM
D_FF = 2816
ROPE_THETA = 10000.0
EPS = 1e-6
NEG = -1e30
EVEN_SPLITS = (A_WIDTH, A_WIDTH, A_WIDTH, A_WIDTH, 4 * A_HEADS, B_HEADS * HEAD_DIM, B_KV * HEAD_DIM, B_KV * HEAD_DIM)
ODD_SPLITS = (C_HEADS * HEAD_DIM, C_KV * HEAD_DIM, C_KV * HEAD_DIM, D_HEADS * HEAD_DIM, D_HEADS * HEAD_DIM, D_HEADS * HEAD_DIM)
EVEN_COLS = sum(EVEN_SPLITS)
ODD_COLS = sum(ODD_SPLITS)

kernel_name = 'hybrid_mlstm_swa_gqa_natten_prefix'

f32 = jnp.float32


def split_cols(u, sizes):
    return jnp.split(u, np.cumsum(sizes)[:-1].tolist(), axis=-1)


def heads(a, n):
    return a.reshape(a.shape[:-1] + (n, a.shape[-1] // n))


def group(q, n_kv):
    return q.reshape(q.shape[:2] + (n_kv, q.shape[2] // n_kv, q.shape[3]))


def rms_norm(x, g):
    xf = x.astype(f32)
    y = xf * lax.rsqrt(jnp.mean(xf * xf, axis=-1, keepdims=True) + EPS)
    return (y * g.astype(f32)).astype(x.dtype)


def modulate(x, g, shift, scale):
    return rms_norm(x, g) * (1 + scale) + shift


def axial_rope(n):
    t = jnp.arange(n)
    row = (t // GRID_W).astype(f32)
    col = (t % GRID_W).astype(f32)
    half = HEAD_DIM // 2
    freq = ROPE_THETA ** (-jnp.arange(0, half, 2, dtype=f32) / half)
    ang_r = row[:, None] * freq[None, :]
    ang_c = col[:, None] * freq[None, :]
    ang = jnp.concatenate([ang_r, ang_r, ang_c, ang_c], axis=-1)
    return jnp.cos(ang), jnp.sin(ang)


def apply_rope(x, cos, sin):
    xr = x.reshape(x.shape[:-1] + (2, 2, HEAD_DIM // 4))
    rot = jnp.stack([-xr[..., 1, :], xr[..., 0, :]], axis=-2).reshape(x.shape)
    return x * cos[:, None, :].astype(x.dtype) + rot * sin[:, None, :].astype(x.dtype)


def qk_prep(a, g, rope, scale=1.0):
    a = rms_norm(a, g)
    if rope is not None:
        a = apply_rope(a, rope[0], rope[1])
    return a * scale


def joint_softmax(parts):
    sizes = [p.shape[-1] for p in parts]
    logits = jnp.concatenate([p.astype(f32) for p in parts], axis=-1)
    probs = jax.nn.softmax(logits, axis=-1)
    return jnp.split(probs, np.cumsum(sizes)[:-1].tolist(), axis=-1)


def sink_column(sink, scores):
    hkv, g = scores.shape[-4], scores.shape[-3]
    return jnp.broadcast_to(sink.astype(f32).reshape(hkv, g, 1, 1), scores.shape[:-1] + (1,))


def context_attention(q, k, v, sink=None):
    s = jnp.einsum('bqhgd,bkhd->bhgqk', q, k).astype(f32)
    if sink is None:
        p = jax.nn.softmax(s, axis=-1)
    else:
        _, p = joint_softmax([sink_column(sink, s), s])
    o = jnp.einsum('bhgqk,bkhd->bqhgd', p.astype(v.dtype), v)
    return o.reshape(o.shape[:2] + (-1,))


def window_attention(q, k, v, k_ctx, v_ctx, sink):
    B, S, Hkv, G, d = q.shape
    nb = S // BLK
    qb = q.reshape(B, nb, BLK, Hkv, G, d)

    def band(a):
        ap = jnp.pad(a, ((0, 0), (BLK, BLK), (0, 0), (0, 0))).reshape(B, nb + 2, BLK, Hkv, d)
        return jnp.concatenate([ap[:, :-2], ap[:, 1:-1], ap[:, 2:]], axis=2)

    kb, vb = band(k), band(v)
    t_pos = jnp.arange(S).reshape(nb, BLK)
    s_pos = jnp.arange(nb)[:, None] * BLK - BLK + jnp.arange(3 * BLK)[None, :]
    sp = s_pos[:, None, :]
    valid = (sp >= 0) & (sp < S) & (jnp.abs(t_pos[:, :, None] - sp) <= B_WIN)
    s_band = jnp.einsum('bnqhgd,bnkhd->bnhgqk', qb, kb).astype(f32)
    s_band = jnp.where(valid[:, None, None, :, :], s_band, NEG)
    s_ctx = jnp.einsum('bnqhgd,bchd->bnhgqc', qb, k_ctx)
    _, p_ctx, p_band = joint_softmax([sink_column(sink, s_band), s_ctx, s_band])
    o = (jnp.einsum('bnhgqc,bchd->bnqhgd', p_ctx.astype(v.dtype), v_ctx)
         + jnp.einsum('bnhgqk,bnkhd->bnqhgd', p_band.astype(v.dtype), vb))
    return o.reshape(B, S, Hkv * G * d)


def global_attention(q, k, v, k_ctx, v_ctx):
    B, S, Hkv, G, d = q.shape
    nb = S // BLK
    kk = jnp.concatenate([k_ctx, k], axis=1)
    vv = jnp.concatenate([v_ctx, v], axis=1)

    def one_block(qi):
        s = jnp.einsum('bqhgd,bkhd->bhgqk', qi, kk).astype(f32)
        p = jax.nn.softmax(s, axis=-1).astype(vv.dtype)
        return jnp.einsum('bhgqk,bkhd->bqhgd', p, vv)

    qb = jnp.moveaxis(q.reshape(B, nb, BLK, Hkv, G, d), 1, 0)
    o = lax.map(one_block, qb)
    return jnp.moveaxis(o, 0, 1).reshape(B, S, Hkv * G * d)


def neighbourhood_attention(q, k, v, k_ctx, v_ctx, rpb):
    B, S, H, d = q.shape
    rows = S // GRID_W
    wr = min(NA_ROWS, rows)
    r = jnp.arange(rows)
    row_idx = jnp.clip(r - wr // 2, 0, rows - wr)[:, None] + jnp.arange(wr)[None, :]
    col = jnp.arange(GRID_W)
    col_start = jnp.clip(col - NA_COLS // 2, 0, GRID_W - NA_COLS)
    col_ok = (col[None, :] >= col_start[:, None]) & (col[None, :] < col_start[:, None] + NA_COLS)

    def gather_rows(a):
        return a.reshape(B, rows, GRID_W, H, d)[:, row_idx].reshape(B, rows, wr * GRID_W, H, d)

    kg, vg = gather_rows(k), gather_rows(v)
    dr = row_idx - r[:, None] + NA_ROWS - 1
    dc = jnp.clip(col[None, :] - col[:, None] + NA_COLS - 1, 0, 2 * NA_COLS - 2)
    bias = rpb[:, dr[:, None, :, None], dc[None, :, None, :]].astype(f32)
    bias = jnp.where(col_ok[None, None, :, None, :], bias, NEG).reshape(H, rows, GRID_W, wr * GRID_W)
    qg = q.reshape(B, rows, GRID_W, H, d)
    s_nb = jnp.einsum('brqhd,brkhd->bhrqk', qg, kg).astype(f32) + bias
    s_ctx = jnp.einsum('brqhd,bchd->bhrqc', qg, k_ctx)
    p_ctx, p_nb = joint_softmax([s_ctx, s_nb])
    o = (jnp.einsum('bhrqc,bchd->brqhd', p_ctx.astype(v.dtype), v_ctx)
         + jnp.einsum('bhrqk,brkhd->brqhd', p_nb.astype(v.dtype), vg))
    return o.reshape(B, S, H * d)


def mlstm_scan(q, k, v, log_i, log_f, state):
    B, H, T, dh = q.shape
    nc = T // A_CHUNK

    def to_chunks(a):
        return jnp.moveaxis(a.reshape(a.shape[:2] + (nc, A_CHUNK) + a.shape[3:]), 2, 0)

    xs = (to_chunks(q), to_chunks(k), to_chunks(v), to_chunks(log_i), to_chunks(log_f))
    tri = jnp.tril(jnp.ones((A_CHUNK, A_CHUNK), dtype=bool))

    def step(carry, inp):
        C, n, m = carry
        qc, kc, vc, li, lf = inp
        b = jnp.cumsum(lf, axis=-1)
        dmat = jnp.where(tri, b[..., :, None] - b[..., None, :] + li[..., None, :], -jnp.inf)
        inter = b + m[..., None]
        m_t = jnp.maximum(inter, jnp.max(dmat, axis=-1))
        w = jnp.exp(dmat - m_t[..., None])
        a = jnp.exp(inter - m_t)
        s = jnp.einsum('bhtd,bhsd->bhts', qc, kc) * w
        num = a[..., None] * jnp.einsum('bhtd,bhde->bhte', qc, C) + jnp.einsum('bhts,bhse->bhte', s, vc)
        den = a * jnp.einsum('bhtd,bhd->bht', qc, n) + jnp.sum(s, axis=-1)
        h = num / jnp.maximum(jnp.abs(den), jnp.exp(-m_t))[..., None]
        b_last = b[..., -1]
        g = b_last[..., None] - b + li
        m_new = jnp.maximum(b_last + m, jnp.max(g, axis=-1))
        decay = jnp.exp(b_last + m - m_new)
        wk = jnp.exp(g - m_new[..., None])[..., None] * kc
        C_new = decay[..., None, None] * C + jnp.einsum('bhsd,bhse->bhde', wk, vc)
        n_new = decay[..., None] * n + jnp.sum(wk, axis=2)
        return (C_new, n_new, m_new), h

    state, hs = lax.scan(step, state, xs)
    return jnp.moveaxis(hs, 0, 2).reshape(B, H, T, dh), state


def mlstm_prep(q, k, v, gates, gate_b):
    to_h = lambda a: jnp.swapaxes(heads(a.astype(f32), A_HEADS), 1, 2)
    g = (gates.astype(f32) + gate_b.astype(f32)).reshape(gates.shape[:2] + (4, A_HEADS))
    g = jnp.transpose(g, (2, 0, 3, 1))
    return (to_h(q), to_h(k) * A_DIM ** -0.5, to_h(v),
            g[0], jax.nn.log_sigmoid(g[1]), g[2], jax.nn.log_sigmoid(g[3]))


def mlstm_bidirectional(ctx_s, lat_s):
    qc, kc, vc, icf, fcf, icb, fcb = ctx_s
    ql, kl, vl, ilf, flf, ilb, flb = lat_s
    b = qc.shape[0]
    zero = (jnp.zeros((b, A_HEADS, A_DIM, A_DIM), f32), jnp.zeros((b, A_HEADS, A_DIM), f32), jnp.zeros((b, A_HEADS), f32))
    flip = lambda a: jnp.flip(a, axis=2)
    hcf, st_f = mlstm_scan(qc, kc, vc, icf, fcf, zero)
    hlf, _ = mlstm_scan(ql, kl, vl, ilf, flf, st_f)
    hcb, st_b = mlstm_scan(flip(qc), flip(kc), flip(vc), flip(icb), flip(fcb), zero)
    hlb, _ = mlstm_scan(flip(ql), flip(kl), flip(vl), flip(ilb), flip(flb), st_b)
    return hcf + flip(hcb), hlf + flip(hlb)


def mlstm_out(h, o, head_g, dtype):
    h = rms_norm(jnp.swapaxes(h, 1, 2), head_g.reshape(A_HEADS, A_DIM))
    return (jax.nn.sigmoid(o.astype(f32)) * h.reshape(h.shape[:2] + (A_WIDTH,))).astype(dtype)


def even_mixer(hc, hl, w_in, gate_b, head_g, qk_g, sink, rope, need_ctx):
    aq_l, ak_l, av_l, ao_l, ag_l, bq_l, bk_l, bv_l = split_cols(hl @ w_in, EVEN_SPLITS)
    aq_c, ak_c, av_c, ao_c, ag_c, bq_c, bk_c, bv_c = split_cols(hc @ w_in, EVEN_SPLITS)
    h_ctx, h_lat = mlstm_bidirectional(mlstm_prep(aq_c, ak_c, av_c, ag_c, gate_b),
                                       mlstm_prep(aq_l, ak_l, av_l, ag_l, gate_b))
    scale = HEAD_DIM ** -0.5
    k_c = qk_prep(heads(bk_c, B_KV), qk_g[1], None)
    v_c = heads(bv_c, B_KV)
    q_l = qk_prep(heads(bq_l, B_HEADS), qk_g[0], rope, scale)
    k_l = qk_prep(heads(bk_l, B_KV), qk_g[1], rope)
    b_lat = window_attention(group(q_l, B_KV), k_l, heads(bv_l, B_KV), k_c, v_c, sink)
    lat = jnp.concatenate([mlstm_out(h_lat, ao_l, head_g, hl.dtype), b_lat], axis=-1)
    if not need_ctx:
        return None, lat
    q_c = qk_prep(heads(bq_c, B_HEADS), qk_g[0], None, scale)
    b_ctx = context_attention(group(q_c, B_KV), k_c, v_c, sink)
    ctx_out = jnp.concatenate([mlstm_out(h_ctx, ao_c, head_g, hc.dtype), b_ctx], axis=-1)
    return ctx_out, lat


def odd_mixer(hc, hl, w_in, gqa_g, na_g, rpb, rope, need_ctx):
    cq_l, ck_l, cv_l, nq_l, nk_l, nv_l = split_cols(hl @ w_in, ODD_SPLITS)
    cq_c, ck_c, cv_c, nq_c, nk_c, nv_c = split_cols(hc @ w_in, ODD_SPLITS)
    scale = HEAD_DIM ** -0.5
    gk_c = qk_prep(heads(ck_c, C_KV), gqa_g[1], None)
    gv_c = heads(cv_c, C_KV)
    dk_c = qk_prep(heads(nk_c, D_HEADS), na_g[1], None)
    dv_c = heads(nv_c, D_HEADS)
    gq_l = qk_prep(heads(cq_l, C_HEADS), gqa_g[0], rope, scale)
    gk_l = qk_prep(heads(ck_l, C_KV), gqa_g[1], rope)
    c_lat = global_attention(group(gq_l, C_KV), gk_l, heads(cv_l, C_KV), gk_c, gv_c)
    dq_l = qk_prep(heads(nq_l, D_HEADS), na_g[0], None, scale)
    dk_l = qk_prep(heads(nk_l, D_HEADS), na_g[1], None)
    d_lat = neighbourhood_attention(dq_l, dk_l, heads(nv_l, D_HEADS), dk_c, dv_c, rpb)
    lat = jnp.concatenate([c_lat, d_lat], axis=-1)
    if not need_ctx:
        return None, lat
    gq_c = qk_prep(heads(cq_c, C_HEADS), gqa_g[0], None, scale)
    dq_c = qk_prep(heads(nq_c, D_HEADS), na_g[0], None, scale)
    ctx_out = jnp.concatenate([context_attention(group(gq_c, C_KV), gk_c, gv_c),
                               context_attention(dq_c[:, :, :, None, :], dk_c, dv_c)], axis=-1)
    return ctx_out, lat


def conv_ffn(h, w_up, conv_w, conv_b, w_down):
    u = h @ w_up
    u = lax.conv_general_dilated(u, conv_w[:, None, :].astype(u.dtype), window_strides=(1,), padding=((1, 1),),
                                 dimension_numbers=('NWC', 'WIO', 'NWC'), feature_group_count=u.shape[-1]) + conv_b
    gate, val = jnp.split(u, 2, axis=-1)
    return (jax.nn.silu(gate) * val) @ w_down


def setup_inputs(seed: int = 0) -> dict:
    key = jax.random.key(seed)
    ks = jax.random.split(key, 24)
    nrm = lambda k, shape, s: jax.random.normal(k, shape, f32) * s
    D = D_MODEL
    gk = jax.random.split(ks[13], 4)
    mlstm_gate_b = jnp.concatenate([
        nrm(gk[0], (N_EVEN, A_HEADS), 0.1),
        3.0 + 3.0 * jax.random.uniform(gk[1], (N_EVEN, A_HEADS), f32),
        nrm(gk[2], (N_EVEN, A_HEADS), 0.1),
        3.0 + 3.0 * jax.random.uniform(gk[3], (N_EVEN, A_HEADS), f32)], axis=-1)
    return {
        'x': nrm(ks[0], (BATCH, SEQ, D), 1.0),
        'c': nrm(ks[1], (BATCH, D), 1.0),
        'ctx': nrm(ks[2], (BATCH, CTX_LEN, D), 1.0),
        'c_ctx': nrm(ks[3], (D,), 1.0),
        'ada_w': nrm(ks[4], (DEPTH, D, 6 * D), 0.3 * D ** -0.5),
        'ada_b': nrm(ks[5], (DEPTH, 6 * D), 0.02),
        'norm_g': 1.0 + nrm(ks[6], (DEPTH, 2, D), 0.02),
        'w_out': nrm(ks[7], (DEPTH, MIX_WIDTH, D), MIX_WIDTH ** -0.5),
        'ffn_up': nrm(ks[8], (DEPTH, D, 2 * D_FF), D ** -0.5),
        'ffn_conv_w': nrm(ks[9], (DEPTH, 3, 2 * D_FF), 3 ** -0.5),
        'ffn_conv_b': nrm(ks[10], (DEPTH, 2 * D_FF), 0.02),
        'ffn_down': nrm(ks[11], (DEPTH, D_FF, D), D_FF ** -0.5),
        'even_w_in': nrm(ks[12], (N_EVEN, D, EVEN_COLS), D ** -0.5),
        'mlstm_gate_b': mlstm_gate_b,
        'mlstm_head_g': 1.0 + nrm(ks[14], (N_EVEN, A_WIDTH), 0.02),
        'swa_qk_g': 1.0 + nrm(ks[15], (N_EVEN, 2, HEAD_DIM), 0.02),
        'swa_sink': nrm(ks[16], (N_EVEN, B_HEADS), 0.5),
        'odd_w_in': nrm(ks[17], (N_ODD, D, ODD_COLS), D ** -0.5),
        'gqa_qk_g': 1.0 + nrm(ks[18], (N_ODD, 2, HEAD_DIM), 0.02),
        'na_qk_g': 1.0 + nrm(ks[19], (N_ODD, 2, HEAD_DIM), 0.02),
        'na_rpb': nrm(ks[20], (N_ODD, D_HEADS, 2 * NA_ROWS - 1, 2 * NA_COLS - 1), 0.5),
    }


def reference(x, c, ctx, c_ctx, ada_w, ada_b, norm_g, w_out, ffn_up, ffn_conv_w, ffn_conv_b, ffn_down,
              even_w_in, mlstm_gate_b, mlstm_head_g, swa_qk_g, swa_sink,
              odd_w_in, gqa_qk_g, na_qk_g, na_rpb):
    rope = axial_rope(x.shape[1])
    x_lat, x_ctx = x, ctx
    for l in range(DEPTH):
        need_ctx = l < DEPTH - 1
        mod_l = jnp.split((jax.nn.silu(c) @ ada_w[l] + ada_b[l])[:, None, :], 6, axis=-1)
        mod_c = jnp.split((jax.nn.silu(c_ctx) @ ada_w[l] + ada_b[l])[None, None, :], 6, axis=-1)
        hl = modulate(x_lat, norm_g[l, 0], mod_l[0], mod_l[1])
        hc = modulate(x_ctx, norm_g[l, 0], mod_c[0], mod_c[1])
        if l % 2 == 0:
            e = l // 2
            mix_c, mix_l = even_mixer(hc, hl, even_w_in[e], mlstm_gate_b[e], mlstm_head_g[e],
                                      swa_qk_g[e], swa_sink[e], rope, need_ctx)
        else:
            o = l // 2
            mix_c, mix_l = odd_mixer(hc, hl, odd_w_in[o], gqa_qk_g[o], na_qk_g[o], na_rpb[o], rope, need_ctx)
        x_lat = x_lat + mod_l[2] * (mix_l @ w_out[l])
        x_lat = x_lat + mod_l[5] * conv_ffn(modulate(x_lat, norm_g[l, 1], mod_l[3], mod_l[4]),
                                            ffn_up[l], ffn_conv_w[l], ffn_conv_b[l], ffn_down[l])
        if need_ctx:
            x_ctx = x_ctx + mod_c[2] * (mix_c @ w_out[l])
            x_ctx = x_ctx + mod_c[5] * conv_ffn(modulate(x_ctx, norm_g[l, 1], mod_c[3], mod_c[4]),
                                                ffn_up[l], ffn_conv_w[l], ffn_conv_b[l], ffn_down[l])
    return x_lat
```

```python
import functools

import jax
import jax.numpy as jnp
import numpy as np
from jax import lax
from jax.experimental import pallas as pl
from jax.experimental.pallas import tpu as pltpu

f32 = jnp.float32
bf16 = jnp.bfloat16

GRID_W = 64
HEAD_DIM = 64
LANES = 128
A_HEADS = 4
A_DIM = 128
A_CHUNK = 64
B_HEADS = 8
B_KV = 2
B_WIN = 128
C_HEADS = 8
C_KV = 2
D_HEADS = 8
NA_ROWS = 8
NA_COLS = 16
NA_QROWS = 4
NA_KROWS = 12
ROPE_THETA = 10000.0
EPS = 1e-6
NEG = -1e30
VMEM_LIMIT = 56 * 1024 * 1024


def _cparams(sem):
    return pltpu.CompilerParams(dimension_semantics=sem, vmem_limit_bytes=VMEM_LIMIT)


def _dot(a, b):
    return jnp.dot(a, b, preferred_element_type=f32)


def _dot_nt(a, b):
    return lax.dot_general(a, b, (((1,), (1,)), ((), ())), preferred_element_type=f32)


def _dot_tn(a, b):
    return lax.dot_general(a, b, (((0,), (0,)), ((), ())), preferred_element_type=f32)


def _modulated_norm(x, g, shift, scale):
    y = x * lax.rsqrt(jnp.mean(x * x, axis=-1, keepdims=True) + EPS)
    return (y * g) * (1.0 + scale) + shift


def _mod_spec(l, k, row, d):
    if row is None:
        return pl.BlockSpec((None, None, None, 1, d), lambda b, *_: (l, b, k, 0, 0))
    return pl.BlockSpec((None, None, None, 1, d), lambda b, *_: (l, row, k, 0, 0))


def _ada_kernel(c_ref, w_ref, b_ref, o_ref):
    c = c_ref[...]
    s = c / (1.0 + jnp.exp(-c))
    o_ref[...] = _dot(s.astype(bf16), w_ref[...].astype(bf16)) + b_ref[...]


def ada_modulation(cc, ada_w, ada_b):
    depth, d, n = ada_w.shape
    tn = 1536
    return pl.pallas_call(
        _ada_kernel,
        out_shape=jax.ShapeDtypeStruct((depth, cc.shape[0], n), f32),
        grid=(depth, n // tn),
        in_specs=[pl.BlockSpec(cc.shape, lambda l, j: (0, 0)),
                  pl.BlockSpec((None, d, tn), lambda l, j: (l, 0, j)),
                  pl.BlockSpec((None, 1, tn), lambda l, j: (l, 0, j))],
        out_specs=pl.BlockSpec((None, cc.shape[0], tn), lambda l, j: (l, 0, j)),
        compiler_params=_cparams(("parallel", "parallel")),
        name="ada",
    )(cc, ada_w, ada_b.reshape(depth, 1, n))


def _proj_kernel(x_ref, sh_ref, sc_ref, g_ref, w_ref, o_ref):
    h = _modulated_norm(x_ref[...], g_ref[...], sh_ref[...], sc_ref[...])
    o_ref[...] = _dot(h.astype(bf16), w_ref[...])


def modulated_proj(x, modt, l, row, g, w, tm):
    bsz, t, d = x.shape
    n = w.shape[1]
    return pl.pallas_call(
        _proj_kernel,
        out_shape=jax.ShapeDtypeStruct((bsz, t, n), f32),
        grid=(bsz, t // tm),
        in_specs=[pl.BlockSpec((None, tm, d), lambda b, i: (b, i, 0)),
                  _mod_spec(l, 0, row, d), _mod_spec(l, 1, row, d),
                  pl.BlockSpec((1, d), lambda b, i: (0, 0)),
                  pl.BlockSpec((d, n), lambda b, i: (0, 0))],
        out_specs=pl.BlockSpec((None, tm, n), lambda b, i: (b, i, 0)),
        compiler_params=_cparams(("parallel", "parallel")),
        name="proj",
    )(x, modt, modt, g.reshape(1, d), w)


def _head_ms(a):
    ri = lax.broadcasted_iota(jnp.int32, (LANES, LANES), 0) // HEAD_DIM
    ci = lax.broadcasted_iota(jnp.int32, (LANES, LANES), 1) // HEAD_DIM
    bd = jnp.where(ri == ci, 1.0, 0.0).astype(bf16)
    ss = a * a
    hi = ss.astype(bf16)
    lo = (ss - hi.astype(f32)).astype(bf16)
    return (_dot(hi, bd) + _dot(lo, bd)) * (1.0 / HEAD_DIM)


def _head_norm(a, g):
    return (a * lax.rsqrt(_head_ms(a) + EPS)) * g


def _rope(a, cos, sin):
    lane = lax.broadcasted_iota(jnp.int32, a.shape, 1)
    quarter = HEAD_DIM // 4
    rot = jnp.where(lane % (2 * quarter) < quarter,
                    -pltpu.roll(a, LANES - quarter, axis=1), pltpu.roll(a, quarter, axis=1))
    return a * cos + rot * sin


def _dup_halves(a):
    lane = lax.broadcasted_iota(jnp.int32, a.shape, 1)
    sw = pltpu.roll(a, HEAD_DIM, axis=1)
    lo = lane < HEAD_DIM
    return jnp.where(lo, a, sw), jnp.where(lo, sw, a)


def _prep_kernel(*refs, n_q, n_kpair, n_kfull, n_vfull, rope, scale):
    it = iter(refs)
    q_refs = [next(it) for _ in range(n_q)]
    kp_refs = [next(it) for _ in range(n_kpair)]
    kf_refs = [next(it) for _ in range(n_kfull)]
    vf_refs = [next(it) for _ in range(n_vfull)]
    g_refs = [next(it) for _ in range(n_q + n_kpair + n_kfull)]
    cos_ref, sin_ref = next(it), next(it)
    qo_refs = [next(it) for _ in range(n_q)]
    kpo_refs = [(next(it), next(it)) for _ in range(n_kpair)]
    kfo_refs = [next(it) for _ in range(n_kfull)]
    vfo_refs = [next(it) for _ in range(n_vfull)]
    gi = iter(g_refs)
    for qi, (q_ref, qo_ref) in enumerate(zip(q_refs, qo_refs)):
        g = next(gi)[...]
        for p in range(q_ref.shape[-1] // LANES):
            a = _head_norm(q_ref[:, p * LANES:(p + 1) * LANES], g)
            if rope[qi]:
                a = _rope(a, cos_ref[...], sin_ref[...])
            qo_ref[:, p * LANES:(p + 1) * LANES] = (a * scale).astype(bf16)
    for ki, (kp_ref, (ko_ref, vo_ref)) in enumerate(zip(kp_refs, kpo_refs)):
        g = next(gi)[...]
        k = _head_norm(kp_ref[:, :LANES], g)
        if rope[n_q + ki]:
            k = _rope(k, cos_ref[...], sin_ref[...])
        k0, k1 = _dup_halves(k)
        ko_ref[0] = k0.astype(bf16)
        ko_ref[1] = k1.astype(bf16)
        v0, v1 = _dup_halves(kp_ref[:, LANES:])
        vo_ref[0] = v0.astype(bf16)
        vo_ref[1] = v1.astype(bf16)
    for kf_ref, kfo_ref in zip(kf_refs, kfo_refs):
        g = next(gi)[...]
        for p in range(kf_ref.shape[-1] // LANES):
            kfo_ref[:, p * LANES:(p + 1) * LANES] = _head_norm(kf_ref[:, p * LANES:(p + 1) * LANES], g).astype(bf16)
    for vf_ref, vfo_ref in zip(vf_refs, vfo_refs):
        vfo_ref[...] = vf_ref[...].astype(bf16)


def qk_prep(y, *, q_blocks, kpair_blocks, kfull_blocks, vfull_blocks, gains, rope, cos, sin, tm):
    bsz, t, _ = y.shape
    wide, pair = 4 * LANES, 2 * LANES
    in_specs, args = [], []
    for cb in q_blocks + ():
        in_specs.append(pl.BlockSpec((None, tm, wide), functools.partial(lambda b, i, cb: (b, i, cb), cb=cb)))
        args.append(y)
    for cb in kpair_blocks:
        in_specs.append(pl.BlockSpec((None, tm, pair), functools.partial(lambda b, i, cb: (b, i, cb), cb=cb)))
        args.append(y)
    for cb in kfull_blocks + vfull_blocks:
        in_specs.append(pl.BlockSpec((None, tm, wide), functools.partial(lambda b, i, cb: (b, i, cb), cb=cb)))
        args.append(y)
    for g in gains:
        in_specs.append(pl.BlockSpec((1, LANES), lambda b, i: (0, 0)))
        args.append(jnp.tile(g, 2).reshape(1, LANES))
    for tbl in (cos, sin):
        in_specs.append(pl.BlockSpec((tm, LANES), lambda b, i: (i, 0)))
        args.append(tbl)
    out_shape, out_specs = [], []
    for _ in q_blocks:
        out_shape.append(jax.ShapeDtypeStruct((bsz, t, wide), bf16))
        out_specs.append(pl.BlockSpec((None, tm, wide), lambda b, i: (b, i, 0)))
    for _ in kpair_blocks:
        for _ in range(2):
            out_shape.append(jax.ShapeDtypeStruct((bsz, 2, t, LANES), bf16))
            out_specs.append(pl.BlockSpec((None, 2, tm, LANES), lambda b, i: (b, 0, i, 0)))
    for _ in kfull_blocks + vfull_blocks:
        out_shape.append(jax.ShapeDtypeStruct((bsz, t, wide), bf16))
        out_specs.append(pl.BlockSpec((None, tm, wide), lambda b, i: (b, i, 0)))
    kern = functools.partial(_prep_kernel, n_q=len(q_blocks), n_kpair=len(kpair_blocks), n_kfull=len(kfull_blocks),
                             n_vfull=len(vfull_blocks), rope=rope, scale=HEAD_DIM ** -0.5)
    return pl.pallas_call(
        kern, out_shape=out_shape, grid=(bsz, t // tm), in_specs=in_specs, out_specs=out_specs,
        compiler_params=_cparams(("parallel", "parallel")), name="qk_prep",
    )(*args)


def _stack_heads(q_ref, n_sub):
    parts = []
    for g in range(n_sub):
        blk = q_ref[:, (g // 2) * LANES:(g // 2 + 1) * LANES]
        lane = lax.broadcasted_iota(jnp.int32, blk.shape, 1)
        keep = (lane < HEAD_DIM) if g % 2 == 0 else (lane >= HEAD_DIM)
        parts.append(jnp.where(keep, blk, jnp.zeros_like(blk)))
    return jnp.concatenate(parts, axis=0)


def _softmax_attend(q, segs, sink):
    scores = []
    for k, _, bias, mask in segs:
        s = _dot_nt(q, k)
        if bias is not None:
            s = s + bias
        if mask is not None:
            s = jnp.where(mask, s, NEG)
        scores.append(s)
    m = scores[0].max(axis=-1, keepdims=True)
    for s in scores[1:]:
        m = jnp.maximum(m, s.max(axis=-1, keepdims=True))
    if sink is not None:
        m = jnp.maximum(m, sink)
    den = jnp.exp(sink - m) if sink is not None else None
    acc = None
    for s, (_, v, _, _) in zip(scores, segs):
        p = jnp.exp(s - m)
        ps = p.sum(axis=-1, keepdims=True)
        den = ps if den is None else den + ps
        o = _dot(p.astype(bf16), v)
        acc = o if acc is None else acc + o
    return acc / den


def _unstack_heads(o, o_ref, n_sub, tq):
    lane = lax.broadcasted_iota(jnp.int32, (tq, LANES), 1)
    for p in range(n_sub // 2):
        even = o[(2 * p) * tq:(2 * p + 1) * tq]
        odd = o[(2 * p + 1) * tq:(2 * p + 2) * tq]
        o_ref[:, p * LANES:(p + 1) * LANES] = jnp.where(lane < HEAD_DIM, even, odd).astype(o_ref.dtype)


def _window_attn_kernel(q_ref, kc_ref, vc_ref, kp_ref, k0_ref, kn_ref, vp_ref, v0_ref, vn_ref, sink_ref, o_ref, *, tq):
    i = pl.program_id(2)
    nb = pl.num_programs(2)
    n_sub = B_HEADS // B_KV
    q = _stack_heads(q_ref, n_sub)
    rows = n_sub * tq
    qpos = lax.broadcasted_iota(jnp.int32, (rows, tq), 0) % tq
    kpos = lax.broadcasted_iota(jnp.int32, (rows, tq), 1)
    m_prev = (kpos >= qpos) & (i > 0)
    m_next = (kpos <= qpos) & (i < nb - 1)
    segs = [(kc_ref[...], vc_ref[...], None, None),
            (kp_ref[...], vp_ref[...], None, m_prev),
            (k0_ref[...], v0_ref[...], None, None),
            (kn_ref[...], vn_ref[...], None, m_next)]
    o = _softmax_attend(q, segs, sink_ref[...])
    _unstack_heads(o, o_ref, n_sub, tq)


def window_attention(qn, k2, v2, kc2, vc2, sink_col):
    bsz, s, _ = qn.shape
    c = kc2.shape[2]
    tq = B_WIN
    nb = s // tq
    n_sub = B_HEADS // B_KV
    wq = n_sub * HEAD_DIM
    kv_spec = lambda fn: pl.BlockSpec((None, None, tq, LANES), fn)
    prev = lambda b, h, i: (b, h, jnp.maximum(i - 1, 0), 0)
    cur = lambda b, h, i: (b, h, i, 0)
    nxt = lambda b, h, i: (b, h, jnp.minimum(i + 1, nb - 1), 0)
    ctx_spec = pl.BlockSpec((None, None, c, LANES), lambda b, h, i: (b, h, 0, 0))
    return pl.pallas_call(
        functools.partial(_window_attn_kernel, tq=tq),
        out_shape=jax.ShapeDtypeStruct((bsz, s, B_HEADS * HEAD_DIM), bf16),
        grid=(bsz, B_KV, nb),
        in_specs=[pl.BlockSpec((None, tq, wq), lambda b, h, i: (b, i, h)),
                  ctx_spec, ctx_spec,
                  kv_spec(prev), kv_spec(cur), kv_spec(nxt),
                  kv_spec(prev), kv_spec(cur), kv_spec(nxt),
                  pl.BlockSpec((None, n_sub * tq, 1), lambda b, h, i: (h, 0, 0))],
        out_specs=pl.BlockSpec((None, tq, wq), lambda b, h, i: (b, i, h)),
        compiler_params=_cparams(("parallel", "parallel", "parallel")),
        name="window_attn",
    )(qn, kc2, vc2, k2, k2, k2, v2, v2, v2, sink_col)


def _seg_attn_kernel(*refs, n_seg, n_sub, tq, has_sink):
    q_ref = refs[0]
    kv = refs[1:1 + 2 * n_seg]
    sink = refs[1 + 2 * n_seg][...] if has_sink else None
    o_ref = refs[-1]
    q = _stack_heads(q_ref, n_sub)
    segs = [(kv[2 * j][...], kv[2 * j + 1][...], None, None) for j in range(n_seg)]
    o = _softmax_attend(q, segs, sink)
    _unstack_heads(o, o_ref, n_sub, tq)


def gqa_attention(qn, kv_segs, n_kv, tq, sink_col=None):
    bsz, s, width = qn.shape
    n_sub = width // HEAD_DIM // n_kv
    wq = n_sub * HEAD_DIM
    in_specs = [pl.BlockSpec((None, tq, wq), lambda b, h, i: (b, i, h))]
    args = [qn]
    for k2, v2 in kv_segs:
        n = k2.shape[2]
        spec = pl.BlockSpec((None, None, n, LANES), lambda b, h, i: (b, h, 0, 0))
        in_specs += [spec, spec]
        args += [k2, v2]
    if sink_col is not None:
        in_specs.append(pl.BlockSpec((None, n_sub * tq, 1), lambda b, h, i: (h, 0, 0)))
        args.append(sink_col)
    return pl.pallas_call(
        functools.partial(_seg_attn_kernel, n_seg=len(kv_segs), n_sub=n_sub, tq=tq, has_sink=sink_col is not None),
        out_shape=jax.ShapeDtypeStruct((bsz, s, width), bf16),
        grid=(bsz, n_kv, s // tq),
        in_specs=in_specs,
        out_specs=pl.BlockSpec((None, tq, wq), lambda b, h, i: (b, i, h)),
        compiler_params=_cparams(("parallel", "parallel", "parallel")),
        name="gqa_attn",
    )(*args)


def _na_attn_kernel(q_ref, kc_ref, vc_ref, k_ref, v_ref, bias_ref, o_ref, *, tq, n_grp):
    r = pl.program_id(1)
    rows_total = k_ref.shape[0] // GRID_W
    start = jnp.clip(r * NA_QROWS - NA_ROWS // 2, 0, rows_total - NA_KROWS)
    off = pl.multiple_of(start * GRID_W, GRID_W)
    nk = NA_KROWS * GRID_W
    q = _stack_heads(q_ref, 2)
    segs = [(kc_ref[...], vc_ref[...], None, None),
            (k_ref[pl.ds(off, nk), :], v_ref[pl.ds(off, nk), :], bias_ref[...], None)]
    o = _softmax_attend(q, segs, None)
    _unstack_heads(o, o_ref, 2, tq)


def neighbourhood_attention(qn, kn, vb, kcn, vcb, bias_tbl):
    bsz, s, width = qn.shape
    c = kcn.shape[1]
    tq = NA_QROWS * GRID_W
    n_grp = s // tq
    n_pair = width // LANES
    variant = lambda r: jnp.where(r == 0, 0, jnp.where(r == n_grp - 1, 2, 1))
    return pl.pallas_call(
        functools.partial(_na_attn_kernel, tq=tq, n_grp=n_grp),
        out_shape=jax.ShapeDtypeStruct((bsz, s, width), bf16),
        grid=(n_pair, n_grp, bsz),
        in_specs=[pl.BlockSpec((None, tq, LANES), lambda p, r, b: (b, r, p)),
                  pl.BlockSpec((None, c, LANES), lambda p, r, b: (b, 0, p)),
                  pl.BlockSpec((None, c, LANES), lambda p, r, b: (b, 0, p)),
                  pl.BlockSpec((None, s, LANES), lambda p, r, b: (b, 0, p)),
                  pl.BlockSpec((None, s, LANES), lambda p, r, b: (b, 0, p)),
                  pl.BlockSpec((None, None, 2 * tq, NA_KROWS * GRID_W), lambda p, r, b: (variant(r), p, 0, 0))],
        out_specs=pl.BlockSpec((None, tq, LANES), lambda p, r, b: (b, r, p)),
        compiler_params=_cparams(("parallel", "parallel", "parallel")),
        name="na_attn",
    )(qn, kcn, vcb, kn, vb, bias_tbl)


def na_bias_table(rpb, rows_total):
    h = rpb.shape[0]
    n_grp = rows_total // NA_QROWS
    tabs = []
    for r0 in (0, NA_QROWS, (n_grp - 1) * NA_QROWS):
        start = int(np.clip(r0 - NA_ROWS // 2, 0, rows_total - NA_KROWS))
        qr = r0 + np.arange(NA_QROWS)
        kr = start + np.arange(NA_KROWS)
        ws = np.clip(qr - NA_ROWS // 2, 0, rows_total - NA_ROWS)
        row_ok = (kr[None, :] >= ws[:, None]) & (kr[None, :] < ws[:, None] + NA_ROWS)
        dr = np.clip(kr[None, :] - qr[:, None] + NA_ROWS - 1, 0, 2 * NA_ROWS - 2)
        col = np.arange(GRID_W)
        cs = np.clip(col - NA_COLS // 2, 0, GRID_W - NA_COLS)
        col_ok = (col[None, :] >= cs[:, None]) & (col[None, :] < cs[:, None] + NA_COLS)
        dc = np.clip(col[None, :] - col[:, None] + NA_COLS - 1, 0, 2 * NA_COLS - 2)
        b = rpb[:, dr[:, None, :, None], dc[None, :, None, :]].astype(f32)
        ok = row_ok[:, None, :, None] & col_ok[None, :, None, :]
        b = jnp.where(ok[None], b, NEG)
        tabs.append(b.reshape(h // 2, 2 * NA_QROWS * GRID_W, NA_KROWS * GRID_W))
    return jnp.stack(tabs)


def _log_sigmoid(x):
    return jnp.minimum(x, 0.0) - jnp.log1p(jnp.exp(-jnp.abs(x)))


def _mlstm_chunk(c_ref, n_ref, m_ref, d, q, k, v, li, lf, backward):
    L = q.shape[0]
    t_idx = lax.broadcasted_iota(jnp.int32, (L, L), 0)
    s_idx = lax.broadcasted_iota(jnp.int32, (L, L), 1)
    tri = (s_idx >= t_idx) if backward else (s_idx <= t_idx)
    eye = s_idx == t_idx
    C, n, m = c_ref[d], n_ref[d], m_ref[d]
    b_col = jnp.sum(jnp.where(tri, lf, 0.0), axis=1, keepdims=True)
    b_row = jnp.sum(jnp.where(eye, b_col, 0.0), axis=0, keepdims=True)
    b_last = jnp.sum(lf, axis=1, keepdims=True)
    dmat = jnp.where(tri, b_col - b_row + li, -jnp.inf)
    inter = b_col + m
    m_t = jnp.maximum(inter, jnp.max(dmat, axis=1, keepdims=True))
    w = jnp.exp(dmat - m_t)
    a = jnp.exp(inter - m_t)
    qb, kb, vb = q.astype(bf16), k.astype(bf16), v.astype(bf16)
    s = _dot_nt(qb, kb) * w
    num = a * _dot(qb, C.astype(bf16)) + _dot(s.astype(bf16), vb)
    den = a * jnp.sum(q * n, axis=1, keepdims=True) + jnp.sum(s, axis=1, keepdims=True)
    h = num / jnp.maximum(jnp.abs(den), jnp.exp(-m_t))
    g_row = b_last - b_row + li
    m_new = jnp.maximum(b_last + m, jnp.max(g_row, axis=1, keepdims=True))
    decay = jnp.exp(b_last + m - m_new)
    wg_col = jnp.sum(jnp.where(eye, jnp.exp(g_row - m_new), 0.0), axis=1, keepdims=True)
    wk = wg_col * k
    c_ref[d] = decay * C + _dot_tn(wk.astype(bf16), vb)
    n_ref[d] = decay * n + jnp.sum(wk, axis=0, keepdims=True)
    m_ref[d] = m_new
    return h


def _mlstm_kernel(gb_ref, qc_ref, kc_ref, vc_ref, oc_ref, gc_ref, ql_ref, kl_ref, vl_ref, ol_ref, gl_ref, hg_ref,
                  outc_ref, outl_ref, c_scr, n_scr, m_scr, hfc, hbc, hfl, hbl):
    hd = pl.program_id(1)
    L = A_CHUNK
    c_scr[...] = jnp.zeros_like(c_scr)
    n_scr[...] = jnp.zeros_like(n_scr)
    m_scr[...] = jnp.zeros_like(m_scr)
    bias = [gb_ref[gi * A_HEADS + hd] for gi in range(4)]
    kscale = A_DIM ** -0.5

    def run(q_ref, k_ref, v_ref, g_ref, hf_ref, hb_ref):
        nc = q_ref.shape[0] // L

        def body(j, carry):
            for d, (cj, ig, fg, h_ref) in enumerate(((j, 0, 1, hf_ref), (nc - 1 - j, 2, 3, hb_ref))):
                rows = pl.ds(pl.multiple_of(cj * L, L), L)
                li = g_ref[ig, pl.ds(cj, 1), :] + bias[ig]
                lf = _log_sigmoid(g_ref[fg, pl.ds(cj, 1), :] + bias[fg])
                h_ref[rows, :] = _mlstm_chunk(c_scr, n_scr, m_scr, d, q_ref[rows, :], k_ref[rows, :] * kscale,
                                              v_ref[rows, :], li, lf, backward=(d == 1))
            return carry

        lax.fori_loop(0, nc, body, 0)

    run(qc_ref, kc_ref, vc_ref, gc_ref, hfc, hbc)
    run(ql_ref, kl_ref, vl_ref, gl_ref, hfl, hbl)
    for o_ref, hf_ref, hb_ref, out_ref in ((oc_ref, hfc, hbc, outc_ref), (ol_ref, hfl, hbl, outl_ref)):
        h = hf_ref[...] + hb_ref[...]
        hn = (h * lax.rsqrt(jnp.mean(h * h, axis=-1, keepdims=True) + EPS)) * hg_ref[...]
        o = o_ref[...]
        out_ref[...] = (hn / (1.0 + jnp.exp(-o))).astype(out_ref.dtype)


def mlstm_mixer(y_ctx, y_lat, gates_ctx, gates_lat, gate_b, head_g):
    bsz, tc, _ = y_ctx.shape
    tl = y_lat.shape[1]
    L = A_CHUNK

    def col(t, base):
        return pl.BlockSpec((None, t, A_DIM), functools.partial(lambda b, h, base: (b, 0, base + h), base=base))

    def gate_spec(t):
        return pl.BlockSpec((None, None, 4, t // L, L), lambda b, h: (b, h, 0, 0, 0))

    def out_spec(t):
        return pl.BlockSpec((None, t, A_DIM), lambda b, h: (b, 0, h))

    return pl.pallas_call(
        _mlstm_kernel,
        out_shape=[jax.ShapeDtypeStruct((bsz, tc, A_HEADS * A_DIM), bf16),
                   jax.ShapeDtypeStruct((bsz, tl, A_HEADS * A_DIM), bf16)],
        grid=(bsz, A_HEADS),
        in_specs=[pl.BlockSpec(memory_space=pltpu.SMEM)]
        + [col(tc, 4 * j) for j in range(4)] + [gate_spec(tc)]
        + [col(tl, 4 * j) for j in range(4)] + [gate_spec(tl)]
        + [pl.BlockSpec((None, 1, A_DIM), lambda b, h: (h, 0, 0))],
        out_specs=[out_spec(tc), out_spec(tl)],
        scratch_shapes=[pltpu.VMEM((2, A_DIM, A_DIM), f32), pltpu.VMEM((2, 1, A_DIM), f32), pltpu.VMEM((2, 1, 1), f32),
                        pltpu.VMEM((tc, A_DIM), f32), pltpu.VMEM((tc, A_DIM), f32),
                        pltpu.VMEM((tl, A_DIM), f32), pltpu.VMEM((tl, A_DIM), f32)],
        compiler_params=_cparams(("parallel", "parallel")),
        name="mlstm",
    )(gate_b, y_ctx, y_ctx, y_ctx, y_ctx, gates_ctx, y_lat, y_lat, y_lat, y_lat, gates_lat,
      head_g.reshape(A_HEADS, 1, A_DIM))


def _outproj_kernel(x_ref, a1_ref, a2_ref, w1_ref, w2_ref, gate_ref, o_ref):
    y = _dot(a1_ref[...], w1_ref[...]) + _dot(a2_ref[...], w2_ref[...])
    o_ref[...] = x_ref[...] + gate_ref[...] * y


def out_proj(x, a1, a2, w, modt, l, row, tm):
    bsz, t, d = x.shape
    k1, k2 = a1.shape[-1], a2.shape[-1]
    return pl.pallas_call(
        _outproj_kernel,
        out_shape=jax.ShapeDtypeStruct((bsz, t, d), f32),
        grid=(bsz, t // tm),
        in_specs=[pl.BlockSpec((None, tm, d), lambda b, i: (b, i, 0)),
                  pl.BlockSpec((None, tm, k1), lambda b, i: (b, i, 0)),
                  pl.BlockSpec((None, tm, k2), lambda b, i: (b, i, 0)),
                  pl.BlockSpec((k1, d), lambda b, i: (0, 0)),
                  pl.BlockSpec((k2, d), lambda b, i: (0, 0)),
                  _mod_spec(l, 2, row, d)],
        out_specs=pl.BlockSpec((None, tm, d), lambda b, i: (b, i, 0)),
        compiler_params=_cparams(("parallel", "parallel")),
        name="out_proj",
    )(x, a1, a2, w[:k1], w[k1:], modt)


FFN_HALO = 8


def _ffn_kernel(x_ref, xp_ref, xn_ref, sh_ref, sc_ref, gate_ref, g_ref, wg_ref, wv_ref, cwg_ref, cwv_ref,
                cbg_ref, cbv_ref, wd_ref, o_ref, h_scr, acc_scr, *, tm):
    i, j = pl.program_id(1), pl.program_id(2)
    rows = tm + 2 * FFN_HALO

    @pl.when(j == 0)
    def _():
        g, sh, sc = g_ref[...], sh_ref[...], sc_ref[...]
        hp = _modulated_norm(xp_ref[...], g, sh, sc)
        hn = _modulated_norm(xn_ref[...], g, sh, sc)
        h_scr[:FFN_HALO] = jnp.where(i > 0, hp, 0.0).astype(bf16)
        h_scr[FFN_HALO:FFN_HALO + tm] = _modulated_norm(x_ref[...], g, sh, sc).astype(bf16)
        h_scr[FFN_HALO + tm:] = jnp.where(i < pl.num_programs(1) - 1, hn, 0.0).astype(bf16)
        acc_scr[...] = jnp.zeros_like(acc_scr)

    h = h_scr[...]

    def conv(u, cw_ref, cb_ref):
        cw = cw_ref[...]
        prev = pltpu.roll(u, 1, axis=0)[FFN_HALO:FFN_HALO + tm]
        nxt = pltpu.roll(u, rows - 1, axis=0)[FFN_HALO:FFN_HALO + tm]
        return cw[0:1] * prev + cw[1:2] * u[FFN_HALO:FFN_HALO + tm] + cw[2:3] * nxt + cb_ref[...]

    cg = conv(_dot(h, wg_ref[...]), cwg_ref, cbg_ref)
    cv = conv(_dot(h, wv_ref[...]), cwv_ref, cbv_ref)
    act = (cg / (1.0 + jnp.exp(-cg))) * cv
    acc_scr[...] += _dot(act.astype(bf16), wd_ref[...])

    @pl.when(j == pl.num_programs(2) - 1)
    def _():
        o_ref[...] = x_ref[...] + gate_ref[...] * acc_scr[...]


def conv_ffn(x, modt, l, row, g, w_up, conv_w, conv_b, w_down, tm, nt):
    bsz, t, d = x.shape
    dff = w_down.shape[0]
    nj = dff // nt
    hb = tm // FFN_HALO
    last = t // FFN_HALO - 1
    cb = conv_b.reshape(1, 2 * dff)
    return pl.pallas_call(
        functools.partial(_ffn_kernel, tm=tm),
        out_shape=jax.ShapeDtypeStruct((bsz, t, d), f32),
        grid=(bsz, t // tm, nj),
        in_specs=[pl.BlockSpec((None, tm, d), lambda b, i, j: (b, i, 0)),
                  pl.BlockSpec((None, FFN_HALO, d), lambda b, i, j: (b, jnp.maximum(i * hb - 1, 0), 0)),
                  pl.BlockSpec((None, FFN_HALO, d), lambda b, i, j: (b, jnp.minimum((i + 1) * hb, last), 0)),
                  _mod_spec(l, 3, row, d), _mod_spec(l, 4, row, d), _mod_spec(l, 5, row, d),
                  pl.BlockSpec((1, d), lambda b, i, j: (0, 0)),
                  pl.BlockSpec((d, nt), lambda b, i, j: (0, j)),
                  pl.BlockSpec((d, nt), lambda b, i, j: (0, nj + j)),
                  pl.BlockSpec((3, nt), lambda b, i, j: (0, j)),
                  pl.BlockSpec((3, nt), lambda b, i, j: (0, nj + j)),
                  pl.BlockSpec((1, nt), lambda b, i, j: (0, j)),
                  pl.BlockSpec((1, nt), lambda b, i, j: (0, nj + j)),
                  pl.BlockSpec((nt, d), lambda b, i, j: (j, 0))],
        out_specs=pl.BlockSpec((None, tm, d), lambda b, i, j: (b, i, 0)),
        scratch_shapes=[pltpu.VMEM((tm + 2 * FFN_HALO, d), bf16), pltpu.VMEM((tm, d), f32)],
        compiler_params=_cparams(("parallel", "parallel", "arbitrary")),
        name="conv_ffn",
    )(x, x, x, modt, modt, modt, g.reshape(1, d), w_up, w_up, conv_w, conv_w, cb, cb, w_down)


def _rope_tables(n):
    t = jnp.arange(n)
    row = (t // GRID_W).astype(f32)
    colp = (t % GRID_W).astype(f32)
    half = HEAD_DIM // 2
    freq = ROPE_THETA ** (-jnp.arange(0, half, 2, dtype=f32) / half)
    ang_r = row[:, None] * freq[None, :]
    ang_c = colp[:, None] * freq[None, :]
    ang = jnp.concatenate([ang_r, ang_r, ang_c, ang_c] * 2, axis=-1)
    return jnp.cos(ang), jnp.sin(ang)


def _gate_layout(y, col0):
    bsz, t, _ = y.shape
    g = y[:, :, col0:col0 + 4 * A_HEADS].reshape(bsz, t // A_CHUNK, A_CHUNK, 4, A_HEADS)
    return jnp.transpose(g, (0, 4, 3, 1, 2))


def _sink_column(sink, n_kv, tq):
    n_sub = sink.shape[0] // n_kv
    return jnp.repeat(sink.reshape(n_kv, n_sub), tq, axis=1)[..., None].astype(f32)


def kernel(x, c, ctx, c_ctx, ada_w, ada_b, norm_g, w_out, ffn_up, ffn_conv_w, ffn_conv_b, ffn_down, even_w_in,
           mlstm_gate_b, mlstm_head_g, swa_qk_g, swa_sink, odd_w_in, gqa_qk_g, na_qk_g, na_rpb):
    bsz, seq, d = x.shape
    n_ctx = ctx.shape[1]
    depth = ada_w.shape[0]
    a_w = A_HEADS * A_DIM
    cos, sin = _rope_tables(seq)
    ones_c, zeros_c = jnp.ones((n_ctx, LANES), f32), jnp.zeros((n_ctx, LANES), f32)

    cc = jnp.zeros((16, d), f32).at[:bsz].set(c).at[bsz].set(c_ctx)
    modt = ada_modulation(cc, ada_w, ada_b).reshape(depth, 16, 6, 1, d)

    x_lat, x_ctx = x, ctx
    tm_l, tm_c = 256, 256
    for l in range(depth):
        need_ctx = l < depth - 1
        w_o = w_out[l].astype(bf16)
        if l % 2 == 0:
            e = l // 2
            wi = even_w_in[e]
            sp = np.cumsum((0, a_w, a_w, a_w, a_w, 4 * A_HEADS, B_HEADS * HEAD_DIM, B_KV * HEAD_DIM, B_KV * HEAD_DIM))
            aq, ak, av, ao, ag, bq, bk, bv = (wi[:, sp[k]:sp[k + 1]] for k in range(8))
            w = jnp.concatenate([aq, ak, av, ao, bq, bk, bv, ag, jnp.zeros((d, LANES - 4 * A_HEADS), f32)],
                                axis=1).astype(bf16)
            gate_col = 4 * a_w + (B_HEADS + 2 * B_KV) * HEAD_DIM
            y_lat = modulated_proj(x_lat, modt, l, None, norm_g[l, 0], w, tm_l)
            y_ctx = modulated_proj(x_ctx, modt, l, bsz, norm_g[l, 0], w, tm_c)
            a_ctx, a_lat = mlstm_mixer(y_ctx, y_lat, _gate_layout(y_ctx, gate_col), _gate_layout(y_lat, gate_col),
                                       mlstm_gate_b[e], mlstm_head_g[e])
            gq, gk = swa_qk_g[e, 0], swa_qk_g[e, 1]
            qn_l, k2_l, v2_l = qk_prep(y_lat, q_blocks=(4,), kpair_blocks=(10,), kfull_blocks=(), vfull_blocks=(),
                                       gains=(gq, gk), rope=(True, True), cos=cos, sin=sin, tm=tm_l)
            qn_c, k2_c, v2_c = qk_prep(y_ctx, q_blocks=(4,), kpair_blocks=(10,), kfull_blocks=(), vfull_blocks=(),
                                       gains=(gq, gk), rope=(False, False), cos=ones_c, sin=zeros_c, tm=tm_c)
            b_lat = window_attention(qn_l, k2_l, v2_l, k2_c, v2_c, _sink_column(swa_sink[e], B_KV, B_WIN))
            mix_l = (a_lat, b_lat)
            if need_ctx:
                b_ctx = gqa_attention(qn_c, [(k2_c, v2_c)], B_KV, 128, _sink_column(swa_sink[e], B_KV, 128))
                mix_c = (a_ctx, b_ctx)
        else:
            o = l // 2
            wi = odd_w_in[o]
            sp = np.cumsum((0, C_HEADS * HEAD_DIM, C_KV * HEAD_DIM, C_KV * HEAD_DIM) + (D_HEADS * HEAD_DIM,) * 3)
            cq, ck, cv, nq, nk, nv = (wi[:, sp[k]:sp[k + 1]] for k in range(6))
            w_ctx = jnp.concatenate([nk, nv, ck, cv], axis=1).astype(bf16)
            w = jnp.concatenate([cq.astype(bf16), nq.astype(bf16), w_ctx], axis=1)
            y_lat = modulated_proj(x_lat, modt, l, None, norm_g[l, 0], w, tm_l)
            gq, gk = gqa_qk_g[o, 0], gqa_qk_g[o, 1]
            nq_g, nk_g = na_qk_g[o, 0], na_qk_g[o, 1]
            cqn, nqn, ck2, cv2, nkn, nvb = qk_prep(
                y_lat, q_blocks=(0, 1), kpair_blocks=(8,), kfull_blocks=(2,), vfull_blocks=(3,),
                gains=(gq, nq_g, gk, nk_g), rope=(True, False, True), cos=cos, sin=sin, tm=tm_l)
            if need_ctx:
                raise NotImplementedError("context outputs of an odd layer")
            y_ctx = modulated_proj(x_ctx, modt, l, bsz, norm_g[l, 0], w_ctx, tm_c)
            ck2_c, cv2_c, nkn_c, nvb_c = qk_prep(
                y_ctx, q_blocks=(), kpair_blocks=(4,), kfull_blocks=(0,), vfull_blocks=(1,),
                gains=(gk, nk_g), rope=(False,), cos=ones_c, sin=zeros_c, tm=tm_c)
            c_lat = gqa_attention(cqn, [(ck2_c, cv2_c), (ck2, cv2)], C_KV, 128)
            d_lat = neighbourhood_attention(nqn, nkn, nvb, nkn_c, nvb_c, na_bias_table(na_rpb[o], seq // GRID_W))
            mix_l = (c_lat, d_lat)

        ffn = (norm_g[l, 1], ffn_up[l].astype(bf16), ffn_conv_w[l], ffn_conv_b[l], ffn_down[l].astype(bf16))
        x_lat = out_proj(x_lat, mix_l[0], mix_l[1], w_o, modt, l, None, tm_l)
        x_lat = conv_ffn(x_lat, modt, l, None, *ffn, tm=512, nt=256)
        if need_ctx:
            x_ctx = out_proj(x_ctx, mix_c[0], mix_c[1], w_o, modt, l, bsz, tm_c)
            x_ctx = conv_ffn(x_ctx, modt, l, bsz, *ffn, tm=256, nt=256)
    return x_lat
```

```python
import functools

import jax
import jax.numpy as jnp
import numpy as np
from jax import lax
from jax.experimental import pallas as pl
from jax.experimental.pallas import tpu as pltpu

f32 = jnp.float32
bf16 = jnp.bfloat16

GRID_W = 64
HEAD_DIM = 64
LANES = 128
A_HEADS = 4
A_DIM = 128
A_CHUNK = 64
B_HEADS = 8
B_KV = 2
B_WIN = 128
C_HEADS = 8
C_KV = 2
D_HEADS = 8
NA_ROWS = 8
NA_COLS = 16
NA_QROWS = 4
NA_KROWS = 12
ROPE_THETA = 10000.0
EPS = 1e-6
NEG = -1e30
VMEM_LIMIT = 56 * 1024 * 1024


def _cparams(sem):
    return pltpu.CompilerParams(dimension_semantics=sem, vmem_limit_bytes=VMEM_LIMIT)


def _dot(a, b):
    return jnp.dot(a, b, preferred_element_type=f32)


def _dot_nt(a, b):
    return lax.dot_general(a, b, (((1,), (1,)), ((), ())), preferred_element_type=f32)


def _dot_tn(a, b):
    return lax.dot_general(a, b, (((0,), (0,)), ((), ())), preferred_element_type=f32)


def _modulated_norm(x, g, shift, scale):
    y = x * lax.rsqrt(jnp.mean(x * x, axis=-1, keepdims=True) + EPS)
    return (y * g) * (1.0 + scale) + shift


def _mod_spec(l, k, row, d):
    if row is None:
        return pl.BlockSpec((None, None, None, 1, d), lambda b, *_: (l, b, k, 0, 0))
    return pl.BlockSpec((None, None, None, 1, d), lambda b, *_: (l, row, k, 0, 0))


def _ada_kernel(c_ref, w_ref, b_ref, o_ref):
    c = c_ref[...]
    s = c / (1.0 + jnp.exp(-c))
    o_ref[...] = _dot(s.astype(bf16), w_ref[...].astype(bf16)) + b_ref[...]


def ada_modulation(cc, ada_w, ada_b):
    depth, d, n = ada_w.shape
    tn = 1536
    return pl.pallas_call(
        _ada_kernel,
        out_shape=jax.ShapeDtypeStruct((depth, cc.shape[0], n), f32),
        grid=(depth, n // tn),
        in_specs=[pl.BlockSpec(cc.shape, lambda l, j: (0, 0)),
                  pl.BlockSpec((None, d, tn), lambda l, j: (l, 0, j)),
                  pl.BlockSpec((None, 1, tn), lambda l, j: (l, 0, j))],
        out_specs=pl.BlockSpec((None, cc.shape[0], tn), lambda l, j: (l, 0, j)),
        compiler_params=_cparams(("parallel", "parallel")),
        name="ada",
    )(cc, ada_w, ada_b.reshape(depth, 1, n))


def _proj_kernel(x_ref, sh_ref, sc_ref, g_ref, w_ref, o_ref):
    h = _modulated_norm(x_ref[...], g_ref[...], sh_ref[...], sc_ref[...])
    o_ref[...] = _dot(h.astype(bf16), w_ref[...])


def modulated_proj(x, modt, l, row, g, w, tm):
    bsz, t, d = x.shape
    n = w.shape[1]
    return pl.pallas_call(
        _proj_kernel,
        out_shape=jax.ShapeDtypeStruct((bsz, t, n), f32),
        grid=(bsz, t // tm),
        in_specs=[pl.BlockSpec((None, tm, d), lambda b, i: (b, i, 0)),
                  _mod_spec(l, 0, row, d), _mod_spec(l, 1, row, d),
                  pl.BlockSpec((1, d), lambda b, i: (0, 0)),
                  pl.BlockSpec((d, n), lambda b, i: (0, 0))],
        out_specs=pl.BlockSpec((None, tm, n), lambda b, i: (b, i, 0)),
        compiler_params=_cparams(("parallel", "parallel")),
        name="proj",
    )(x, modt, modt, g.reshape(1, d), w)


def _head_ms(a):
    ri = lax.broadcasted_iota(jnp.int32, (LANES, LANES), 0) // HEAD_DIM
    ci = lax.broadcasted_iota(jnp.int32, (LANES, LANES), 1) // HEAD_DIM
    bd = jnp.where(ri == ci, 1.0, 0.0).astype(bf16)
    ss = a * a
    hi = ss.astype(bf16)
    lo = (ss - hi.astype(f32)).astype(bf16)
    return (_dot(hi, bd) + _dot(lo, bd)) * (1.0 / HEAD_DIM)


def _head_norm(a, g):
    return (a * lax.rsqrt(_head_ms(a) + EPS)) * g


def _rope(a, cos, sin):
    lane = lax.broadcasted_iota(jnp.int32, a.shape, 1)
    quarter = HEAD_DIM // 4
    rot = jnp.where(lane % (2 * quarter) < quarter,
                    -pltpu.roll(a, LANES - quarter, axis=1), pltpu.roll(a, quarter, axis=1))
    return a * cos + rot * sin


def _dup_halves(a):
    lane = lax.broadcasted_iota(jnp.int32, a.shape, 1)
    sw = pltpu.roll(a, HEAD_DIM, axis=1)
    lo = lane < HEAD_DIM
    return jnp.where(lo, a, sw), jnp.where(lo, sw, a)


def _prep_kernel(*refs, n_q, n_kpair, n_kfull, n_vfull, rope, scale):
    it = iter(refs)
    q_refs = [next(it) for _ in range(n_q)]
    kp_refs = [next(it) for _ in range(n_kpair)]
    kf_refs = [next(it) for _ in range(n_kfull)]
    vf_refs = [next(it) for _ in range(n_vfull)]
    g_refs = [next(it) for _ in range(n_q + n_kpair + n_kfull)]
    cos_ref, sin_ref = next(it), next(it)
    qo_refs = [next(it) for _ in range(n_q)]
    kpo_refs = [(next(it), next(it)) for _ in range(n_kpair)]
    kfo_refs = [next(it) for _ in range(n_kfull)]
    vfo_refs = [next(it) for _ in range(n_vfull)]
    gi = iter(g_refs)
    for qi, (q_ref, qo_ref) in enumerate(zip(q_refs, qo_refs)):
        g = next(gi)[...]
        for p in range(q_ref.shape[-1] // LANES):
            a = _head_norm(q_ref[:, p * LANES:(p + 1) * LANES], g)
            if rope[qi]:
                a = _rope(a, cos_ref[...], sin_ref[...])
            qo_ref[:, p * LANES:(p + 1) * LANES] = (a * scale).astype(bf16)
    for ki, (kp_ref, (ko_ref, vo_ref)) in enumerate(zip(kp_refs, kpo_refs)):
        g = next(gi)[...]
        k = _head_norm(kp_ref[:, :LANES], g)
        if rope[n_q + ki]:
            k = _rope(k, cos_ref[...], sin_ref[...])
        k0, k1 = _dup_halves(k)
        ko_ref[0] = k0.astype(bf16)
        ko_ref[1] = k1.astype(bf16)
        v0, v1 = _dup_halves(kp_ref[:, LANES:])
        vo_ref[0] = v0.astype(bf16)
        vo_ref[1] = v1.astype(bf16)
    for kf_ref, kfo_ref in zip(kf_refs, kfo_refs):
        g = next(gi)[...]
        for p in range(kf_ref.shape[-1] // LANES):
            kfo_ref[:, p * LANES:(p + 1) * LANES] = _head_norm(kf_ref[:, p * LANES:(p + 1) * LANES], g).astype(bf16)
    for vf_ref, vfo_ref in zip(vf_refs, vfo_refs):
        vfo_ref[...] = vf_ref[...].astype(bf16)


def qk_prep(y, *, q_blocks, kpair_blocks, kfull_blocks, vfull_blocks, gains, rope, cos, sin, tm):
    bsz, t, _ = y.shape
    wide, pair = 4 * LANES, 2 * LANES
    in_specs, args = [], []
    for cb in q_blocks + ():
        in_specs.append(pl.BlockSpec((None, tm, wide), functools.partial(lambda b, i, cb: (b, i, cb), cb=cb)))
        args.append(y)
    for cb in kpair_blocks:
        in_specs.append(pl.BlockSpec((None, tm, pair), functools.partial(lambda b, i, cb: (b, i, cb), cb=cb)))
        args.append(y)
    for cb in kfull_blocks + vfull_blocks:
        in_specs.append(pl.BlockSpec((None, tm, wide), functools.partial(lambda b, i, cb: (b, i, cb), cb=cb)))
        args.append(y)
    for g in gains:
        in_specs.append(pl.BlockSpec((1, LANES), lambda b, i: (0, 0)))
        args.append(jnp.tile(g, 2).reshape(1, LANES))
    for tbl in (cos, sin):
        in_specs.append(pl.BlockSpec((tm, LANES), lambda b, i: (i, 0)))
        args.append(tbl)
    out_shape, out_specs = [], []
    for _ in q_blocks:
        out_shape.append(jax.ShapeDtypeStruct((bsz, t, wide), bf16))
        out_specs.append(pl.BlockSpec((None, tm, wide), lambda b, i: (b, i, 0)))
    for _ in kpair_blocks:
        for _ in range(2):
            out_shape.append(jax.ShapeDtypeStruct((bsz, 2, t, LANES), bf16))
            out_specs.append(pl.BlockSpec((None, 2, tm, LANES), lambda b, i: (b, 0, i, 0)))
    for _ in kfull_blocks + vfull_blocks:
        out_shape.append(jax.ShapeDtypeStruct((bsz, t, wide), bf16))
        out_specs.append(pl.BlockSpec((None, tm, wide), lambda b, i: (b, i, 0)))
    kern = functools.partial(_prep_kernel, n_q=len(q_blocks), n_kpair=len(kpair_blocks), n_kfull=len(kfull_blocks),
                             n_vfull=len(vfull_blocks), rope=rope, scale=HEAD_DIM ** -0.5)
    return pl.pallas_call(
        kern, out_shape=out_shape, grid=(bsz, t // tm), in_specs=in_specs, out_specs=out_specs,
        compiler_params=_cparams(("parallel", "parallel")), name="qk_prep",
    )(*args)


def _stack_heads(q_ref, n_sub):
    parts = []
    for g in range(n_sub):
        blk = q_ref[:, (g // 2) * LANES:(g // 2 + 1) * LANES]
        lane = lax.broadcasted_iota(jnp.int32, blk.shape, 1)
        keep = (lane < HEAD_DIM) if g % 2 == 0 else (lane >= HEAD_DIM)
        parts.append(jnp.where(keep, blk, jnp.zeros_like(blk)))
    return jnp.concatenate(parts, axis=0)


def _softmax_attend(q, segs, sink):
    scores = []
    for k, _, bias, mask in segs:
        s = _dot_nt(q, k)
        if bias is not None:
            s = s + bias
        if mask is not None:
            s = jnp.where(mask, s, NEG)
        scores.append(s)
    m = scores[0].max(axis=-1, keepdims=True)
    for s in scores[1:]:
        m = jnp.maximum(m, s.max(axis=-1, keepdims=True))
    if sink is not None:
        m = jnp.maximum(m, sink)
    den = jnp.exp(sink - m) if sink is not None else None
    acc = None
    for s, (_, v, _, _) in zip(scores, segs):
        p = jnp.exp(s - m)
        ps = p.sum(axis=-1, keepdims=True)
        den = ps if den is None else den + ps
        o = _dot(p.astype(bf16), v)
        acc = o if acc is None else acc + o
    return acc / den


def _unstack_heads(o, o_ref, n_sub, tq):
    lane = lax.broadcasted_iota(jnp.int32, (tq, LANES), 1)
    for p in range(n_sub // 2):
        even = o[(2 * p) * tq:(2 * p + 1) * tq]
        odd = o[(2 * p + 1) * tq:(2 * p + 2) * tq]
        o_ref[:, p * LANES:(p + 1) * LANES] = jnp.where(lane < HEAD_DIM, even, odd).astype(o_ref.dtype)


def _window_attn_kernel(q_ref, kc_ref, vc_ref, kp_ref, k0_ref, kn_ref, vp_ref, v0_ref, vn_ref, sink_ref, o_ref, *, tq):
    i = pl.program_id(2)
    nb = pl.num_programs(2)
    n_sub = B_HEADS // B_KV
    q = _stack_heads(q_ref, n_sub)
    rows = n_sub * tq
    qpos = lax.broadcasted_iota(jnp.int32, (rows, tq), 0) % tq
    kpos = lax.broadcasted_iota(jnp.int32, (rows, tq), 1)
    m_prev = (kpos >= qpos) & (i > 0)
    m_next = (kpos <= qpos) & (i < nb - 1)
    segs = [(kc_ref[...], vc_ref[...], None, None),
            (kp_ref[...], vp_ref[...], None, m_prev),
            (k0_ref[...], v0_ref[...], None, None),
            (kn_ref[...], vn_ref[...], None, m_next)]
    o = _softmax_attend(q, segs, sink_ref[...])
    _unstack_heads(o, o_ref, n_sub, tq)


def window_attention(qn, k2, v2, kc2, vc2, sink_col):
    bsz, s, _ = qn.shape
    c = kc2.shape[2]
    tq = B_WIN
    nb = s // tq
    n_sub = B_HEADS // B_KV
    wq = n_sub * HEAD_DIM
    kv_spec = lambda fn: pl.BlockSpec((None, None, tq, LANES), fn)
    prev = lambda b, h, i: (b, h, jnp.maximum(i - 1, 0), 0)
    cur = lambda b, h, i: (b, h, i, 0)
    nxt = lambda b, h, i: (b, h, jnp.minimum(i + 1, nb - 1), 0)
    ctx_spec = pl.BlockSpec((None, None, c, LANES), lambda b, h, i: (b, h, 0, 0))
    return pl.pallas_call(
        functools.partial(_window_attn_kernel, tq=tq),
        out_shape=jax.ShapeDtypeStruct((bsz, s, B_HEADS * HEAD_DIM), bf16),
        grid=(bsz, B_KV, nb),
        in_specs=[pl.BlockSpec((None, tq, wq), lambda b, h, i: (b, i, h)),
                  ctx_spec, ctx_spec,
                  kv_spec(prev), kv_spec(cur), kv_spec(nxt),
                  kv_spec(prev), kv_spec(cur), kv_spec(nxt),
                  pl.BlockSpec((None, n_sub * tq, 1), lambda b, h, i: (h, 0, 0))],
        out_specs=pl.BlockSpec((None, tq, wq), lambda b, h, i: (b, i, h)),
        compiler_params=_cparams(("parallel", "parallel", "parallel")),
        name="window_attn",
    )(qn, kc2, vc2, k2, k2, k2, v2, v2, v2, sink_col)


def _seg_attn_kernel(*refs, n_seg, n_sub, tq, has_sink):
    q_ref = refs[0]
    kv = refs[1:1 + 2 * n_seg]
    sink = refs[1 + 2 * n_seg][...] if has_sink else None
    o_ref = refs[-1]
    q = _stack_heads(q_ref, n_sub)
    segs = [(kv[2 * j][...], kv[2 * j + 1][...], None, None) for j in range(n_seg)]
    o = _softmax_attend(q, segs, sink)
    _unstack_heads(o, o_ref, n_sub, tq)


def gqa_attention(qn, kv_segs, n_kv, tq, sink_col=None):
    bsz, s, width = qn.shape
    n_sub = width // HEAD_DIM // n_kv
    wq = n_sub * HEAD_DIM
    in_specs = [pl.BlockSpec((None, tq, wq), lambda b, h, i: (b, i, h))]
    args = [qn]
    for k2, v2 in kv_segs:
        n = k2.shape[2]
        spec = pl.BlockSpec((None, None, n, LANES), lambda b, h, i: (b, h, 0, 0))
        in_specs += [spec, spec]
        args += [k2, v2]
    if sink_col is not None:
        in_specs.append(pl.BlockSpec((None, n_sub * tq, 1), lambda b, h, i: (h, 0, 0)))
        args.append(sink_col)
    return pl.pallas_call(
        functools.partial(_seg_attn_kernel, n_seg=len(kv_segs), n_sub=n_sub, tq=tq, has_sink=sink_col is not None),
        out_shape=jax.ShapeDtypeStruct((bsz, s, width), bf16),
        grid=(bsz, n_kv, s // tq),
        in_specs=in_specs,
        out_specs=pl.BlockSpec((None, tq, wq), lambda b, h, i: (b, i, h)),
        compiler_params=_cparams(("parallel", "parallel", "parallel")),
        name="gqa_attn",
    )(*args)


def _na_attn_kernel(q_ref, kc_ref, vc_ref, k_ref, v_ref, bias_ref, o_ref, *, tq, n_grp):
    r = pl.program_id(1)
    rows_total = k_ref.shape[0] // GRID_W
    start = jnp.clip(r * NA_QROWS - NA_ROWS // 2, 0, rows_total - NA_KROWS)
    off = pl.multiple_of(start * GRID_W, GRID_W)
    nk = NA_KROWS * GRID_W
    q = _stack_heads(q_ref, 2)
    segs = [(kc_ref[...], vc_ref[...], None, None),
            (k_ref[pl.ds(off, nk), :], v_ref[pl.ds(off, nk), :], bias_ref[...], None)]
    o = _softmax_attend(q, segs, None)
    _unstack_heads(o, o_ref, 2, tq)


def neighbourhood_attention(qn, kn, vb, kcn, vcb, bias_tbl):
    bsz, s, width = qn.shape
    c = kcn.shape[1]
    tq = NA_QROWS * GRID_W
    n_grp = s // tq
    n_pair = width // LANES
    variant = lambda r: jnp.where(r == 0, 0, jnp.where(r == n_grp - 1, 2, 1))
    return pl.pallas_call(
        functools.partial(_na_attn_kernel, tq=tq, n_grp=n_grp),
        out_shape=jax.ShapeDtypeStruct((bsz, s, width), bf16),
        grid=(n_pair, n_grp, bsz),
        in_specs=[pl.BlockSpec((None, tq, LANES), lambda p, r, b: (b, r, p)),
                  pl.BlockSpec((None, c, LANES), lambda p, r, b: (b, 0, p)),
                  pl.BlockSpec((None, c, LANES), lambda p, r, b: (b, 0, p)),
                  pl.BlockSpec((None, s, LANES), lambda p, r, b: (b, 0, p)),
                  pl.BlockSpec((None, s, LANES), lambda p, r, b: (b, 0, p)),
                  pl.BlockSpec((None, None, 2 * tq, NA_KROWS * GRID_W), lambda p, r, b: (variant(r), p, 0, 0))],
        out_specs=pl.BlockSpec((None, tq, LANES), lambda p, r, b: (b, r, p)),
        compiler_params=_cparams(("parallel", "parallel", "parallel")),
        name="na_attn",
    )(qn, kcn, vcb, kn, vb, bias_tbl)


def _na_bias_kernel(rp_ref, o_ref, *, rows_total):
    n_grp = rows_total // NA_QROWS
    qc = lax.broadcasted_iota(jnp.int32, (GRID_W, LANES), 0)
    lane = lax.broadcasted_iota(jnp.int32, (GRID_W, LANES), 1)
    kc = lane % GRID_W
    cs = jnp.clip(qc - NA_COLS // 2, 0, GRID_W - NA_COLS)
    col_ok = (kc >= cs) & (kc < cs + NA_COLS)
    left = lane < GRID_W
    neg = jnp.full((GRID_W, LANES), NEG, f32)
    for v, r0 in enumerate((0, NA_QROWS, (n_grp - 1) * NA_QROWS)):
        start = min(max(r0 - NA_ROWS // 2, 0), rows_total - NA_KROWS)
        for hh in range(2):
            for i in range(NA_QROWS):
                qr = r0 + i
                ws = min(max(qr - NA_ROWS // 2, 0), rows_total - NA_ROWS)
                for jj in range(NA_KROWS // 2):
                    kra, krb = start + 2 * jj, start + 2 * jj + 1
                    ok_a, ok_b = ws <= kra < ws + NA_ROWS, ws <= krb < ws + NA_ROWS
                    r_lo = hh * NA_QROWS * GRID_W + i * GRID_W
                    dst = (v, slice(r_lo, r_lo + GRID_W), slice(jj * LANES, (jj + 1) * LANES))
                    if not (ok_a or ok_b):
                        o_ref[dst] = neg
                        continue
                    dra = min(max(kra - qr + NA_ROWS - 1, 0), 2 * NA_ROWS - 2)
                    drb = min(max(krb - qr + NA_ROWS - 1, 0), 2 * NA_ROWS - 2)
                    row = jnp.where(left[:1], rp_ref[hh, dra:dra + 1, :], rp_ref[hh, drb:drb + 1, :])
                    toe = pltpu.roll(jnp.broadcast_to(row, (GRID_W, LANES)), LANES - (NA_COLS - 1), axis=1,
                                     stride=1, stride_axis=0)
                    ok = col_ok if (ok_a and ok_b) else (col_ok & left if ok_a else col_ok & ~left)
                    o_ref[dst] = jnp.where(ok, toe, neg)


def na_bias_table(rpb, rows_total):
    h, ndr, ndc = rpb.shape
    rp = jnp.zeros((h, 16, LANES), f32).at[:, :ndr, :ndc].set(rpb).at[:, :ndr, GRID_W:GRID_W + ndc].set(rpb)
    return pl.pallas_call(
        functools.partial(_na_bias_kernel, rows_total=rows_total),
        out_shape=jax.ShapeDtypeStruct((3, h // 2, 2 * NA_QROWS * GRID_W, NA_KROWS * GRID_W), f32),
        grid=(h // 2,),
        in_specs=[pl.BlockSpec((2, 16, LANES), lambda p: (p, 0, 0))],
        out_specs=pl.BlockSpec((3, None, 2 * NA_QROWS * GRID_W, NA_KROWS * GRID_W), lambda p: (0, p, 0, 0)),
        compiler_params=_cparams(("parallel",)),
        name="na_bias",
    )(rp)


def _log_sigmoid(x):
    return jnp.minimum(x, 0.0) - jnp.log1p(jnp.exp(-jnp.abs(x)))


def _mlstm_chunk(c_ref, n_ref, m_ref, d, q, k, v, li, lf, backward):
    L = q.shape[0]
    t_idx = lax.broadcasted_iota(jnp.int32, (L, L), 0)
    s_idx = lax.broadcasted_iota(jnp.int32, (L, L), 1)
    tri = (s_idx >= t_idx) if backward else (s_idx <= t_idx)
    eye = s_idx == t_idx
    C, n, m = c_ref[d], n_ref[d], m_ref[d]
    b_col = jnp.sum(jnp.where(tri, lf, 0.0), axis=1, keepdims=True)
    b_row = jnp.sum(jnp.where(eye, b_col, 0.0), axis=0, keepdims=True)
    b_last = jnp.sum(lf, axis=1, keepdims=True)
    dmat = jnp.where(tri, b_col - b_row + li, -jnp.inf)
    inter = b_col + m
    m_t = jnp.maximum(inter, jnp.max(dmat, axis=1, keepdims=True))
    w = jnp.exp(dmat - m_t)
    a = jnp.exp(inter - m_t)
    qb, kb, vb = q.astype(bf16), k.astype(bf16), v.astype(bf16)
    s = _dot_nt(qb, kb) * w
    num = a * _dot(qb, C.astype(bf16)) + _dot(s.astype(bf16), vb)
    den = a * jnp.sum(q * n, axis=1, keepdims=True) + jnp.sum(s, axis=1, keepdims=True)
    h = num / jnp.maximum(jnp.abs(den), jnp.exp(-m_t))
    g_row = b_last - b_row + li
    m_new = jnp.maximum(b_last + m, jnp.max(g_row, axis=1, keepdims=True))
    decay = jnp.exp(b_last + m - m_new)
    wg_col = jnp.sum(jnp.where(eye, jnp.exp(g_row - m_new), 0.0), axis=1, keepdims=True)
    wk = wg_col * k
    c_ref[d] = decay * C + _dot_tn(wk.astype(bf16), vb)
    n_ref[d] = decay * n + jnp.sum(wk, axis=0, keepdims=True)
    m_ref[d] = m_new
    return h


def _mlstm_kernel(gb_ref, qc_ref, kc_ref, vc_ref, oc_ref, gc_ref, ql_ref, kl_ref, vl_ref, ol_ref, gl_ref, hg_ref,
                  outc_ref, outl_ref, c_scr, n_scr, m_scr, hfc, hbc, hfl, hbl):
    hd = pl.program_id(1)
    L = A_CHUNK
    c_scr[...] = jnp.zeros_like(c_scr)
    n_scr[...] = jnp.zeros_like(n_scr)
    m_scr[...] = jnp.zeros_like(m_scr)
    bias = [gb_ref[gi * A_HEADS + hd] for gi in range(4)]
    kscale = A_DIM ** -0.5

    def run(q_ref, k_ref, v_ref, g_ref, hf_ref, hb_ref):
        nc = q_ref.shape[0] // L

        def body(j, carry):
            for d, (cj, ig, fg, h_ref) in enumerate(((j, 0, 1, hf_ref), (nc - 1 - j, 2, 3, hb_ref))):
                rows = pl.ds(pl.multiple_of(cj * L, L), L)
                li = g_ref[ig, pl.ds(cj, 1), :] + bias[ig]
                lf = _log_sigmoid(g_ref[fg, pl.ds(cj, 1), :] + bias[fg])
                h_ref[rows, :] = _mlstm_chunk(c_scr, n_scr, m_scr, d, q_ref[rows, :], k_ref[rows, :] * kscale,
                                              v_ref[rows, :], li, lf, backward=(d == 1))
            return carry

        lax.fori_loop(0, nc, body, 0)

    run(qc_ref, kc_ref, vc_ref, gc_ref, hfc, hbc)
    run(ql_ref, kl_ref, vl_ref, gl_ref, hfl, hbl)
    for o_ref, hf_ref, hb_ref, out_ref in ((oc_ref, hfc, hbc, outc_ref), (ol_ref, hfl, hbl, outl_ref)):
        h = hf_ref[...] + hb_ref[...]
        hn = (h * lax.rsqrt(jnp.mean(h * h, axis=-1, keepdims=True) + EPS)) * hg_ref[...]
        o = o_ref[...]
        out_ref[...] = (hn / (1.0 + jnp.exp(-o))).astype(out_ref.dtype)


def mlstm_mixer(y_ctx, y_lat, gates_ctx, gates_lat, gate_b, head_g):
    bsz, tc, _ = y_ctx.shape
    tl = y_lat.shape[1]
    L = A_CHUNK

    def col(t, base):
        return pl.BlockSpec((None, t, A_DIM), functools.partial(lambda b, h, base: (b, 0, base + h), base=base))

    def gate_spec(t):
        return pl.BlockSpec((None, None, 4, t // L, L), lambda b, h: (b, h, 0, 0, 0))

    def out_spec(t):
        return pl.BlockSpec((None, t, A_DIM), lambda b, h: (b, 0, h))

    return pl.pallas_call(
        _mlstm_kernel,
        out_shape=[jax.ShapeDtypeStruct((bsz, tc, A_HEADS * A_DIM), bf16),
                   jax.ShapeDtypeStruct((bsz, tl, A_HEADS * A_DIM), bf16)],
        grid=(bsz, A_HEADS),
        in_specs=[pl.BlockSpec(memory_space=pltpu.SMEM)]
        + [col(tc, 4 * j) for j in range(4)] + [gate_spec(tc)]
        + [col(tl, 4 * j) for j in range(4)] + [gate_spec(tl)]
        + [pl.BlockSpec((None, 1, A_DIM), lambda b, h: (h, 0, 0))],
        out_specs=[out_spec(tc), out_spec(tl)],
        scratch_shapes=[pltpu.VMEM((2, A_DIM, A_DIM), f32), pltpu.VMEM((2, 1, A_DIM), f32), pltpu.VMEM((2, 1, 1), f32),
                        pltpu.VMEM((tc, A_DIM), f32), pltpu.VMEM((tc, A_DIM), f32),
                        pltpu.VMEM((tl, A_DIM), f32), pltpu.VMEM((tl, A_DIM), f32)],
        compiler_params=_cparams(("parallel", "parallel")),
        name="mlstm",
    )(gate_b, y_ctx, y_ctx, y_ctx, y_ctx, gates_ctx, y_lat, y_lat, y_lat, y_lat, gates_lat,
      head_g.reshape(A_HEADS, 1, A_DIM))


def _outproj_kernel(x_ref, a1_ref, a2_ref, w1_ref, w2_ref, gate_ref, o_ref):
    y = _dot(a1_ref[...], w1_ref[...]) + _dot(a2_ref[...], w2_ref[...])
    o_ref[...] = x_ref[...] + gate_ref[...] * y


def out_proj(x, a1, a2, w, modt, l, row, tm):
    bsz, t, d = x.shape
    k1, k2 = a1.shape[-1], a2.shape[-1]
    return pl.pallas_call(
        _outproj_kernel,
        out_shape=jax.ShapeDtypeStruct((bsz, t, d), f32),
        grid=(bsz, t // tm),
        in_specs=[pl.BlockSpec((None, tm, d), lambda b, i: (b, i, 0)),
                  pl.BlockSpec((None, tm, k1), lambda b, i: (b, i, 0)),
                  pl.BlockSpec((None, tm, k2), lambda b, i: (b, i, 0)),
                  pl.BlockSpec((k1, d), lambda b, i: (0, 0)),
                  pl.BlockSpec((k2, d), lambda b, i: (0, 0)),
                  _mod_spec(l, 2, row, d)],
        out_specs=pl.BlockSpec((None, tm, d), lambda b, i: (b, i, 0)),
        compiler_params=_cparams(("parallel", "parallel")),
        name="out_proj",
    )(x, a1, a2, w[:k1], w[k1:], modt)


FFN_HALO = 8


def _ffn_kernel(x_ref, xp_ref, xn_ref, sh_ref, sc_ref, gate_ref, g_ref, wg_ref, wv_ref, cwg_ref, cwv_ref,
                cbg_ref, cbv_ref, wd_ref, o_ref, h_scr, acc_scr, *, tm):
    i, j = pl.program_id(1), pl.program_id(2)
    rows = tm + 2 * FFN_HALO

    @pl.when(j == 0)
    def _():
        g, sh, sc = g_ref[...], sh_ref[...], sc_ref[...]
        hp = _modulated_norm(xp_ref[...], g, sh, sc)
        hn = _modulated_norm(xn_ref[...], g, sh, sc)
        h_scr[:FFN_HALO] = jnp.where(i > 0, hp, 0.0).astype(bf16)
        h_scr[FFN_HALO:FFN_HALO + tm] = _modulated_norm(x_ref[...], g, sh, sc).astype(bf16)
        h_scr[FFN_HALO + tm:] = jnp.where(i < pl.num_programs(1) - 1, hn, 0.0).astype(bf16)
        acc_scr[...] = jnp.zeros_like(acc_scr)

    h = h_scr[...]

    def conv(u, cw_ref, cb_ref):
        cw = cw_ref[...]
        prev = pltpu.roll(u, 1, axis=0)[FFN_HALO:FFN_HALO + tm]
        nxt = pltpu.roll(u, rows - 1, axis=0)[FFN_HALO:FFN_HALO + tm]
        return cw[0:1] * prev + cw[1:2] * u[FFN_HALO:FFN_HALO + tm] + cw[2:3] * nxt + cb_ref[...]

    cg = conv(_dot(h, wg_ref[...]), cwg_ref, cbg_ref)
    cv = conv(_dot(h, wv_ref[...]), cwv_ref, cbv_ref)
    act = (cg / (1.0 + jnp.exp(-cg))) * cv
    acc_scr[...] += _dot(act.astype(bf16), wd_ref[...])

    @pl.when(j == pl.num_programs(2) - 1)
    def _():
        o_ref[...] = x_ref[...] + gate_ref[...] * acc_scr[...]


def conv_ffn(x, modt, l, row, g, w_up, conv_w, conv_b, w_down, tm, nt):
    bsz, t, d = x.shape
    dff = w_down.shape[0]
    nj = dff // nt
    hb = tm // FFN_HALO
    last = t // FFN_HALO - 1
    cb = conv_b.reshape(1, 2 * dff)
    return pl.pallas_call(
        functools.partial(_ffn_kernel, tm=tm),
        out_shape=jax.ShapeDtypeStruct((bsz, t, d), f32),
        grid=(bsz, t // tm, nj),
        in_specs=[pl.BlockSpec((None, tm, d), lambda b, i, j: (b, i, 0)),
                  pl.BlockSpec((None, FFN_HALO, d), lambda b, i, j: (b, jnp.maximum(i * hb - 1, 0), 0)),
                  pl.BlockSpec((None, FFN_HALO, d), lambda b, i, j: (b, jnp.minimum((i + 1) * hb, last), 0)),
                  _mod_spec(l, 3, row, d), _mod_spec(l, 4, row, d), _mod_spec(l, 5, row, d),
                  pl.BlockSpec((1, d), lambda b, i, j: (0, 0)),
                  pl.BlockSpec((d, nt), lambda b, i, j: (0, j)),
                  pl.BlockSpec((d, nt), lambda b, i, j: (0, nj + j)),
                  pl.BlockSpec((3, nt), lambda b, i, j: (0, j)),
                  pl.BlockSpec((3, nt), lambda b, i, j: (0, nj + j)),
                  pl.BlockSpec((1, nt), lambda b, i, j: (0, j)),
                  pl.BlockSpec((1, nt), lambda b, i, j: (0, nj + j)),
                  pl.BlockSpec((nt, d), lambda b, i, j: (j, 0))],
        out_specs=pl.BlockSpec((None, tm, d), lambda b, i, j: (b, i, 0)),
        scratch_shapes=[pltpu.VMEM((tm + 2 * FFN_HALO, d), bf16), pltpu.VMEM((tm, d), f32)],
        compiler_params=_cparams(("parallel", "parallel", "arbitrary")),
        name="conv_ffn",
    )(x, x, x, modt, modt, modt, g.reshape(1, d), w_up, w_up, conv_w, conv_w, cb, cb, w_down)


def _rope_tables(n):
    t = jnp.arange(n)
    row = (t // GRID_W).astype(f32)
    colp = (t % GRID_W).astype(f32)
    half = HEAD_DIM // 2
    freq = ROPE_THETA ** (-jnp.arange(0, half, 2, dtype=f32) / half)
    ang_r = row[:, None] * freq[None, :]
    ang_c = colp[:, None] * freq[None, :]
    ang = jnp.concatenate([ang_r, ang_r, ang_c, ang_c] * 2, axis=-1)
    return jnp.cos(ang), jnp.sin(ang)


def _gate_layout(y, col0):
    bsz, t, _ = y.shape
    g = y[:, :, col0:col0 + 4 * A_HEADS].reshape(bsz, t // A_CHUNK, A_CHUNK, 4, A_HEADS)
    return jnp.transpose(g, (0, 4, 3, 1, 2))


def _sink_column(sink, n_kv, tq):
    n_sub = sink.shape[0] // n_kv
    return jnp.repeat(sink.reshape(n_kv, n_sub), tq, axis=1)[..., None].astype(f32)


def kernel(x, c, ctx, c_ctx, ada_w, ada_b, norm_g, w_out, ffn_up, ffn_conv_w, ffn_conv_b, ffn_down, even_w_in,
           mlstm_gate_b, mlstm_head_g, swa_qk_g, swa_sink, odd_w_in, gqa_qk_g, na_qk_g, na_rpb):
    bsz, seq, d = x.shape
    n_ctx = ctx.shape[1]
    depth = ada_w.shape[0]
    a_w = A_HEADS * A_DIM
    cos, sin = _rope_tables(seq)
    ones_c, zeros_c = jnp.ones((n_ctx, LANES), f32), jnp.zeros((n_ctx, LANES), f32)

    cc = jnp.zeros((16, d), f32).at[:bsz].set(c).at[bsz].set(c_ctx)
    modt = ada_modulation(cc, ada_w, ada_b).reshape(depth, 16, 6, 1, d)

    x_lat, x_ctx = x, ctx
    tm_l, tm_c = 256, 256
    for l in range(depth):
        need_ctx = l < depth - 1
        w_o = w_out[l].astype(bf16)
        if l % 2 == 0:
            e = l // 2
            wi = even_w_in[e]
            sp = np.cumsum((0, a_w, a_w, a_w, a_w, 4 * A_HEADS, B_HEADS * HEAD_DIM, B_KV * HEAD_DIM, B_KV * HEAD_DIM))
            aq, ak, av, ao, ag, bq, bk, bv = (wi[:, sp[k]:sp[k + 1]] for k in range(8))
            w = jnp.concatenate([aq, ak, av, ao, bq, bk, bv, ag, jnp.zeros((d, LANES - 4 * A_HEADS), f32)],
                                axis=1).astype(bf16)
            gate_col = 4 * a_w + (B_HEADS + 2 * B_KV) * HEAD_DIM
            y_lat = modulated_proj(x_lat, modt, l, None, norm_g[l, 0], w, tm_l)
            y_ctx = modulated_proj(x_ctx, modt, l, bsz, norm_g[l, 0], w, tm_c)
            a_ctx, a_lat = mlstm_mixer(y_ctx, y_lat, _gate_layout(y_ctx, gate_col), _gate_layout(y_lat, gate_col),
                                       mlstm_gate_b[e], mlstm_head_g[e])
            gq, gk = swa_qk_g[e, 0], swa_qk_g[e, 1]
            qn_l, k2_l, v2_l = qk_prep(y_lat, q_blocks=(4,), kpair_blocks=(10,), kfull_blocks=(), vfull_blocks=(),
                                       gains=(gq, gk), rope=(True, True), cos=cos, sin=sin, tm=tm_l)
            qn_c, k2_c, v2_c = qk_prep(y_ctx, q_blocks=(4,), kpair_blocks=(10,), kfull_blocks=(), vfull_blocks=(),
                                       gains=(gq, gk), rope=(False, False), cos=ones_c, sin=zeros_c, tm=tm_c)
            b_lat = window_attention(qn_l, k2_l, v2_l, k2_c, v2_c, _sink_column(swa_sink[e], B_KV, B_WIN))
            mix_l = (a_lat, b_lat)
            if need_ctx:
                b_ctx = gqa_attention(qn_c, [(k2_c, v2_c)], B_KV, 128, _sink_column(swa_sink[e], B_KV, 128))
                mix_c = (a_ctx, b_ctx)
        else:
            o = l // 2
            wi = odd_w_in[o]
            sp = np.cumsum((0, C_HEADS * HEAD_DIM, C_KV * HEAD_DIM, C_KV * HEAD_DIM) + (D_HEADS * HEAD_DIM,) * 3)
            cq, ck, cv, nq, nk, nv = (wi[:, sp[k]:sp[k + 1]] for k in range(6))
            w_ctx = jnp.concatenate([nk, nv, ck, cv], axis=1).astype(bf16)
            w = jnp.concatenate([cq.astype(bf16), nq.astype(bf16), w_ctx], axis=1)
            y_lat = modulated_proj(x_lat, modt, l, None, norm_g[l, 0], w, tm_l)
            gq, gk = gqa_qk_g[o, 0], gqa_qk_g[o, 1]
            nq_g, nk_g = na_qk_g[o, 0], na_qk_g[o, 1]
            cqn, nqn, ck2, cv2, nkn, nvb = qk_prep(
                y_lat, q_blocks=(0, 1), kpair_blocks=(8,), kfull_blocks=(2,), vfull_blocks=(3,),
                gains=(gq, nq_g, gk, nk_g), rope=(True, False, True), cos=cos, sin=sin, tm=tm_l)
            if need_ctx:
                raise NotImplementedError("context outputs of an odd layer")
            y_ctx = modulated_proj(x_ctx, modt, l, bsz, norm_g[l, 0], w_ctx, tm_c)
            ck2_c, cv2_c, nkn_c, nvb_c = qk_prep(
                y_ctx, q_blocks=(), kpair_blocks=(4,), kfull_blocks=(0,), vfull_blocks=(1,),
                gains=(gk, nk_g), rope=(False,), cos=ones_c, sin=zeros_c, tm=tm_c)
            c_lat = gqa_attention(cqn, [(ck2_c, cv2_c), (ck2, cv2)], C_KV, 128)
            d_lat = neighbourhood_attention(nqn, nkn, nvb, nkn_c, nvb_c, na_bias_table(na_rpb[o], seq // GRID_W))
            mix_l = (c_lat, d_lat)

        ffn = (norm_g[l, 1], ffn_up[l].astype(bf16), ffn_conv_w[l], ffn_conv_b[l], ffn_down[l].astype(bf16))
        x_lat = out_proj(x_lat, mix_l[0], mix_l[1], w_o, modt, l, None, tm_l)
        x_lat = conv_ffn(x_lat, modt, l, None, *ffn, tm=512, nt=256)
        if need_ctx:
            x_ctx = out_proj(x_ctx, mix_c[0], mix_c[1], w_o, modt, l, bsz, tm_c)
            x_ctx = conv_ffn(x_ctx, modt, l, bsz, *ffn, tm=256, nt=256)
    return x_lat
```

```python
import functools

import jax
import jax.numpy as jnp
import numpy as np
from jax import lax
from jax.experimental import pallas as pl
from jax.experimental.pallas import tpu as pltpu

f32 = jnp.float32
bf16 = jnp.bfloat16

GRID_W = 64
HEAD_DIM = 64
LANES = 128
A_HEADS = 4
A_DIM = 128
B_HEADS = 8
B_KV = 2
B_WIN = 128
C_HEADS = 8
C_KV = 2
D_HEADS = 8
NA_ROWS = 8
NA_COLS = 16
NA_QROWS = 4
NA_KROWS = 12
ROPE_THETA = 10000.0
EPS = 1e-6
NEG = -1e30
VMEM_LIMIT = 56 * 1024 * 1024


def _cparams(sem):
    return pltpu.CompilerParams(dimension_semantics=sem, vmem_limit_bytes=VMEM_LIMIT)


def _dot(a, b):
    return jnp.dot(a, b, preferred_element_type=f32)


def _dot_nt(a, b):
    return lax.dot_general(a, b, (((1,), (1,)), ((), ())), preferred_element_type=f32)


def _modulated_norm(x, g, shift, scale):
    y = x * lax.rsqrt(jnp.mean(x * x, axis=-1, keepdims=True) + EPS)
    return (y * g) * (1.0 + scale) + shift


def _mod_spec(l, k, row, d):
    if row is None:
        return pl.BlockSpec((None, None, None, 1, d), lambda b, *_: (l, b, k, 0, 0))
    return pl.BlockSpec((None, None, None, 1, d), lambda b, *_: (l, row, k, 0, 0))


def _ada_kernel(c_ref, w_ref, b_ref, o_ref):
    c = c_ref[...]
    s = c / (1.0 + jnp.exp(-c))
    o_ref[...] = _dot(s.astype(bf16), w_ref[...].astype(bf16)) + b_ref[...]


def ada_modulation(cc, ada_w, ada_b):
    depth, d, n = ada_w.shape
    tn = 1536
    return pl.pallas_call(
        _ada_kernel,
        out_shape=jax.ShapeDtypeStruct((depth, cc.shape[0], n), f32),
        grid=(depth, n // tn),
        in_specs=[pl.BlockSpec(cc.shape, lambda l, j: (0, 0)),
                  pl.BlockSpec((None, d, tn), lambda l, j: (l, 0, j)),
                  pl.BlockSpec((None, 1, tn), lambda l, j: (l, 0, j))],
        out_specs=pl.BlockSpec((None, cc.shape[0], tn), lambda l, j: (l, 0, j)),
        compiler_params=_cparams(("parallel", "parallel")),
        name="ada",
    )(cc, ada_w, ada_b.reshape(depth, 1, n))


def _proj_kernel(x_ref, sh_ref, sc_ref, g_ref, w_ref, o_ref):
    h = _modulated_norm(x_ref[...], g_ref[...], sh_ref[...], sc_ref[...])
    o_ref[...] = _dot(h.astype(bf16), w_ref[...])


def modulated_proj(x, modt, l, row, g, w, tm):
    bsz, t, d = x.shape
    n = w.shape[1]
    return pl.pallas_call(
        _proj_kernel,
        out_shape=jax.ShapeDtypeStruct((bsz, t, n), f32),
        grid=(bsz, t // tm),
        in_specs=[pl.BlockSpec((None, tm, d), lambda b, i: (b, i, 0)),
                  _mod_spec(l, 0, row, d), _mod_spec(l, 1, row, d),
                  pl.BlockSpec((1, d), lambda b, i: (0, 0)),
                  pl.BlockSpec((d, n), lambda b, i: (0, 0))],
        out_specs=pl.BlockSpec((None, tm, n), lambda b, i: (b, i, 0)),
        compiler_params=_cparams(("parallel", "parallel")),
        name="proj",
    )(x, modt, modt, g.reshape(1, d), w)


def _head_ms(a):
    ri = lax.broadcasted_iota(jnp.int32, (LANES, LANES), 0) // HEAD_DIM
    ci = lax.broadcasted_iota(jnp.int32, (LANES, LANES), 1) // HEAD_DIM
    bd = jnp.where(ri == ci, 1.0, 0.0).astype(bf16)
    ss = a * a
    hi = ss.astype(bf16)
    lo = (ss - hi.astype(f32)).astype(bf16)
    return (_dot(hi, bd) + _dot(lo, bd)) * (1.0 / HEAD_DIM)


def _head_norm(a, g):
    return (a * lax.rsqrt(_head_ms(a) + EPS)) * g


def _rope(a, cos, sin):
    lane = lax.broadcasted_iota(jnp.int32, a.shape, 1)
    quarter = HEAD_DIM // 4
    rot = jnp.where(lane % (2 * quarter) < quarter,
                    -pltpu.roll(a, LANES - quarter, axis=1), pltpu.roll(a, quarter, axis=1))
    return a * cos + rot * sin


def _dup_halves(a):
    lane = lax.broadcasted_iota(jnp.int32, a.shape, 1)
    sw = pltpu.roll(a, HEAD_DIM, axis=1)
    lo = lane < HEAD_DIM
    return jnp.where(lo, a, sw), jnp.where(lo, sw, a)


def _prep_kernel(*refs, n_q, n_kpair, n_kfull, n_vfull, rope, scale):
    it = iter(refs)
    q_refs = [next(it) for _ in range(n_q)]
    kp_refs = [next(it) for _ in range(n_kpair)]
    kf_refs = [next(it) for _ in range(n_kfull)]
    vf_refs = [next(it) for _ in range(n_vfull)]
    g_refs = [next(it) for _ in range(n_q + n_kpair + n_kfull)]
    cos_ref, sin_ref = next(it), next(it)
    qo_refs = [next(it) for _ in range(n_q)]
    kpo_refs = [(next(it), next(it)) for _ in range(n_kpair)]
    kfo_refs = [next(it) for _ in range(n_kfull)]
    vfo_refs = [next(it) for _ in range(n_vfull)]
    gi = iter(g_refs)
    for qi, (q_ref, qo_ref) in enumerate(zip(q_refs, qo_refs)):
        g = next(gi)[...]
        for p in range(q_ref.shape[-1] // LANES):
            a = _head_norm(q_ref[:, p * LANES:(p + 1) * LANES], g)
            if rope[qi]:
                a = _rope(a, cos_ref[...], sin_ref[...])
            qo_ref[:, p * LANES:(p + 1) * LANES] = (a * scale).astype(bf16)
    for ki, (kp_ref, (ko_ref, vo_ref)) in enumerate(zip(kp_refs, kpo_refs)):
        g = next(gi)[...]
        k = _head_norm(kp_ref[:, :LANES], g)
        if rope[n_q + ki]:
            k = _rope(k, cos_ref[...], sin_ref[...])
        k0, k1 = _dup_halves(k)
        ko_ref[0] = k0.astype(bf16)
        ko_ref[1] = k1.astype(bf16)
        v0, v1 = _dup_halves(kp_ref[:, LANES:])
        vo_ref[0] = v0.astype(bf16)
        vo_ref[1] = v1.astype(bf16)
    for kf_ref, kfo_ref in zip(kf_refs, kfo_refs):
        g = next(gi)[...]
        for p in range(kf_ref.shape[-1] // LANES):
            kfo_ref[:, p * LANES:(p + 1) * LANES] = _head_norm(kf_ref[:, p * LANES:(p + 1) * LANES], g).astype(bf16)
    for vf_ref, vfo_ref in zip(vf_refs, vfo_refs):
        vfo_ref[...] = vf_ref[...].astype(bf16)


def qk_prep(y, *, q_blocks, kpair_blocks, kfull_blocks, vfull_blocks, gains, rope, cos, sin, tm):
    bsz, t, _ = y.shape
    wide, pair = 4 * LANES, 2 * LANES
    in_specs, args = [], []
    for cb in q_blocks + ():
        in_specs.append(pl.BlockSpec((None, tm, wide), functools.partial(lambda b, i, cb: (b, i, cb), cb=cb)))
        args.append(y)
    for cb in kpair_blocks:
        in_specs.append(pl.BlockSpec((None, tm, pair), functools.partial(lambda b, i, cb: (b, i, cb), cb=cb)))
        args.append(y)
    for cb in kfull_blocks + vfull_blocks:
        in_specs.append(pl.BlockSpec((None, tm, wide), functools.partial(lambda b, i, cb: (b, i, cb), cb=cb)))
        args.append(y)
    for g in gains:
        in_specs.append(pl.BlockSpec((1, LANES), lambda b, i: (0, 0)))
        args.append(jnp.tile(g, 2).reshape(1, LANES))
    for tbl in (cos, sin):
        in_specs.append(pl.BlockSpec((tm, LANES), lambda b, i: (i, 0)))
        args.append(tbl)
    out_shape, out_specs = [], []
    for _ in q_blocks:
        out_shape.append(jax.ShapeDtypeStruct((bsz, t, wide), bf16))
        out_specs.append(pl.BlockSpec((None, tm, wide), lambda b, i: (b, i, 0)))
    for _ in kpair_blocks:
        for _ in range(2):
            out_shape.append(jax.ShapeDtypeStruct((bsz, 2, t, LANES), bf16))
            out_specs.append(pl.BlockSpec((None, 2, tm, LANES), lambda b, i: (b, 0, i, 0)))
    for _ in kfull_blocks + vfull_blocks:
        out_shape.append(jax.ShapeDtypeStruct((bsz, t, wide), bf16))
        out_specs.append(pl.BlockSpec((None, tm, wide), lambda b, i: (b, i, 0)))
    kern = functools.partial(_prep_kernel, n_q=len(q_blocks), n_kpair=len(kpair_blocks), n_kfull=len(kfull_blocks),
                             n_vfull=len(vfull_blocks), rope=rope, scale=HEAD_DIM ** -0.5)
    return pl.pallas_call(
        kern, out_shape=out_shape, grid=(bsz, t // tm), in_specs=in_specs, out_specs=out_specs,
        compiler_params=_cparams(("parallel", "parallel")), name="qk_prep",
    )(*args)


def _stack_heads(q_ref, n_sub):
    parts = []
    for g in range(n_sub):
        blk = q_ref[:, (g // 2) * LANES:(g // 2 + 1) * LANES]
        lane = lax.broadcasted_iota(jnp.int32, blk.shape, 1)
        keep = (lane < HEAD_DIM) if g % 2 == 0 else (lane >= HEAD_DIM)
        parts.append(jnp.where(keep, blk, jnp.zeros_like(blk)))
    return jnp.concatenate(parts, axis=0)


def _softmax_attend(q, segs, sink):
    scores = []
    for k, _, bias, mask in segs:
        s = _dot_nt(q, k)
        if bias is not None:
            s = s + bias
        if mask is not None:
            s = jnp.where(mask, s, NEG)
        scores.append(s)
    m = scores[0].max(axis=-1, keepdims=True)
    for s in scores[1:]:
        m = jnp.maximum(m, s.max(axis=-1, keepdims=True))
    if sink is not None:
        m = jnp.maximum(m, sink)
    den = jnp.exp(sink - m) if sink is not None else None
    acc = None
    for s, (_, v, _, _) in zip(scores, segs):
        p = jnp.exp(s - m)
        ps = p.sum(axis=-1, keepdims=True)
        den = ps if den is None else den + ps
        o = _dot(p.astype(bf16), v)
        acc = o if acc is None else acc + o
    return acc / den


def _unstack_heads(o, o_ref, n_sub, tq):
    lane = lax.broadcasted_iota(jnp.int32, (tq, LANES), 1)
    for p in range(n_sub // 2):
        even = o[(2 * p) * tq:(2 * p + 1) * tq]
        odd = o[(2 * p + 1) * tq:(2 * p + 2) * tq]
        o_ref[:, p * LANES:(p + 1) * LANES] = jnp.where(lane < HEAD_DIM, even, odd).astype(o_ref.dtype)


def _window_attn_kernel(q_ref, kc_ref, vc_ref, kp_ref, k0_ref, kn_ref, vp_ref, v0_ref, vn_ref, sink_ref, o_ref, *, tq):
    i = pl.program_id(2)
    nb = pl.num_programs(2)
    n_sub = B_HEADS // B_KV
    rows = 2 * tq
    qpos = lax.broadcasted_iota(jnp.int32, (rows, tq), 0) % tq
    kpos = lax.broadcasted_iota(jnp.int32, (rows, tq), 1)
    m_prev = (kpos >= qpos) & (i > 0)
    m_next = (kpos <= qpos) & (i < nb - 1)
    segs = [(kc_ref[...], vc_ref[...], None, None),
            (kp_ref[...], vp_ref[...], None, m_prev),
            (k0_ref[...], v0_ref[...], None, None),
            (kn_ref[...], vn_ref[...], None, m_next)]
    for p in range(n_sub // 2):
        lanes = slice(p * LANES, (p + 1) * LANES)
        o = _softmax_attend(_stack_heads(q_ref.at[:, lanes], 2), segs, sink_ref[2 * p * tq:(2 * p + 2) * tq])
        _unstack_heads(o, o_ref.at[:, lanes], 2, tq)


def window_attention(qn, k2, v2, kc2, vc2, sink_col):
    bsz, s, _ = qn.shape
    c = kc2.shape[2]
    tq = B_WIN
    nb = s // tq
    n_sub = B_HEADS // B_KV
    wq = n_sub * HEAD_DIM
    kv_spec = lambda fn: pl.BlockSpec((None, None, tq, LANES), fn)
    prev = lambda b, h, i: (b, h, jnp.maximum(i - 1, 0), 0)
    cur = lambda b, h, i: (b, h, i, 0)
    nxt = lambda b, h, i: (b, h, jnp.minimum(i + 1, nb - 1), 0)
    ctx_spec = pl.BlockSpec((None, None, c, LANES), lambda b, h, i: (b, h, 0, 0))
    return pl.pallas_call(
        functools.partial(_window_attn_kernel, tq=tq),
        out_shape=jax.ShapeDtypeStruct((bsz, s, B_HEADS * HEAD_DIM), bf16),
        grid=(bsz, B_KV, nb),
        in_specs=[pl.BlockSpec((None, tq, wq), lambda b, h, i: (b, i, h)),
                  ctx_spec, ctx_spec,
                  kv_spec(prev), kv_spec(cur), kv_spec(nxt),
                  kv_spec(prev), kv_spec(cur), kv_spec(nxt),
                  pl.BlockSpec((None, n_sub * tq, 1), lambda b, h, i: (h, 0, 0))],
        out_specs=pl.BlockSpec((None, tq, wq), lambda b, h, i: (b, i, h)),
        compiler_params=_cparams(("parallel", "parallel", "parallel")),
        name="window_attn",
    )(qn, kc2, vc2, k2, k2, k2, v2, v2, v2, sink_col)


def _seg_attn_kernel(*refs, n_seg, n_sub, tq, has_sink):
    q_ref = refs[0]
    kv = refs[1:1 + 2 * n_seg]
    sink = refs[1 + 2 * n_seg][...] if has_sink else None
    o_ref = refs[-1]
    segs = [(kv[2 * j][...], kv[2 * j + 1][...], None, None) for j in range(n_seg)]
    for p in range(n_sub // 2):
        lanes = slice(p * LANES, (p + 1) * LANES)
        sink_p = None if sink is None else sink[2 * p * tq:(2 * p + 2) * tq]
        o = _softmax_attend(_stack_heads(q_ref.at[:, lanes], 2), segs, sink_p)
        _unstack_heads(o, o_ref.at[:, lanes], 2, tq)


def gqa_attention(qn, kv_segs, n_kv, tq, sink_col=None):
    bsz, s, width = qn.shape
    n_sub = width // HEAD_DIM // n_kv
    wq = n_sub * HEAD_DIM
    in_specs = [pl.BlockSpec((None, tq, wq), lambda b, h, i: (b, i, h))]
    args = [qn]
    for k2, v2 in kv_segs:
        n = k2.shape[2]
        spec = pl.BlockSpec((None, None, n, LANES), lambda b, h, i: (b, h, 0, 0))
        in_specs += [spec, spec]
        args += [k2, v2]
    if sink_col is not None:
        in_specs.append(pl.BlockSpec((None, n_sub * tq, 1), lambda b, h, i: (h, 0, 0)))
        args.append(sink_col)
    return pl.pallas_call(
        functools.partial(_seg_attn_kernel, n_seg=len(kv_segs), n_sub=n_sub, tq=tq, has_sink=sink_col is not None),
        out_shape=jax.ShapeDtypeStruct((bsz, s, width), bf16),
        grid=(bsz, n_kv, s // tq),
        in_specs=in_specs,
        out_specs=pl.BlockSpec((None, tq, wq), lambda b, h, i: (b, i, h)),
        compiler_params=_cparams(("parallel", "parallel", "parallel")),
        name="gqa_attn",
    )(*args)


def _na_attn_kernel(q_ref, kc_ref, vc_ref, k_ref, v_ref, bias_ref, o_ref, *, tq, n_grp):
    r = pl.program_id(1)
    rows_total = k_ref.shape[0] // GRID_W
    start = jnp.clip(r * NA_QROWS - NA_ROWS // 2, 0, rows_total - NA_KROWS)
    off = pl.multiple_of(start * GRID_W, GRID_W)
    nk = NA_KROWS * GRID_W
    blk = q_ref[...]
    lane = lax.broadcasted_iota(jnp.int32, blk.shape, 1)
    k_nb, v_nb = k_ref[pl.ds(off, nk), :], v_ref[pl.ds(off, nk), :]
    outs = []
    for half in range(2):
        keep = (lane < HEAD_DIM) if half == 0 else (lane >= HEAD_DIM)
        segs = [(kc_ref[...], vc_ref[...], None, None),
                (k_nb, v_nb, bias_ref[half * tq:(half + 1) * tq, :], None)]
        outs.append(_softmax_attend(jnp.where(keep, blk, jnp.zeros_like(blk)), segs, None))
    o_ref[...] = jnp.where(lane < HEAD_DIM, outs[0], outs[1]).astype(o_ref.dtype)


def neighbourhood_attention(qn, kn, vb, kcn, vcb, bias_tbl):
    bsz, s, width = qn.shape
    c = kcn.shape[1]
    tq = NA_QROWS * GRID_W
    n_grp = s // tq
    n_pair = width // LANES
    variant = lambda r: jnp.where(r == 0, 0, jnp.where(r == n_grp - 1, 2, 1))
    return pl.pallas_call(
        functools.partial(_na_attn_kernel, tq=tq, n_grp=n_grp),
        out_shape=jax.ShapeDtypeStruct((bsz, s, width), bf16),
        grid=(n_pair, n_grp, bsz),
        in_specs=[pl.BlockSpec((None, tq, LANES), lambda p, r, b: (b, r, p)),
                  pl.BlockSpec((None, c, LANES), lambda p, r, b: (b, 0, p)),
                  pl.BlockSpec((None, c, LANES), lambda p, r, b: (b, 0, p)),
                  pl.BlockSpec((None, s, LANES), lambda p, r, b: (b, 0, p)),
                  pl.BlockSpec((None, s, LANES), lambda p, r, b: (b, 0, p)),
                  pl.BlockSpec((None, None, 2 * tq, NA_KROWS * GRID_W), lambda p, r, b: (variant(r), p, 0, 0))],
        out_specs=pl.BlockSpec((None, tq, LANES), lambda p, r, b: (b, r, p)),
        compiler_params=_cparams(("parallel", "parallel", "parallel")),
        name="na_attn",
    )(qn, kcn, vcb, kn, vb, bias_tbl)


def _na_bias_kernel(rp_ref, o_ref, *, rows_total):
    n_grp = rows_total // NA_QROWS
    qc = lax.broadcasted_iota(jnp.int32, (GRID_W, LANES), 0)
    lane = lax.broadcasted_iota(jnp.int32, (GRID_W, LANES), 1)
    kc = lane % GRID_W
    cs = jnp.clip(qc - NA_COLS // 2, 0, GRID_W - NA_COLS)
    col_ok = (kc >= cs) & (kc < cs + NA_COLS)
    left = lane < GRID_W
    neg = jnp.full((GRID_W, LANES), NEG, f32)
    for v, r0 in enumerate((0, NA_QROWS, (n_grp - 1) * NA_QROWS)):
        start = min(max(r0 - NA_ROWS // 2, 0), rows_total - NA_KROWS)
        for hh in range(2):
            for i in range(NA_QROWS):
                qr = r0 + i
                ws = min(max(qr - NA_ROWS // 2, 0), rows_total - NA_ROWS)
                for jj in range(NA_KROWS // 2):
                    kra, krb = start + 2 * jj, start + 2 * jj + 1
                    ok_a, ok_b = ws <= kra < ws + NA_ROWS, ws <= krb < ws + NA_ROWS
                    r_lo = hh * NA_QROWS * GRID_W + i * GRID_W
                    dst = (v, slice(r_lo, r_lo + GRID_W), slice(jj * LANES, (jj + 1) * LANES))
                    if not (ok_a or ok_b):
                        o_ref[dst] = neg
                        continue
                    dra = min(max(kra - qr + NA_ROWS - 1, 0), 2 * NA_ROWS - 2)
                    drb = min(max(krb - qr + NA_ROWS - 1, 0), 2 * NA_ROWS - 2)
                    row = jnp.where(left[:1], rp_ref[hh, dra:dra + 1, :], rp_ref[hh, drb:drb + 1, :])
                    toe = pltpu.roll(jnp.broadcast_to(row, (GRID_W, LANES)), LANES - (NA_COLS - 1), axis=1,
                                     stride=1, stride_axis=0)
                    ok = col_ok if (ok_a and ok_b) else (col_ok & left if ok_a else col_ok & ~left)
                    o_ref[dst] = jnp.where(ok, toe, neg)


def na_bias_table(rpb, rows_total):
    h, ndr, ndc = rpb.shape
    rp = jnp.zeros((h, 16, LANES), f32).at[:, :ndr, :ndc].set(rpb).at[:, :ndr, GRID_W:GRID_W + ndc].set(rpb)
    return pl.pallas_call(
        functools.partial(_na_bias_kernel, rows_total=rows_total),
        out_shape=jax.ShapeDtypeStruct((3, h // 2, 2 * NA_QROWS * GRID_W, NA_KROWS * GRID_W), f32),
        grid=(h // 2,),
        in_specs=[pl.BlockSpec((2, 16, LANES), lambda p: (p, 0, 0))],
        out_specs=pl.BlockSpec((3, None, 2 * NA_QROWS * GRID_W, NA_KROWS * GRID_W), lambda p: (0, p, 0, 0)),
        compiler_params=_cparams(("parallel",)),
        name="na_bias",
    )(rp)


def _log_sigmoid(x):
    return jnp.minimum(x, 0.0) - jnp.log1p(jnp.exp(-jnp.abs(x)))


MLSTM_L = 128
MLSTM_HP = 2


def _mlstm_chunk(cx_ref, m_ref, d, masks, q, k_t, v, li, lf):
    tri, eye = masks
    cx, m = cx_ref[d], m_ref[d]
    b_col = jnp.sum(jnp.where(tri, lf, 0.0), axis=1, keepdims=True)
    b_row = jnp.sum(jnp.where(eye, b_col, 0.0), axis=0, keepdims=True)
    b_last = jnp.sum(lf, axis=1, keepdims=True)
    dmat = jnp.where(tri, b_col - b_row + li, -jnp.inf)
    inter = b_col + m
    m_t = jnp.maximum(inter, jnp.max(dmat, axis=1, keepdims=True))
    w = jnp.exp(dmat - m_t)
    a = jnp.exp(inter - m_t)
    vx = jnp.concatenate([v, jnp.ones_like(v)], axis=1)
    s = _dot(q, k_t.astype(bf16)) * w
    nd = a * _dot(q, cx.astype(bf16)) + _dot(s.astype(bf16), vx)
    h = nd[:, :A_DIM] / jnp.maximum(jnp.abs(nd[:, A_DIM:]), jnp.exp(-m_t))
    g_row = b_last - b_row + li
    m_new = jnp.maximum(b_last + m, jnp.max(g_row, axis=1, keepdims=True))
    decay = jnp.exp(b_last + m - m_new)
    wk_t = k_t * jnp.exp(g_row - m_new)
    cx_ref[d] = decay * cx + _dot(wk_t.astype(bf16), vx)
    m_ref[d] = m_new
    return h


def _mlstm_kernel(gb_ref, qc_ref, kc_ref, vc_ref, oc_ref, gc_ref, ql_ref, kl_ref, vl_ref, ol_ref, gl_ref, hg_ref,
                  outc_ref, outl_ref, cx_scr, m_scr, g_scr, hc_scr, hl_scr):
    hg = pl.program_id(1)
    L = MLSTM_L
    cx_scr[...] = jnp.zeros_like(cx_scr)
    m_scr[...] = jnp.zeros_like(m_scr)
    kscale = A_DIM ** -0.5
    t_idx = lax.broadcasted_iota(jnp.int32, (L, L), 0)
    s_idx = lax.broadcasted_iota(jnp.int32, (L, L), 1)
    eye = s_idx == t_idx
    masks = ((s_idx <= t_idx, eye), (s_idx >= t_idx, eye))
    n_gates = 4 * A_HEADS

    def run(q_ref, k_ref, v_ref, g_ref, h_ref):
        nc = q_ref.shape[0] // L

        def body(j, accumulate):
            for d in range(2):
                cj = j if d == 0 else nc - 1 - j
                rows = pl.ds(pl.multiple_of(cj * L, L), L)
                g_scr[d] = g_ref[rows, :].T[:n_gates]
                for hh in range(MLSTM_HP):
                    hd = hg * MLSTM_HP + hh
                    ig, fg = (2 * d) * A_HEADS + hd, (2 * d + 1) * A_HEADS + hd
                    li = g_scr[d, pl.ds(ig, 1), :] + gb_ref[ig]
                    lf = _log_sigmoid(g_scr[d, pl.ds(fg, 1), :] + gb_ref[fg])
                    cols = slice(hh * A_DIM, (hh + 1) * A_DIM)
                    k_t = (k_ref[rows, cols] * kscale).T
                    h = _mlstm_chunk(cx_scr, m_scr, 2 * hh + d, masks[d], q_ref[rows, cols].astype(bf16), k_t,
                                     v_ref[rows, cols].astype(bf16), li, lf)
                    if accumulate:
                        h_ref[rows, cols] += h
                    else:
                        h_ref[rows, cols] = h

        lax.fori_loop(0, nc // 2, lambda j, c: (body(j, False), c)[1], 0)
        lax.fori_loop(nc // 2, nc, lambda j, c: (body(j, True), c)[1], 0)

    run(qc_ref, kc_ref, vc_ref, gc_ref, hc_scr)
    run(ql_ref, kl_ref, vl_ref, gl_ref, hl_scr)
    for o_ref, h_ref, out_ref in ((oc_ref, hc_scr, outc_ref), (ol_ref, hl_scr, outl_ref)):
        for hh in range(MLSTM_HP):
            cols = slice(hh * A_DIM, (hh + 1) * A_DIM)
            h = h_ref[:, cols]
            hn = (h * lax.rsqrt(jnp.mean(h * h, axis=-1, keepdims=True) + EPS)) * hg_ref[:, cols]
            o = o_ref[:, cols]
            out_ref[:, cols] = (hn / (1.0 + jnp.exp(-o))).astype(out_ref.dtype)


def mlstm_mixer(y_ctx, y_lat, gate_block, gate_b, head_g):
    bsz, tc, _ = y_ctx.shape
    tl = y_lat.shape[1]
    wide = MLSTM_HP * A_DIM
    n_grp = A_HEADS // MLSTM_HP

    def col(t, base):
        return pl.BlockSpec((None, t, wide), functools.partial(lambda b, h, base: (b, 0, base + h), base=base))

    def gate_spec(t):
        return pl.BlockSpec((None, t, LANES), lambda b, h: (b, 0, gate_block))

    def out_spec(t):
        return pl.BlockSpec((None, t, wide), lambda b, h: (b, 0, h))

    return pl.pallas_call(
        _mlstm_kernel,
        out_shape=[jax.ShapeDtypeStruct((bsz, tc, A_HEADS * A_DIM), bf16),
                   jax.ShapeDtypeStruct((bsz, tl, A_HEADS * A_DIM), bf16)],
        grid=(bsz, n_grp),
        in_specs=[pl.BlockSpec(memory_space=pltpu.SMEM)]
        + [col(tc, n_grp * j) for j in range(4)] + [gate_spec(tc)]
        + [col(tl, n_grp * j) for j in range(4)] + [gate_spec(tl)]
        + [pl.BlockSpec((1, wide), lambda b, h: (0, h))],
        out_specs=[out_spec(tc), out_spec(tl)],
        scratch_shapes=[pltpu.VMEM((2 * MLSTM_HP, A_DIM, 2 * A_DIM), f32), pltpu.VMEM((2 * MLSTM_HP, 1, 1), f32),
                        pltpu.VMEM((2, 4 * A_HEADS, LANES), f32),
                        pltpu.VMEM((tc, wide), f32), pltpu.VMEM((tl, wide), f32)],
        compiler_params=_cparams(("parallel", "parallel")),
        name="mlstm",
    )(gate_b, y_ctx, y_ctx, y_ctx, y_ctx, y_ctx, y_lat, y_lat, y_lat, y_lat, y_lat,
      head_g.reshape(1, A_HEADS * A_DIM))


def _outproj_kernel(x_ref, a1_ref, a2_ref, w1_ref, w2_ref, gate_ref, o_ref):
    y = _dot(a1_ref[...], w1_ref[...]) + _dot(a2_ref[...], w2_ref[...])
    o_ref[...] = x_ref[...] + gate_ref[...] * y


def out_proj(x, a1, a2, w, modt, l, row, tm):
    bsz, t, d = x.shape
    k1, k2 = a1.shape[-1], a2.shape[-1]
    return pl.pallas_call(
        _outproj_kernel,
        out_shape=jax.ShapeDtypeStruct((bsz, t, d), f32),
        grid=(bsz, t // tm),
        in_specs=[pl.BlockSpec((None, tm, d), lambda b, i: (b, i, 0)),
                  pl.BlockSpec((None, tm, k1), lambda b, i: (b, i, 0)),
                  pl.BlockSpec((None, tm, k2), lambda b, i: (b, i, 0)),
                  pl.BlockSpec((k1, d), lambda b, i: (0, 0)),
                  pl.BlockSpec((k2, d), lambda b, i: (0, 0)),
                  _mod_spec(l, 2, row, d)],
        out_specs=pl.BlockSpec((None, tm, d), lambda b, i: (b, i, 0)),
        compiler_params=_cparams(("parallel", "parallel")),
        name="out_proj",
    )(x, a1, a2, w[:k1], w[k1:], modt)


FFN_HALO = 8


FFN_CHUNK = 256


def _ffn_kernel(x_ref, xp_ref, xn_ref, sh_ref, sc_ref, gate_ref, g_ref, wup_ref, cw_ref, cb_ref, wd_ref,
                o_ref, h_scr, u_scr, acc_scr, *, tm):
    i = pl.program_id(1)
    g, sh, sc = g_ref[...], sh_ref[...], sc_ref[...]
    hp = _modulated_norm(xp_ref[...], g, sh, sc)
    hn = _modulated_norm(xn_ref[...], g, sh, sc)
    h_scr[:FFN_HALO] = jnp.where(i > 0, hp, 0.0).astype(bf16)
    h_scr[FFN_HALO:FFN_HALO + tm] = _modulated_norm(x_ref[...], g, sh, sc).astype(bf16)
    h_scr[FFN_HALO + tm:] = jnp.where(i < pl.num_programs(1) - 1, hn, 0.0).astype(bf16)
    n_chunks = wup_ref.shape[0]
    u_scr[0] = _dot(h_scr[...], wup_ref[0])
    for c in range(n_chunks):
        u = u_scr.at[c % 2]
        acc = acc_scr.at[c % 2]
        if c + 1 < n_chunks:
            u_scr[(c + 1) % 2] = _dot(h_scr[...], wup_ref[c + 1])
        cw = cw_ref[c]
        conv = (cw[0:1] * u[FFN_HALO - 1:FFN_HALO - 1 + tm] + cw[1:2] * u[FFN_HALO:FFN_HALO + tm]
                + cw[2:3] * u[FFN_HALO + 1:FFN_HALO + 1 + tm] + cb_ref[c])
        cg, cv = conv[:, :FFN_CHUNK], conv[:, FFN_CHUNK:]
        act = (cg / (1.0 + jnp.exp(-cg))) * cv
        part = _dot(act.astype(bf16), wd_ref[c])
        if c < 2:
            acc[...] = part
        else:
            acc[...] += part
    o_ref[...] = x_ref[...] + gate_ref[...] * (acc_scr[0] + acc_scr[1])


def ffn_weights(w_up, conv_w, conv_b, w_down):
    d, dff = w_up.shape[0], w_down.shape[0]
    nc = dff // FFN_CHUNK

    def pair(a):
        a = jnp.moveaxis(a.reshape(a.shape[0], 2, nc, FFN_CHUNK), 2, 0)
        return a.reshape(nc, a.shape[1], 2 * FFN_CHUNK)

    return (pair(w_up.astype(bf16)), pair(conv_w), pair(conv_b.reshape(1, 2 * dff)),
            w_down.astype(bf16).reshape(nc, FFN_CHUNK, d))


def conv_ffn(x, modt, l, row, g, weights, tm):
    bsz, t, d = x.shape
    wup, cw, cb, wd = weights
    hb = tm // FFN_HALO
    last = t // FFN_HALO - 1

    def resident(a):
        return pl.BlockSpec(a.shape, lambda b, i: (0,) * a.ndim, pipeline_mode=pl.Buffered(1))

    return pl.pallas_call(
        functools.partial(_ffn_kernel, tm=tm),
        out_shape=jax.ShapeDtypeStruct((bsz, t, d), f32),
        grid=(bsz, t // tm),
        in_specs=[pl.BlockSpec((None, tm, d), lambda b, i: (b, i, 0)),
                  pl.BlockSpec((None, FFN_HALO, d), lambda b, i: (b, jnp.maximum(i * hb - 1, 0), 0)),
                  pl.BlockSpec((None, FFN_HALO, d), lambda b, i: (b, jnp.minimum((i + 1) * hb, last), 0)),
                  _mod_spec(l, 3, row, d), _mod_spec(l, 4, row, d), _mod_spec(l, 5, row, d),
                  pl.BlockSpec((1, d), lambda b, i: (0, 0)),
                  resident(wup), resident(cw), resident(cb), resident(wd)],
        out_specs=pl.BlockSpec((None, tm, d), lambda b, i: (b, i, 0)),
        scratch_shapes=[pltpu.VMEM((tm + 2 * FFN_HALO, d), bf16),
                        pltpu.VMEM((2, tm + 2 * FFN_HALO, 2 * FFN_CHUNK), f32),
                        pltpu.VMEM((2, tm, d), f32)],
        compiler_params=_cparams(("parallel", "parallel")),
        name="conv_ffn",
    )(x, x, x, modt, modt, modt, g.reshape(1, d), wup, cw, cb, wd)


def _rope_tables(n):
    t = jnp.arange(n)
    row = (t // GRID_W).astype(f32)
    colp = (t % GRID_W).astype(f32)
    half = HEAD_DIM // 2
    freq = ROPE_THETA ** (-jnp.arange(0, half, 2, dtype=f32) / half)
    ang_r = row[:, None] * freq[None, :]
    ang_c = colp[:, None] * freq[None, :]
    ang = jnp.concatenate([ang_r, ang_r, ang_c, ang_c] * 2, axis=-1)
    return jnp.cos(ang), jnp.sin(ang)


def _sink_column(sink, n_kv, tq):
    n_sub = sink.shape[0] // n_kv
    return jnp.repeat(sink.reshape(n_kv, n_sub), tq, axis=1)[..., None].astype(f32)


def kernel(x, c, ctx, c_ctx, ada_w, ada_b, norm_g, w_out, ffn_up, ffn_conv_w, ffn_conv_b, ffn_down, even_w_in,
           mlstm_gate_b, mlstm_head_g, swa_qk_g, swa_sink, odd_w_in, gqa_qk_g, na_qk_g, na_rpb):
    bsz, seq, d = x.shape
    n_ctx = ctx.shape[1]
    depth = ada_w.shape[0]
    a_w = A_HEADS * A_DIM
    cos, sin = _rope_tables(seq)
    ones_c, zeros_c = jnp.ones((n_ctx, LANES), f32), jnp.zeros((n_ctx, LANES), f32)

    cc = jnp.zeros((16, d), f32).at[:bsz].set(c).at[bsz].set(c_ctx)
    modt = ada_modulation(cc, ada_w, ada_b).reshape(depth, 16, 6, 1, d)

    x_lat, x_ctx = x, ctx
    tm_l, tm_c = 256, 256
    for l in range(depth):
        need_ctx = l < depth - 1
        w_o = w_out[l].astype(bf16)
        if l % 2 == 0:
            e = l // 2
            wi = even_w_in[e]
            sp = np.cumsum((0, a_w, a_w, a_w, a_w, 4 * A_HEADS, B_HEADS * HEAD_DIM, B_KV * HEAD_DIM, B_KV * HEAD_DIM))
            aq, ak, av, ao, ag, bq, bk, bv = (wi[:, sp[k]:sp[k + 1]] for k in range(8))
            w = jnp.concatenate([aq, ak, av, ao, bq, bk, bv, ag, jnp.zeros((d, LANES - 4 * A_HEADS), f32)],
                                axis=1).astype(bf16)
            gate_col = 4 * a_w + (B_HEADS + 2 * B_KV) * HEAD_DIM
            y_lat = modulated_proj(x_lat, modt, l, None, norm_g[l, 0], w, tm_l)
            y_ctx = modulated_proj(x_ctx, modt, l, bsz, norm_g[l, 0], w, tm_c)
            a_ctx, a_lat = mlstm_mixer(y_ctx, y_lat, gate_col // LANES, mlstm_gate_b[e], mlstm_head_g[e])
            gq, gk = swa_qk_g[e, 0], swa_qk_g[e, 1]
            qn_l, k2_l, v2_l = qk_prep(y_lat, q_blocks=(4,), kpair_blocks=(10,), kfull_blocks=(), vfull_blocks=(),
                                       gains=(gq, gk), rope=(True, True), cos=cos, sin=sin, tm=tm_l)
            qn_c, k2_c, v2_c = qk_prep(y_ctx, q_blocks=(4,), kpair_blocks=(10,), kfull_blocks=(), vfull_blocks=(),
                                       gains=(gq, gk), rope=(False, False), cos=ones_c, sin=zeros_c, tm=tm_c)
            b_lat = window_attention(qn_l, k2_l, v2_l, k2_c, v2_c, _sink_column(swa_sink[e], B_KV, B_WIN))
            mix_l = (a_lat, b_lat)
            if need_ctx:
                b_ctx = gqa_attention(qn_c, [(k2_c, v2_c)], B_KV, 128, _sink_column(swa_sink[e], B_KV, 128))
                mix_c = (a_ctx, b_ctx)
        else:
            o = l // 2
            wi = odd_w_in[o]
            sp = np.cumsum((0, C_HEADS * HEAD_DIM, C_KV * HEAD_DIM, C_KV * HEAD_DIM) + (D_HEADS * HEAD_DIM,) * 3)
            cq, ck, cv, nq, nk, nv = (wi[:, sp[k]:sp[k + 1]] for k in range(6))
            w_ctx = jnp.concatenate([nk, nv, ck, cv], axis=1).astype(bf16)
            w = jnp.concatenate([cq.astype(bf16), nq.astype(bf16), w_ctx], axis=1)
            y_lat = modulated_proj(x_lat, modt, l, None, norm_g[l, 0], w, tm_l)
            gq, gk = gqa_qk_g[o, 0], gqa_qk_g[o, 1]
            nq_g, nk_g = na_qk_g[o, 0], na_qk_g[o, 1]
            cqn, nqn, ck2, cv2, nkn, nvb = qk_prep(
                y_lat, q_blocks=(0, 1), kpair_blocks=(8,), kfull_blocks=(2,), vfull_blocks=(3,),
                gains=(gq, nq_g, gk, nk_g), rope=(True, False, True), cos=cos, sin=sin, tm=tm_l)
            if need_ctx:
                raise NotImplementedError("context outputs of an odd layer")
            y_ctx = modulated_proj(x_ctx, modt, l, bsz, norm_g[l, 0], w_ctx, tm_c)
            ck2_c, cv2_c, nkn_c, nvb_c = qk_prep(
                y_ctx, q_blocks=(), kpair_blocks=(4,), kfull_blocks=(0,), vfull_blocks=(1,),
                gains=(gk, nk_g), rope=(False,), cos=ones_c, sin=zeros_c, tm=tm_c)
            c_lat = gqa_attention(cqn, [(ck2_c, cv2_c), (ck2, cv2)], C_KV, 128)
            d_lat = neighbourhood_attention(nqn, nkn, nvb, nkn_c, nvb_c, na_bias_table(na_rpb[o], seq // GRID_W))
            mix_l = (c_lat, d_lat)

        ffn_w = ffn_weights(ffn_up[l], ffn_conv_w[l], ffn_conv_b[l], ffn_down[l])
        x_lat = out_proj(x_lat, mix_l[0], mix_l[1], w_o, modt, l, None, tm_l)
        x_lat = conv_ffn(x_lat, modt, l, None, norm_g[l, 1], ffn_w, tm=512)
        if need_ctx:
            x_ctx = out_proj(x_ctx, mix_c[0], mix_c[1], w_o, modt, l, bsz, tm_c)
            x_ctx = conv_ffn(x_ctx, modt, l, bsz, norm_g[l, 1], ffn_w, tm=256)
    return x_lat
```

```python
import functools

import jax
import jax.numpy as jnp
import numpy as np
from jax import lax
from jax.experimental import pallas as pl
from jax.experimental.pallas import tpu as pltpu

f32 = jnp.float32
bf16 = jnp.bfloat16

GRID_W = 64
HEAD_DIM = 64
LANES = 128
A_HEADS = 4
A_DIM = 128
B_HEADS = 8
B_KV = 2
B_WIN = 128
C_HEADS = 8
C_KV = 2
D_HEADS = 8
NA_ROWS = 8
NA_COLS = 16
NA_QROWS = 4
NA_KROWS = 12
ROPE_THETA = 10000.0
EPS = 1e-6
NEG = -1e30
VMEM_LIMIT = 56 * 1024 * 1024


def _cparams(sem):
    return pltpu.CompilerParams(dimension_semantics=sem, vmem_limit_bytes=VMEM_LIMIT)


def _dot(a, b):
    return jnp.dot(a, b, preferred_element_type=f32)


def _dot_nt(a, b):
    return lax.dot_general(a, b, (((1,), (1,)), ((), ())), preferred_element_type=f32)


def _modulated_norm(x, g, shift, scale):
    y = x * lax.rsqrt(jnp.mean(x * x, axis=-1, keepdims=True) + EPS)
    return (y * g) * (1.0 + scale) + shift


def _mod_spec(l, k, row, d):
    if row is None:
        return pl.BlockSpec((None, None, None, 1, d), lambda b, *_: (l, b, k, 0, 0))
    return pl.BlockSpec((None, None, None, 1, d), lambda b, *_: (l, row, k, 0, 0))


def _ada_kernel(c_ref, w_ref, b_ref, o_ref):
    c = c_ref[...]
    s = c / (1.0 + jnp.exp(-c))
    o_ref[...] = _dot(s.astype(bf16), w_ref[...].astype(bf16)) + b_ref[...]


def ada_modulation(cc, ada_w, ada_b):
    depth, d, n = ada_w.shape
    tn = 1536
    return pl.pallas_call(
        _ada_kernel,
        out_shape=jax.ShapeDtypeStruct((depth, cc.shape[0], n), f32),
        grid=(depth, n // tn),
        in_specs=[pl.BlockSpec(cc.shape, lambda l, j: (0, 0)),
                  pl.BlockSpec((None, d, tn), lambda l, j: (l, 0, j)),
                  pl.BlockSpec((None, 1, tn), lambda l, j: (l, 0, j))],
        out_specs=pl.BlockSpec((None, cc.shape[0], tn), lambda l, j: (l, 0, j)),
        compiler_params=_cparams(("parallel", "parallel")),
        name="ada",
    )(cc, ada_w, ada_b.reshape(depth, 1, n))


def _head_ms(a):
    ri = lax.broadcasted_iota(jnp.int32, (LANES, LANES), 0) // HEAD_DIM
    ci = lax.broadcasted_iota(jnp.int32, (LANES, LANES), 1) // HEAD_DIM
    bd = jnp.where(ri == ci, 1.0, 0.0).astype(bf16)
    ss = a * a
    hi = ss.astype(bf16)
    lo = (ss - hi.astype(f32)).astype(bf16)
    return (_dot(hi, bd) + _dot(lo, bd)) * (1.0 / HEAD_DIM)


def _head_norm(a, g):
    return (a * lax.rsqrt(_head_ms(a) + EPS)) * g


def _rope(a, cos, sin):
    lane = lax.broadcasted_iota(jnp.int32, a.shape, 1)
    quarter = HEAD_DIM // 4
    rot = jnp.where(lane % (2 * quarter) < quarter,
                    -pltpu.roll(a, LANES - quarter, axis=1), pltpu.roll(a, quarter, axis=1))
    return a * cos + rot * sin


def _dup_halves(a):
    lane = lax.broadcasted_iota(jnp.int32, a.shape, 1)
    sw = pltpu.roll(a, HEAD_DIM, axis=1)
    lo = lane < HEAD_DIM
    return jnp.where(lo, a, sw), jnp.where(lo, sw, a)


def _prep_kernel(*refs, n_q, n_kpair, n_kfull, n_vfull, rope, scale):
    it = iter(refs)
    q_refs = [next(it) for _ in range(n_q)]
    kp_refs = [next(it) for _ in range(n_kpair)]
    kf_refs = [next(it) for _ in range(n_kfull)]
    vf_refs = [next(it) for _ in range(n_vfull)]
    g_refs = [next(it) for _ in range(n_q + n_kpair + n_kfull)]
    cos_ref, sin_ref = next(it), next(it)
    qo_refs = [next(it) for _ in range(n_q)]
    kpo_refs = [(next(it), next(it)) for _ in range(n_kpair)]
    kfo_refs = [next(it) for _ in range(n_kfull)]
    vfo_refs = [next(it) for _ in range(n_vfull)]
    gi = iter(g_refs)
    for qi, (q_ref, qo_ref) in enumerate(zip(q_refs, qo_refs)):
        g = next(gi)[...]
        for p in range(q_ref.shape[-1] // LANES):
            a = _head_norm(q_ref[:, p * LANES:(p + 1) * LANES], g)
            if rope[qi]:
                a = _rope(a, cos_ref[...], sin_ref[...])
            qo_ref[:, p * LANES:(p + 1) * LANES] = (a * scale).astype(bf16)
    for ki, (kp_ref, (ko_ref, vo_ref)) in enumerate(zip(kp_refs, kpo_refs)):
        g = next(gi)[...]
        k = _head_norm(kp_ref[:, :LANES], g)
        if rope[n_q + ki]:
            k = _rope(k, cos_ref[...], sin_ref[...])
        k0, k1 = _dup_halves(k)
        ko_ref[0] = k0.astype(bf16)
        ko_ref[1] = k1.astype(bf16)
        v0, v1 = _dup_halves(kp_ref[:, LANES:])
        vo_ref[0] = v0.astype(bf16)
        vo_ref[1] = v1.astype(bf16)
    for kf_ref, kfo_ref in zip(kf_refs, kfo_refs):
        g = next(gi)[...]
        for p in range(kf_ref.shape[-1] // LANES):
            kfo_ref[:, p * LANES:(p + 1) * LANES] = _head_norm(kf_ref[:, p * LANES:(p + 1) * LANES], g).astype(bf16)
    for vf_ref, vfo_ref in zip(vf_refs, vfo_refs):
        vfo_ref[...] = vf_ref[...].astype(bf16)


def _proj_prep_kernel(x_ref, sh_ref, sc_ref, g_ref, w_ref, *rest, n_raw, q_cols, kpair_cols, kfull_cols, vfull_cols,
                      rope):
    n_gain = len(q_cols) + len(kpair_cols) + len(kfull_cols)
    side = rest[:n_gain + 2]
    outs = rest[n_gain + 2:-1]
    y_scr = rest[-1]
    h = _modulated_norm(x_ref[...], g_ref[...], sh_ref[...], sc_ref[...])
    y_scr[...] = _dot(h.astype(bf16), w_ref[...])
    if n_raw:
        outs[0][...] = y_scr[:, :n_raw]
        outs = outs[1:]
    wide, pair = 4 * LANES, 2 * LANES
    views = ([y_scr.at[:, c:c + wide] for c in q_cols] + [y_scr.at[:, c:c + pair] for c in kpair_cols]
             + [y_scr.at[:, c:c + wide] for c in kfull_cols + vfull_cols])
    _prep_kernel(*views, *side, *outs, n_q=len(q_cols), n_kpair=len(kpair_cols), n_kfull=len(kfull_cols),
                 n_vfull=len(vfull_cols), rope=rope, scale=HEAD_DIM ** -0.5)


def proj_prep(x, modt, l, row, norm_g, w, *, n_raw, q_cols, kpair_cols, kfull_cols, vfull_cols, gains, rope, cos, sin,
              tm):
    bsz, t, d = x.shape
    n = w.shape[1]
    wide = 4 * LANES
    in_specs = [pl.BlockSpec((None, tm, d), lambda b, i: (b, i, 0)),
                _mod_spec(l, 0, row, d), _mod_spec(l, 1, row, d),
                pl.BlockSpec((1, d), lambda b, i: (0, 0)),
                pl.BlockSpec((d, n), lambda b, i: (0, 0))]
    args = [x, modt, modt, norm_g.reshape(1, d), w]
    for g in gains:
        in_specs.append(pl.BlockSpec((1, LANES), lambda b, i: (0, 0)))
        args.append(jnp.tile(g, 2).reshape(1, LANES))
    for tbl in (cos, sin):
        in_specs.append(pl.BlockSpec((tm, LANES), lambda b, i: (i, 0)))
        args.append(tbl)
    out_shape, out_specs = [], []
    if n_raw:
        out_shape.append(jax.ShapeDtypeStruct((bsz, t, n_raw), f32))
        out_specs.append(pl.BlockSpec((None, tm, n_raw), lambda b, i: (b, i, 0)))
    for _ in q_cols:
        out_shape.append(jax.ShapeDtypeStruct((bsz, t, wide), bf16))
        out_specs.append(pl.BlockSpec((None, tm, wide), lambda b, i: (b, i, 0)))
    for _ in kpair_cols:
        for _ in range(2):
            out_shape.append(jax.ShapeDtypeStruct((bsz, 2, t, LANES), bf16))
            out_specs.append(pl.BlockSpec((None, 2, tm, LANES), lambda b, i: (b, 0, i, 0)))
    for _ in kfull_cols + vfull_cols:
        out_shape.append(jax.ShapeDtypeStruct((bsz, t, wide), bf16))
        out_specs.append(pl.BlockSpec((None, tm, wide), lambda b, i: (b, i, 0)))
    kern = functools.partial(_proj_prep_kernel, n_raw=n_raw, q_cols=q_cols, kpair_cols=kpair_cols,
                             kfull_cols=kfull_cols, vfull_cols=vfull_cols, rope=rope)
    return pl.pallas_call(
        kern, out_shape=out_shape, grid=(bsz, t // tm), in_specs=in_specs, out_specs=out_specs,
        scratch_shapes=[pltpu.VMEM((tm, n), f32)],
        compiler_params=_cparams(("parallel", "parallel")), name="proj_prep",
    )(*args)


def _stack_heads(q_ref, n_sub):
    parts = []
    for g in range(n_sub):
        blk = q_ref[:, (g // 2) * LANES:(g // 2 + 1) * LANES]
        lane = lax.broadcasted_iota(jnp.int32, blk.shape, 1)
        keep = (lane < HEAD_DIM) if g % 2 == 0 else (lane >= HEAD_DIM)
        parts.append(jnp.where(keep, blk, jnp.zeros_like(blk)))
    return jnp.concatenate(parts, axis=0)


def _scores(q, segs):
    scores = []
    for k, _, bias, mask in segs:
        s = _dot_nt(q, k)
        if bias is not None:
            s = s + bias
        if mask is not None:
            s = jnp.where(mask, s, NEG)
        scores.append(s)
    return scores


def _softmax_attend(q, segs, sink, scores=None):
    if scores is None:
        scores = _scores(q, segs)
    m = scores[0].max(axis=-1, keepdims=True)
    for s in scores[1:]:
        m = jnp.maximum(m, s.max(axis=-1, keepdims=True))
    if sink is not None:
        m = jnp.maximum(m, sink)
    den = jnp.exp(sink - m) if sink is not None else None
    acc = None
    for s, (_, v, _, _) in zip(scores, segs):
        p = jnp.exp(s - m)
        ps = p.sum(axis=-1, keepdims=True)
        den = ps if den is None else den + ps
        o = _dot(p.astype(bf16), v)
        acc = o if acc is None else acc + o
    return acc / den


def _unstack_heads(o, o_ref, n_sub, tq):
    lane = lax.broadcasted_iota(jnp.int32, (tq, LANES), 1)
    for p in range(n_sub // 2):
        even = o[(2 * p) * tq:(2 * p + 1) * tq]
        odd = o[(2 * p + 1) * tq:(2 * p + 2) * tq]
        o_ref[:, p * LANES:(p + 1) * LANES] = jnp.where(lane < HEAD_DIM, even, odd).astype(o_ref.dtype)


def _window_attn_kernel(q_ref, kc_ref, vc_ref, kp_ref, k0_ref, kn_ref, vp_ref, v0_ref, vn_ref, sink_ref, o_ref, *, tq):
    i = pl.program_id(2)
    nb = pl.num_programs(2)
    n_sub = B_HEADS // B_KV
    n_ctx = kc_ref.shape[0]
    rows, n_keys = n_sub * tq, n_ctx + 3 * tq
    qpos = lax.broadcasted_iota(jnp.int32, (rows, n_keys), 0) % tq
    kpos = lax.broadcasted_iota(jnp.int32, (rows, n_keys), 1) - n_ctx
    bad_prev = (kpos >= 0) & (kpos < tq) & ((kpos < qpos) | (i == 0))
    bad_next = (kpos >= 2 * tq) & ((kpos - 2 * tq > qpos) | (i == nb - 1))
    k_all = jnp.concatenate([kc_ref[...], kp_ref[...], k0_ref[...], kn_ref[...]], axis=0)
    v_all = jnp.concatenate([vc_ref[...], vp_ref[...], v0_ref[...], vn_ref[...]], axis=0)
    o = _softmax_attend(_stack_heads(q_ref, n_sub), [(k_all, v_all, None, ~(bad_prev | bad_next))], sink_ref[...])
    _unstack_heads(o, o_ref, n_sub, tq)


def window_attention(qn, k2, v2, kc2, vc2, sink_col):
    bsz, s, _ = qn.shape
    c = kc2.shape[2]
    tq = B_WIN
    nb = s // tq
    n_sub = B_HEADS // B_KV
    wq = n_sub * HEAD_DIM
    half_spec = lambda fn: pl.BlockSpec((None, None, tq, LANES), fn)
    prev = lambda b, h, i: (b, h, jnp.maximum(i - 1, 0), 0)
    nxt = lambda b, h, i: (b, h, jnp.minimum(i + 1, nb - 1), 0)
    cur_spec = pl.BlockSpec((None, None, tq, LANES), lambda b, h, i: (b, h, i, 0))
    ctx_spec = pl.BlockSpec((None, None, c, LANES), lambda b, h, i: (b, h, 0, 0))
    return pl.pallas_call(
        functools.partial(_window_attn_kernel, tq=tq),
        out_shape=jax.ShapeDtypeStruct((bsz, s, B_HEADS * HEAD_DIM), bf16),
        grid=(bsz, B_KV, nb),
        in_specs=[pl.BlockSpec((None, tq, wq), lambda b, h, i: (b, i, h)),
                  ctx_spec, ctx_spec,
                  half_spec(prev), cur_spec, half_spec(nxt),
                  half_spec(prev), cur_spec, half_spec(nxt),
                  pl.BlockSpec((None, n_sub * tq, 1), lambda b, h, i: (h, 0, 0))],
        out_specs=pl.BlockSpec((None, tq, wq), lambda b, h, i: (b, i, h)),
        compiler_params=_cparams(("parallel", "parallel", "parallel")),
        name="window_attn",
    )(qn, kc2, vc2, k2, k2, k2, v2, v2, v2, sink_col)


def _seg_attn_kernel(*refs, n_seg, n_sub, tq, has_sink):
    q_ref = refs[0]
    kv = refs[1:1 + 2 * n_seg]
    sink = refs[1 + 2 * n_seg][...] if has_sink else None
    o_ref = refs[-1]
    segs = [(kv[2 * j][...], kv[2 * j + 1][...], None, None) for j in range(n_seg)]
    n_pair = n_sub // 2
    lanes = [slice(p * LANES, (p + 1) * LANES) for p in range(n_pair)]
    scores = [_scores(_stack_heads(q_ref.at[:, lanes[p]], 2), segs) for p in range(n_pair)]
    for p in range(n_pair):
        sink_p = None if sink is None else sink[2 * p * tq:(2 * p + 2) * tq]
        o = _softmax_attend(None, segs, sink_p, scores[p])
        _unstack_heads(o, o_ref.at[:, lanes[p]], 2, tq)


def gqa_attention(qn, kv_segs, n_kv, tq, sink_col=None):
    bsz, s, width = qn.shape
    n_sub = width // HEAD_DIM // n_kv
    wq = n_sub * HEAD_DIM
    in_specs = [pl.BlockSpec((None, tq, wq), lambda b, h, i: (b, i, h))]
    args = [qn]
    for k2, v2 in kv_segs:
        n = k2.shape[2]
        spec = pl.BlockSpec((None, None, n, LANES), lambda b, h, i: (b, h, 0, 0))
        in_specs += [spec, spec]
        args += [k2, v2]
    if sink_col is not None:
        in_specs.append(pl.BlockSpec((None, n_sub * tq, 1), lambda b, h, i: (h, 0, 0)))
        args.append(sink_col)
    return pl.pallas_call(
        functools.partial(_seg_attn_kernel, n_seg=len(kv_segs), n_sub=n_sub, tq=tq, has_sink=sink_col is not None),
        out_shape=jax.ShapeDtypeStruct((bsz, s, width), bf16),
        grid=(bsz, n_kv, s // tq),
        in_specs=in_specs,
        out_specs=pl.BlockSpec((None, tq, wq), lambda b, h, i: (b, i, h)),
        compiler_params=_cparams(("parallel", "parallel", "parallel")),
        name="gqa_attn",
    )(*args)


def _na_attn_kernel(q_ref, kc_ref, vc_ref, k_ref, v_ref, bias_ref, o_ref, *, tq, n_grp):
    r = pl.program_id(1)
    rows_total = k_ref.shape[0] // GRID_W
    start = jnp.clip(r * NA_QROWS - NA_ROWS // 2, 0, rows_total - NA_KROWS)
    off = pl.multiple_of(start * GRID_W, GRID_W)
    nk = NA_KROWS * GRID_W
    blk = q_ref[...]
    lane = lax.broadcasted_iota(jnp.int32, blk.shape, 1)
    k_nb, v_nb = k_ref[pl.ds(off, nk), :], v_ref[pl.ds(off, nk), :]
    outs = []
    for half in range(2):
        keep = (lane < HEAD_DIM) if half == 0 else (lane >= HEAD_DIM)
        segs = [(kc_ref[...], vc_ref[...], None, None),
                (k_nb, v_nb, bias_ref[half * tq:(half + 1) * tq, :], None)]
        outs.append(_softmax_attend(jnp.where(keep, blk, jnp.zeros_like(blk)), segs, None))
    o_ref[...] = jnp.where(lane < HEAD_DIM, outs[0], outs[1]).astype(o_ref.dtype)


def neighbourhood_attention(qn, kn, vb, kcn, vcb, bias_tbl):
    bsz, s, width = qn.shape
    c = kcn.shape[1]
    tq = NA_QROWS * GRID_W
    n_grp = s // tq
    n_pair = width // LANES
    variant = lambda r: jnp.where(r == 0, 0, jnp.where(r == n_grp - 1, 2, 1))
    return pl.pallas_call(
        functools.partial(_na_attn_kernel, tq=tq, n_grp=n_grp),
        out_shape=jax.ShapeDtypeStruct((bsz, s, width), bf16),
        grid=(n_pair, n_grp, bsz),
        in_specs=[pl.BlockSpec((None, tq, LANES), lambda p, r, b: (b, r, p)),
                  pl.BlockSpec((None, c, LANES), lambda p, r, b: (b, 0, p)),
                  pl.BlockSpec((None, c, LANES), lambda p, r, b: (b, 0, p)),
                  pl.BlockSpec((None, s, LANES), lambda p, r, b: (b, 0, p)),
                  pl.BlockSpec((None, s, LANES), lambda p, r, b: (b, 0, p)),
                  pl.BlockSpec((None, None, 2 * tq, NA_KROWS * GRID_W), lambda p, r, b: (variant(r), p, 0, 0))],
        out_specs=pl.BlockSpec((None, tq, LANES), lambda p, r, b: (b, r, p)),
        compiler_params=_cparams(("parallel", "parallel", "parallel")),
        name="na_attn",
    )(qn, kcn, vcb, kn, vb, bias_tbl)


def _na_bias_kernel(rp_ref, o_ref, *, rows_total):
    n_grp = rows_total // NA_QROWS
    qc = lax.broadcasted_iota(jnp.int32, (GRID_W, LANES), 0)
    lane = lax.broadcasted_iota(jnp.int32, (GRID_W, LANES), 1)
    kc = lane % GRID_W
    cs = jnp.clip(qc - NA_COLS // 2, 0, GRID_W - NA_COLS)
    col_ok = (kc >= cs) & (kc < cs + NA_COLS)
    left = lane < GRID_W
    neg = jnp.full((GRID_W, LANES), NEG, f32)
    for v, r0 in enumerate((0, NA_QROWS, (n_grp - 1) * NA_QROWS)):
        start = min(max(r0 - NA_ROWS // 2, 0), rows_total - NA_KROWS)
        for hh in range(2):
            for i in range(NA_QROWS):
                qr = r0 + i
                ws = min(max(qr - NA_ROWS // 2, 0), rows_total - NA_ROWS)
                for jj in range(NA_KROWS // 2):
                    kra, krb = start + 2 * jj, start + 2 * jj + 1
                    ok_a, ok_b = ws <= kra < ws + NA_ROWS, ws <= krb < ws + NA_ROWS
                    r_lo = hh * NA_QROWS * GRID_W + i * GRID_W
                    dst = (v, slice(r_lo, r_lo + GRID_W), slice(jj * LANES, (jj + 1) * LANES))
                    if not (ok_a or ok_b):
                        o_ref[dst] = neg
                        continue
                    dra = min(max(kra - qr + NA_ROWS - 1, 0), 2 * NA_ROWS - 2)
                    drb = min(max(krb - qr + NA_ROWS - 1, 0), 2 * NA_ROWS - 2)
                    row = jnp.where(left[:1], rp_ref[hh, dra:dra + 1, :], rp_ref[hh, drb:drb + 1, :])
                    toe = pltpu.roll(jnp.broadcast_to(row, (GRID_W, LANES)), LANES - (NA_COLS - 1), axis=1,
                                     stride=1, stride_axis=0)
                    ok = col_ok if (ok_a and ok_b) else (col_ok & left if ok_a else col_ok & ~left)
                    o_ref[dst] = jnp.where(ok, toe, neg)


def na_bias_table(rpb, rows_total):
    h, ndr, ndc = rpb.shape
    rp = jnp.zeros((h, 16, LANES), f32).at[:, :ndr, :ndc].set(rpb).at[:, :ndr, GRID_W:GRID_W + ndc].set(rpb)
    return pl.pallas_call(
        functools.partial(_na_bias_kernel, rows_total=rows_total),
        out_shape=jax.ShapeDtypeStruct((3, h // 2, 2 * NA_QROWS * GRID_W, NA_KROWS * GRID_W), f32),
        grid=(h // 2,),
        in_specs=[pl.BlockSpec((2, 16, LANES), lambda p: (p, 0, 0))],
        out_specs=pl.BlockSpec((3, None, 2 * NA_QROWS * GRID_W, NA_KROWS * GRID_W), lambda p: (0, p, 0, 0)),
        compiler_params=_cparams(("parallel",)),
        name="na_bias",
    )(rp)


def _log_sigmoid(x):
    return jnp.minimum(x, 0.0) - jnp.log1p(jnp.exp(-jnp.abs(x)))


MLSTM_L = 128
MLSTM_HP = 4


def _mlstm_chunk(cx_ref, m_ref, d, masks, q, k_t, v, li, lf):
    tri, eye = masks
    cx, m = cx_ref[d], m_ref[d]
    b_col = jnp.sum(jnp.where(tri, lf, 0.0), axis=1, keepdims=True)
    b_row = jnp.sum(jnp.where(eye, b_col, 0.0), axis=0, keepdims=True)
    b_last = jnp.sum(lf, axis=1, keepdims=True)
    dmat = jnp.where(tri, b_col - b_row + li, -jnp.inf)
    inter = b_col + m
    m_t = jnp.maximum(inter, jnp.max(dmat, axis=1, keepdims=True))
    w = jnp.exp(dmat - m_t)
    a = jnp.exp(inter - m_t)
    vx = jnp.concatenate([v, jnp.ones_like(v)], axis=1)
    s = _dot(q, k_t.astype(bf16)) * w
    nd = a * _dot(q, cx.astype(bf16)) + _dot(s.astype(bf16), vx)
    h = nd[:, :A_DIM] / jnp.maximum(jnp.abs(nd[:, A_DIM:]), jnp.exp(-m_t))
    g_row = b_last - b_row + li
    m_new = jnp.maximum(b_last + m, jnp.max(g_row, axis=1, keepdims=True))
    decay = jnp.exp(b_last + m - m_new)
    wk_t = k_t * jnp.exp(g_row - m_new)
    cx_ref[d] = decay * cx + _dot(wk_t.astype(bf16), vx)
    m_ref[d] = m_new
    return h


def _mlstm_kernel(gb_ref, qc_ref, kc_ref, vc_ref, oc_ref, gc_ref, ql_ref, kl_ref, vl_ref, ol_ref, gl_ref, hg_ref,
                  outc_ref, outl_ref, cx_scr, m_scr, g_scr, hc_scr, hl_scr):
    hg = pl.program_id(1)
    L = MLSTM_L
    cx_scr[...] = jnp.zeros_like(cx_scr)
    m_scr[...] = jnp.zeros_like(m_scr)
    kscale = A_DIM ** -0.5
    t_idx = lax.broadcasted_iota(jnp.int32, (L, L), 0)
    s_idx = lax.broadcasted_iota(jnp.int32, (L, L), 1)
    eye = s_idx == t_idx
    masks = ((s_idx <= t_idx, eye), (s_idx >= t_idx, eye))
    n_gates = 4 * A_HEADS

    def run(q_ref, k_ref, v_ref, g_ref, h_ref):
        nc = q_ref.shape[0] // L

        def body(j, accumulate):
            for d in range(2):
                cj = j if d == 0 else nc - 1 - j
                rows = pl.ds(pl.multiple_of(cj * L, L), L)
                g_scr[d] = g_ref[rows, :].T[:n_gates]
                for hh in range(MLSTM_HP):
                    hd = hg * MLSTM_HP + hh
                    ig, fg = (2 * d) * A_HEADS + hd, (2 * d + 1) * A_HEADS + hd
                    li = g_scr[d, pl.ds(ig, 1), :] + gb_ref[ig]
                    lf = _log_sigmoid(g_scr[d, pl.ds(fg, 1), :] + gb_ref[fg])
                    cols = slice(hh * A_DIM, (hh + 1) * A_DIM)
                    k_t = (k_ref[rows, cols] * kscale).T
                    h = _mlstm_chunk(cx_scr, m_scr, 2 * hh + d, masks[d], q_ref[rows, cols].astype(bf16), k_t,
                                     v_ref[rows, cols].astype(bf16), li, lf)
                    if accumulate:
                        h_ref[rows, cols] += h
                    else:
                        h_ref[rows, cols] = h

        lax.fori_loop(0, nc // 2, lambda j, c: (body(j, False), c)[1], 0)
        lax.fori_loop(nc // 2, nc, lambda j, c: (body(j, True), c)[1], 0)

    run(qc_ref, kc_ref, vc_ref, gc_ref, hc_scr)
    run(ql_ref, kl_ref, vl_ref, gl_ref, hl_scr)
    for o_ref, h_ref, out_ref in ((oc_ref, hc_scr, outc_ref), (ol_ref, hl_scr, outl_ref)):
        for hh in range(MLSTM_HP):
            cols = slice(hh * A_DIM, (hh + 1) * A_DIM)
            h = h_ref[:, cols]
            hn = (h * lax.rsqrt(jnp.mean(h * h, axis=-1, keepdims=True) + EPS)) * hg_ref[:, cols]
            o = o_ref[:, cols]
            out_ref[:, cols] = (hn / (1.0 + jnp.exp(-o))).astype(out_ref.dtype)


def mlstm_mixer(y_ctx, y_lat, gate_block, gate_b, head_g):
    bsz, tc, _ = y_ctx.shape
    tl = y_lat.shape[1]
    wide = MLSTM_HP * A_DIM
    n_grp = A_HEADS // MLSTM_HP

    def col(t, base):
        return pl.BlockSpec((None, t, wide), functools.partial(lambda b, h, base: (b, 0, base + h), base=base))

    def gate_spec(t):
        return pl.BlockSpec((None, t, LANES), lambda b, h: (b, 0, gate_block))

    def out_spec(t):
        return pl.BlockSpec((None, t, wide), lambda b, h: (b, 0, h))

    return pl.pallas_call(
        _mlstm_kernel,
        out_shape=[jax.ShapeDtypeStruct((bsz, tc, A_HEADS * A_DIM), bf16),
                   jax.ShapeDtypeStruct((bsz, tl, A_HEADS * A_DIM), bf16)],
        grid=(bsz, n_grp),
        in_specs=[pl.BlockSpec(memory_space=pltpu.SMEM)]
        + [col(tc, n_grp * j) for j in range(4)] + [gate_spec(tc)]
        + [col(tl, n_grp * j) for j in range(4)] + [gate_spec(tl)]
        + [pl.BlockSpec((1, wide), lambda b, h: (0, h))],
        out_specs=[out_spec(tc), out_spec(tl)],
        scratch_shapes=[pltpu.VMEM((2 * MLSTM_HP, A_DIM, 2 * A_DIM), f32), pltpu.VMEM((2 * MLSTM_HP, 1, 1), f32),
                        pltpu.VMEM((2, 4 * A_HEADS, LANES), f32),
                        pltpu.VMEM((tc, wide), f32), pltpu.VMEM((tl, wide), f32)],
        compiler_params=_cparams(("parallel", "parallel")),
        name="mlstm",
    )(gate_b, y_ctx, y_ctx, y_ctx, y_ctx, y_ctx, y_lat, y_lat, y_lat, y_lat, y_lat,
      head_g.reshape(1, A_HEADS * A_DIM))


def _outproj_kernel(x_ref, a1_ref, a2_ref, w1_ref, w2_ref, gate_ref, o_ref):
    y = _dot(a1_ref[...], w1_ref[...]) + _dot(a2_ref[...], w2_ref[...])
    o_ref[...] = x_ref[...] + gate_ref[...] * y


def out_proj(x, a1, a2, w, modt, l, row, tm):
    bsz, t, d = x.shape
    k1, k2 = a1.shape[-1], a2.shape[-1]
    return pl.pallas_call(
        _outproj_kernel,
        out_shape=jax.ShapeDtypeStruct((bsz, t, d), f32),
        grid=(bsz, t // tm),
        in_specs=[pl.BlockSpec((None, tm, d), lambda b, i: (b, i, 0)),
                  pl.BlockSpec((None, tm, k1), lambda b, i: (b, i, 0)),
                  pl.BlockSpec((None, tm, k2), lambda b, i: (b, i, 0)),
                  pl.BlockSpec((k1, d), lambda b, i: (0, 0)),
                  pl.BlockSpec((k2, d), lambda b, i: (0, 0)),
                  _mod_spec(l, 2, row, d)],
        out_specs=pl.BlockSpec((None, tm, d), lambda b, i: (b, i, 0)),
        compiler_params=_cparams(("parallel", "parallel")),
        name="out_proj",
    )(x, a1, a2, w[:k1], w[k1:], modt)


FFN_HALO = 8


FFN_CHUNK = 256


def _ffn_kernel(x_ref, xp_ref, xn_ref, sh_ref, sc_ref, gate_ref, g_ref, wup_ref, cw_ref, cb_ref, wd_ref,
                o_ref, h_scr, u_scr, acc_scr, *, tm):
    i = pl.program_id(1)
    g, sh, sc = g_ref[...], sh_ref[...], sc_ref[...]
    hp = _modulated_norm(xp_ref[...], g, sh, sc)
    hn = _modulated_norm(xn_ref[...], g, sh, sc)
    h_scr[:FFN_HALO] = jnp.where(i > 0, hp, 0.0).astype(bf16)
    h_scr[FFN_HALO:FFN_HALO + tm] = _modulated_norm(x_ref[...], g, sh, sc).astype(bf16)
    h_scr[FFN_HALO + tm:] = jnp.where(i < pl.num_programs(1) - 1, hn, 0.0).astype(bf16)
    dff = wd_ref.shape[0]
    n_chunks = dff // FFN_CHUNK

    def cols(ref, c):
        lo = c * FFN_CHUNK
        return ref[:, lo:lo + FFN_CHUNK], ref[:, dff + lo:dff + lo + FFN_CHUNK]

    def up_proj(c):
        wg, wv = cols(wup_ref, c)
        u_scr[c % 2, :, :FFN_CHUNK] = _dot(h_scr[...], wg)
        u_scr[c % 2, :, FFN_CHUNK:] = _dot(h_scr[...], wv)

    up_proj(0)
    for c in range(n_chunks):
        u = u_scr.at[c % 2]
        acc = acc_scr.at[c % 2]
        if c + 1 < n_chunks:
            up_proj(c + 1)
        cw = jnp.concatenate(cols(cw_ref, c), axis=1)
        cb = jnp.concatenate(cols(cb_ref, c), axis=1)
        conv = (cw[0:1] * u[FFN_HALO - 1:FFN_HALO - 1 + tm] + cw[1:2] * u[FFN_HALO:FFN_HALO + tm]
                + cw[2:3] * u[FFN_HALO + 1:FFN_HALO + 1 + tm] + cb)
        cg, cv = conv[:, :FFN_CHUNK], conv[:, FFN_CHUNK:]
        act = (cg / (1.0 + jnp.exp(-cg))) * cv
        part = _dot(act.astype(bf16), wd_ref[c * FFN_CHUNK:(c + 1) * FFN_CHUNK, :])
        if c < 2:
            acc[...] = part
        else:
            acc[...] += part
    o_ref[...] = x_ref[...] + gate_ref[...] * (acc_scr[0] + acc_scr[1])


def ffn_weights(w_up, conv_w, conv_b, w_down):
    return w_up.astype(bf16), conv_w, conv_b.reshape(1, -1), w_down.astype(bf16)


def conv_ffn(x, modt, l, row, g, weights, tm):
    bsz, t, d = x.shape
    wup, cw, cb, wd = weights
    hb = tm // FFN_HALO
    last = t // FFN_HALO - 1

    def resident(a):
        return pl.BlockSpec(a.shape, lambda b, i: (0,) * a.ndim, pipeline_mode=pl.Buffered(1))

    return pl.pallas_call(
        functools.partial(_ffn_kernel, tm=tm),
        out_shape=jax.ShapeDtypeStruct((bsz, t, d), f32),
        grid=(bsz, t // tm),
        in_specs=[pl.BlockSpec((None, tm, d), lambda b, i: (b, i, 0)),
                  pl.BlockSpec((None, FFN_HALO, d), lambda b, i: (b, jnp.maximum(i * hb - 1, 0), 0)),
                  pl.BlockSpec((None, FFN_HALO, d), lambda b, i: (b, jnp.minimum((i + 1) * hb, last), 0)),
                  _mod_spec(l, 3, row, d), _mod_spec(l, 4, row, d), _mod_spec(l, 5, row, d),
                  pl.BlockSpec((1, d), lambda b, i: (0, 0)),
                  resident(wup), resident(cw), resident(cb), resident(wd)],
        out_specs=pl.BlockSpec((None, tm, d), lambda b, i: (b, i, 0)),
        scratch_shapes=[pltpu.VMEM((tm + 2 * FFN_HALO, d), bf16),
                        pltpu.VMEM((2, tm + 2 * FFN_HALO, 2 * FFN_CHUNK), f32),
                        pltpu.VMEM((2, tm, d), f32)],
        compiler_params=_cparams(("parallel", "parallel")),
        name="conv_ffn",
    )(x, x, x, modt, modt, modt, g.reshape(1, d), wup, cw, cb, wd)


def _rope_tables(n):
    t = jnp.arange(n)
    row = (t // GRID_W).astype(f32)
    colp = (t % GRID_W).astype(f32)
    half = HEAD_DIM // 2
    freq = ROPE_THETA ** (-jnp.arange(0, half, 2, dtype=f32) / half)
    ang_r = row[:, None] * freq[None, :]
    ang_c = colp[:, None] * freq[None, :]
    ang = jnp.concatenate([ang_r, ang_r, ang_c, ang_c] * 2, axis=-1)
    return jnp.cos(ang), jnp.sin(ang)


def _sink_column(sink, n_kv, tq):
    n_sub = sink.shape[0] // n_kv
    return jnp.repeat(sink.reshape(n_kv, n_sub), tq, axis=1)[..., None].astype(f32)


def kernel(x, c, ctx, c_ctx, ada_w, ada_b, norm_g, w_out, ffn_up, ffn_conv_w, ffn_conv_b, ffn_down, even_w_in,
           mlstm_gate_b, mlstm_head_g, swa_qk_g, swa_sink, odd_w_in, gqa_qk_g, na_qk_g, na_rpb):
    bsz, seq, d = x.shape
    n_ctx = ctx.shape[1]
    depth = ada_w.shape[0]
    a_w = A_HEADS * A_DIM
    cos, sin = _rope_tables(seq)
    ones_c, zeros_c = jnp.ones((n_ctx, LANES), f32), jnp.zeros((n_ctx, LANES), f32)

    cc = jnp.zeros((16, d), f32).at[:bsz].set(c).at[bsz].set(c_ctx)
    modt = ada_modulation(cc, ada_w, ada_b).reshape(depth, 16, 6, 1, d)

    x_lat, x_ctx = x, ctx
    tm_l, tm_c = 256, 256
    for l in range(depth):
        need_ctx = l < depth - 1
        w_o = w_out[l].astype(bf16)
        if l % 2 == 0:
            e = l // 2
            wi = even_w_in[e]
            sp = np.cumsum((0, a_w, a_w, a_w, a_w, 4 * A_HEADS, B_HEADS * HEAD_DIM, B_KV * HEAD_DIM, B_KV * HEAD_DIM))
            aq, ak, av, ao, ag, bq, bk, bv = (wi[:, sp[k]:sp[k + 1]] for k in range(8))
            w = jnp.concatenate([aq, ak, av, ao, ag, jnp.zeros((d, LANES - 4 * A_HEADS), f32), bq, bk, bv],
                                axis=1).astype(bf16)
            gate_col = 4 * a_w
            n_raw = gate_col + LANES
            gq, gk = swa_qk_g[e, 0], swa_qk_g[e, 1]
            cols = dict(n_raw=n_raw, q_cols=(n_raw,), kpair_cols=(n_raw + B_HEADS * HEAD_DIM,), kfull_cols=(),
                        vfull_cols=(), gains=(gq, gk))
            y_lat, qn_l, k2_l, v2_l = proj_prep(x_lat, modt, l, None, norm_g[l, 0], w, **cols, rope=(True, True),
                                                cos=cos, sin=sin, tm=tm_l)
            y_ctx, qn_c, k2_c, v2_c = proj_prep(x_ctx, modt, l, bsz, norm_g[l, 0], w, **cols, rope=(False, False),
                                                cos=ones_c, sin=zeros_c, tm=tm_c)
            a_ctx, a_lat = mlstm_mixer(y_ctx, y_lat, gate_col // LANES, mlstm_gate_b[e], mlstm_head_g[e])
            b_lat = window_attention(qn_l, k2_l, v2_l, k2_c, v2_c, _sink_column(swa_sink[e], B_KV, B_WIN))
            mix_l = (a_lat, b_lat)
            if need_ctx:
                b_ctx = gqa_attention(qn_c, [(k2_c, v2_c)], B_KV, 128, _sink_column(swa_sink[e], B_KV, 128))
                mix_c = (a_ctx, b_ctx)
        else:
            o = l // 2
            wi = odd_w_in[o]
            sp = np.cumsum((0, C_HEADS * HEAD_DIM, C_KV * HEAD_DIM, C_KV * HEAD_DIM) + (D_HEADS * HEAD_DIM,) * 3)
            cq, ck, cv, nq, nk, nv = (wi[:, sp[k]:sp[k + 1]] for k in range(6))
            w_ctx = jnp.concatenate([nk, nv, ck, cv], axis=1).astype(bf16)
            w = jnp.concatenate([cq.astype(bf16), nq.astype(bf16), w_ctx], axis=1)
            gq, gk = gqa_qk_g[o, 0], gqa_qk_g[o, 1]
            nq_g, nk_g = na_qk_g[o, 0], na_qk_g[o, 1]
            wide = 4 * LANES
            cqn, nqn, ck2, cv2, nkn, nvb = proj_prep(
                x_lat, modt, l, None, norm_g[l, 0], w, n_raw=0, q_cols=(0, wide), kpair_cols=(4 * wide,),
                kfull_cols=(2 * wide,), vfull_cols=(3 * wide,), gains=(gq, nq_g, gk, nk_g),
                rope=(True, False, True), cos=cos, sin=sin, tm=tm_l)
            if need_ctx:
                raise NotImplementedError("context outputs of an odd layer")
            ck2_c, cv2_c, nkn_c, nvb_c = proj_prep(
                x_ctx, modt, l, bsz, norm_g[l, 0], w_ctx, n_raw=0, q_cols=(), kpair_cols=(2 * wide,),
                kfull_cols=(0,), vfull_cols=(wide,), gains=(gk, nk_g), rope=(False,), cos=ones_c, sin=zeros_c,
                tm=tm_c)
            c_lat = gqa_attention(cqn, [(ck2_c, cv2_c), (ck2, cv2)], C_KV, 128)
            d_lat = neighbourhood_attention(nqn, nkn, nvb, nkn_c, nvb_c, na_bias_table(na_rpb[o], seq // GRID_W))
            mix_l = (c_lat, d_lat)

        ffn_w = ffn_weights(ffn_up[l], ffn_conv_w[l], ffn_conv_b[l], ffn_down[l])
        x_lat = out_proj(x_lat, mix_l[0], mix_l[1], w_o, modt, l, None, tm_l)
        x_lat = conv_ffn(x_lat, modt, l, None, norm_g[l, 1], ffn_w, tm=512)
        if need_ctx:
            x_ctx = out_proj(x_ctx, mix_c[0], mix_c[1], w_o, modt, l, bsz, tm_c)
            x_ctx = conv_ffn(x_ctx, modt, l, bsz, norm_g[l, 1], ffn_w, tm=256)
    return x_lat
```

```python
import functools

import jax
import jax.numpy as jnp
import numpy as np
from jax import lax
from jax.experimental import pallas as pl
from jax.experimental.pallas import tpu as pltpu

f32 = jnp.float32
bf16 = jnp.bfloat16

GRID_W = 64
HEAD_DIM = 64
LANES = 128
A_HEADS = 4
A_DIM = 128
B_HEADS = 8
B_KV = 2
B_WIN = 128
C_HEADS = 8
C_KV = 2
D_HEADS = 8
NA_ROWS = 8
NA_COLS = 16
NA_QROWS = 4
NA_KROWS = 12
ROPE_THETA = 10000.0
EPS = 1e-6
NEG = -1e30
VMEM_LIMIT = 56 * 1024 * 1024


def _cparams(sem):
    return pltpu.CompilerParams(dimension_semantics=sem, vmem_limit_bytes=VMEM_LIMIT)


def _dot(a, b):
    return jnp.dot(a, b, preferred_element_type=f32)


def _dot_nt(a, b):
    return lax.dot_general(a, b, (((1,), (1,)), ((), ())), preferred_element_type=f32)


def _modulated_norm(x, g, shift, scale):
    y = x * lax.rsqrt(jnp.mean(x * x, axis=-1, keepdims=True) + EPS)
    return (y * g) * (1.0 + scale) + shift


def _mod_spec(l, k, row, d):
    if row is None:
        return pl.BlockSpec((None, None, None, 1, d), lambda b, *_: (l, b, k, 0, 0))
    return pl.BlockSpec((None, None, None, 1, d), lambda b, *_: (l, row, k, 0, 0))


def _ada_kernel(c_ref, w_ref, b_ref, o_ref):
    c = c_ref[...]
    s = c / (1.0 + jnp.exp(-c))
    o_ref[...] = _dot(s.astype(bf16), w_ref[...].astype(bf16)) + b_ref[...]


def ada_modulation(cc, ada_w, ada_b):
    depth, d, n = ada_w.shape
    tn = 1536
    return pl.pallas_call(
        _ada_kernel,
        out_shape=jax.ShapeDtypeStruct((depth, cc.shape[0], n), f32),
        grid=(depth, n // tn),
        in_specs=[pl.BlockSpec(cc.shape, lambda l, j: (0, 0)),
                  pl.BlockSpec((None, d, tn), lambda l, j: (l, 0, j)),
                  pl.BlockSpec((None, 1, tn), lambda l, j: (l, 0, j))],
        out_specs=pl.BlockSpec((None, cc.shape[0], tn), lambda l, j: (l, 0, j)),
        compiler_params=_cparams(("parallel", "parallel")),
        name="ada",
    )(cc, ada_w, ada_b.reshape(depth, 1, n))


def _head_ms(a):
    ri = lax.broadcasted_iota(jnp.int32, (LANES, LANES), 0) // HEAD_DIM
    ci = lax.broadcasted_iota(jnp.int32, (LANES, LANES), 1) // HEAD_DIM
    bd = jnp.where(ri == ci, 1.0, 0.0).astype(bf16)
    ss = a * a
    hi = ss.astype(bf16)
    lo = (ss - hi.astype(f32)).astype(bf16)
    return (_dot(hi, bd) + _dot(lo, bd)) * (1.0 / HEAD_DIM)


def _head_norm(a, g):
    return (a * lax.rsqrt(_head_ms(a) + EPS)) * g


def _rope(a, cos, sin):
    lane = lax.broadcasted_iota(jnp.int32, a.shape, 1)
    quarter = HEAD_DIM // 4
    rot = jnp.where(lane % (2 * quarter) < quarter,
                    -pltpu.roll(a, LANES - quarter, axis=1), pltpu.roll(a, quarter, axis=1))
    return a * cos + rot * sin


def _dup_halves(a):
    lane = lax.broadcasted_iota(jnp.int32, a.shape, 1)
    sw = pltpu.roll(a, HEAD_DIM, axis=1)
    lo = lane < HEAD_DIM
    return jnp.where(lo, a, sw), jnp.where(lo, sw, a)


def _prep_kernel(*refs, n_q, n_kpair, n_kfull, n_vfull, rope, scale):
    it = iter(refs)
    q_refs = [next(it) for _ in range(n_q)]
    kp_refs = [next(it) for _ in range(n_kpair)]
    kf_refs = [next(it) for _ in range(n_kfull)]
    vf_refs = [next(it) for _ in range(n_vfull)]
    g_refs = [next(it) for _ in range(n_q + n_kpair + n_kfull)]
    cos_ref, sin_ref = next(it), next(it)
    qo_refs = [next(it) for _ in range(n_q)]
    kpo_refs = [(next(it), next(it)) for _ in range(n_kpair)]
    kfo_refs = [next(it) for _ in range(n_kfull)]
    vfo_refs = [next(it) for _ in range(n_vfull)]
    gi = iter(g_refs)
    for qi, (q_ref, qo_ref) in enumerate(zip(q_refs, qo_refs)):
        g = next(gi)[...]
        for p in range(q_ref.shape[-1] // LANES):
            a = _head_norm(q_ref[:, p * LANES:(p + 1) * LANES], g)
            if rope[qi]:
                a = _rope(a, cos_ref[...], sin_ref[...])
            qo_ref[:, p * LANES:(p + 1) * LANES] = (a * scale).astype(bf16)
    for ki, (kp_ref, (ko_ref, vo_ref)) in enumerate(zip(kp_refs, kpo_refs)):
        g = next(gi)[...]
        k = _head_norm(kp_ref[:, :LANES], g)
        if rope[n_q + ki]:
            k = _rope(k, cos_ref[...], sin_ref[...])
        k0, k1 = _dup_halves(k)
        ko_ref[0] = k0.astype(bf16)
        ko_ref[1] = k1.astype(bf16)
        v0, v1 = _dup_halves(kp_ref[:, LANES:])
        vo_ref[0] = v0.astype(bf16)
        vo_ref[1] = v1.astype(bf16)
    for kf_ref, kfo_ref in zip(kf_refs, kfo_refs):
        g = next(gi)[...]
        for p in range(kf_ref.shape[-1] // LANES):
            kfo_ref[:, p * LANES:(p + 1) * LANES] = _head_norm(kf_ref[:, p * LANES:(p + 1) * LANES], g).astype(bf16)
    for vf_ref, vfo_ref in zip(vf_refs, vfo_refs):
        vfo_ref[...] = vf_ref[...].astype(bf16)


def _proj_prep_kernel(x_ref, sh_ref, sc_ref, g_ref, w_ref, *rest, n_raw, q_cols, kpair_cols, kfull_cols, vfull_cols,
                      rope):
    n_gain = len(q_cols) + len(kpair_cols) + len(kfull_cols)
    side = rest[:n_gain + 2]
    outs = rest[n_gain + 2:-1]
    y_scr = rest[-1]
    h = _modulated_norm(x_ref[...], g_ref[...], sh_ref[...], sc_ref[...])
    y_scr[...] = _dot(h.astype(bf16), w_ref[...])
    if n_raw:
        outs[0][...] = y_scr[:, :n_raw]
        outs = outs[1:]
    wide, pair = 4 * LANES, 2 * LANES
    views = ([y_scr.at[:, c:c + wide] for c in q_cols] + [y_scr.at[:, c:c + pair] for c in kpair_cols]
             + [y_scr.at[:, c:c + wide] for c in kfull_cols + vfull_cols])
    _prep_kernel(*views, *side, *outs, n_q=len(q_cols), n_kpair=len(kpair_cols), n_kfull=len(kfull_cols),
                 n_vfull=len(vfull_cols), rope=rope, scale=HEAD_DIM ** -0.5)


def proj_prep(x, modt, l, row, norm_g, w, *, n_raw, q_cols, kpair_cols, kfull_cols, vfull_cols, gains, rope, cos, sin,
              tm):
    bsz, t, d = x.shape
    n = w.shape[1]
    wide = 4 * LANES
    in_specs = [pl.BlockSpec((None, tm, d), lambda b, i: (b, i, 0)),
                _mod_spec(l, 0, row, d), _mod_spec(l, 1, row, d),
                pl.BlockSpec((1, d), lambda b, i: (0, 0)),
                pl.BlockSpec((d, n), lambda b, i: (0, 0))]
    args = [x, modt, modt, norm_g.reshape(1, d), w]
    for g in gains:
        in_specs.append(pl.BlockSpec((1, LANES), lambda b, i: (0, 0)))
        args.append(jnp.tile(g, 2).reshape(1, LANES))
    for tbl in (cos, sin):
        in_specs.append(pl.BlockSpec((tm, LANES), lambda b, i: (i, 0)))
        args.append(tbl)
    out_shape, out_specs = [], []
    if n_raw:
        out_shape.append(jax.ShapeDtypeStruct((bsz, t, n_raw), f32))
        out_specs.append(pl.BlockSpec((None, tm, n_raw), lambda b, i: (b, i, 0)))
    for _ in q_cols:
        out_shape.append(jax.ShapeDtypeStruct((bsz, t, wide), bf16))
        out_specs.append(pl.BlockSpec((None, tm, wide), lambda b, i: (b, i, 0)))
    for _ in kpair_cols:
        for _ in range(2):
            out_shape.append(jax.ShapeDtypeStruct((bsz, 2, t, LANES), bf16))
            out_specs.append(pl.BlockSpec((None, 2, tm, LANES), lambda b, i: (b, 0, i, 0)))
    for _ in kfull_cols + vfull_cols:
        out_shape.append(jax.ShapeDtypeStruct((bsz, t, wide), bf16))
        out_specs.append(pl.BlockSpec((None, tm, wide), lambda b, i: (b, i, 0)))
    kern = functools.partial(_proj_prep_kernel, n_raw=n_raw, q_cols=q_cols, kpair_cols=kpair_cols,
                             kfull_cols=kfull_cols, vfull_cols=vfull_cols, rope=rope)
    return pl.pallas_call(
        kern, out_shape=out_shape, grid=(bsz, t // tm), in_specs=in_specs, out_specs=out_specs,
        scratch_shapes=[pltpu.VMEM((tm, n), f32)],
        compiler_params=_cparams(("parallel", "parallel")), name="proj_prep",
    )(*args)


def _stack_heads(q_ref, n_sub):
    parts = []
    for g in range(n_sub):
        blk = q_ref[:, (g // 2) * LANES:(g // 2 + 1) * LANES]
        lane = lax.broadcasted_iota(jnp.int32, blk.shape, 1)
        keep = (lane < HEAD_DIM) if g % 2 == 0 else (lane >= HEAD_DIM)
        parts.append(jnp.where(keep, blk, jnp.zeros_like(blk)))
    return jnp.concatenate(parts, axis=0)


def _scores(q, segs):
    scores = []
    for k, _, bias, mask in segs:
        s = _dot_nt(q, k)
        if bias is not None:
            s = s + bias
        if mask is not None:
            s = jnp.where(mask, s, NEG)
        scores.append(s)
    return scores


def _softmax_attend(q, segs, sink, scores=None):
    return _softmax_attend_chains([(q, segs, sink, scores)])[0]


def _softmax_attend_chains(chains):
    scores = [_scores(q, segs) if sc is None else sc for q, segs, _, sc in chains]
    maxes = [[s.max(axis=-1, keepdims=True) for s in sc] for sc in scores]
    m = []
    for (_, _, sink, _), mx in zip(chains, maxes):
        mi = functools.reduce(jnp.maximum, mx)
        m.append(mi if sink is None else jnp.maximum(mi, sink))
    p = [[jnp.exp(s - m[i]) for s in sc] for i, sc in enumerate(scores)]
    outs = []
    for i, (_, segs, sink, _) in enumerate(chains):
        den = functools.reduce(jnp.add, [pj.sum(axis=-1, keepdims=True) for pj in p[i]])
        if sink is not None:
            den = den + jnp.exp(sink - m[i])
        acc = functools.reduce(jnp.add, [_dot(pj.astype(bf16), seg[1]) for pj, seg in zip(p[i], segs)])
        outs.append(acc / den)
    return outs


def _unstack_heads(o, o_ref, n_sub, tq):
    lane = lax.broadcasted_iota(jnp.int32, (tq, LANES), 1)
    for p in range(n_sub // 2):
        even = o[(2 * p) * tq:(2 * p + 1) * tq]
        odd = o[(2 * p + 1) * tq:(2 * p + 2) * tq]
        o_ref[:, p * LANES:(p + 1) * LANES] = jnp.where(lane < HEAD_DIM, even, odd).astype(o_ref.dtype)


def _window_attn_kernel(q_ref, kc_ref, vc_ref, kp_ref, k0_ref, kn_ref, vp_ref, v0_ref, vn_ref, sink_ref, o_ref, *, tq):
    i = pl.program_id(2)
    nb = pl.num_programs(2)
    n_sub = B_HEADS // B_KV
    n_ctx = kc_ref.shape[0]
    rows, n_keys = n_sub * tq, n_ctx + 3 * tq
    qpos = lax.broadcasted_iota(jnp.int32, (rows, n_keys), 0) % tq
    kpos = lax.broadcasted_iota(jnp.int32, (rows, n_keys), 1) - n_ctx
    bad_prev = (kpos >= 0) & (kpos < tq) & ((kpos < qpos) | (i == 0))
    bad_next = (kpos >= 2 * tq) & ((kpos - 2 * tq > qpos) | (i == nb - 1))
    k_all = jnp.concatenate([kc_ref[...], kp_ref[...], k0_ref[...], kn_ref[...]], axis=0)
    v_all = jnp.concatenate([vc_ref[...], vp_ref[...], v0_ref[...], vn_ref[...]], axis=0)
    o = _softmax_attend(_stack_heads(q_ref, n_sub), [(k_all, v_all, None, ~(bad_prev | bad_next))], sink_ref[...])
    _unstack_heads(o, o_ref, n_sub, tq)


def window_attention(qn, k2, v2, kc2, vc2, sink_col):
    bsz, s, _ = qn.shape
    c = kc2.shape[2]
    tq = B_WIN
    nb = s // tq
    n_sub = B_HEADS // B_KV
    wq = n_sub * HEAD_DIM
    half_spec = lambda fn: pl.BlockSpec((None, None, tq, LANES), fn)
    prev = lambda b, h, i: (b, h, jnp.maximum(i - 1, 0), 0)
    nxt = lambda b, h, i: (b, h, jnp.minimum(i + 1, nb - 1), 0)
    cur_spec = pl.BlockSpec((None, None, tq, LANES), lambda b, h, i: (b, h, i, 0))
    ctx_spec = pl.BlockSpec((None, None, c, LANES), lambda b, h, i: (b, h, 0, 0))
    return pl.pallas_call(
        functools.partial(_window_attn_kernel, tq=tq),
        out_shape=jax.ShapeDtypeStruct((bsz, s, B_HEADS * HEAD_DIM), bf16),
        grid=(bsz, B_KV, nb),
        in_specs=[pl.BlockSpec((None, tq, wq), lambda b, h, i: (b, i, h)),
                  ctx_spec, ctx_spec,
                  half_spec(prev), cur_spec, half_spec(nxt),
                  half_spec(prev), cur_spec, half_spec(nxt),
                  pl.BlockSpec((None, n_sub * tq, 1), lambda b, h, i: (h, 0, 0))],
        out_specs=pl.BlockSpec((None, tq, wq), lambda b, h, i: (b, i, h)),
        compiler_params=_cparams(("parallel", "parallel", "parallel")),
        name="window_attn",
    )(qn, kc2, vc2, k2, k2, k2, v2, v2, v2, sink_col)


def _seg_attn_kernel(*refs, n_seg, n_sub, tq, has_sink):
    q_ref = refs[0]
    kv = refs[1:1 + 2 * n_seg]
    sink = refs[1 + 2 * n_seg][...] if has_sink else None
    o_ref = refs[-1]
    segs = [(kv[2 * j][...], kv[2 * j + 1][...], None, None) for j in range(n_seg)]
    n_pair = n_sub // 2
    lanes = [slice(p * LANES, (p + 1) * LANES) for p in range(n_pair)]
    chains = [(_stack_heads(q_ref.at[:, lanes[p]], 2), segs,
               None if sink is None else sink[2 * p * tq:(2 * p + 2) * tq], None) for p in range(n_pair)]
    for p, o in enumerate(_softmax_attend_chains(chains)):
        _unstack_heads(o, o_ref.at[:, lanes[p]], 2, tq)


def gqa_attention(qn, kv_segs, n_kv, tq, sink_col=None):
    bsz, s, width = qn.shape
    n_sub = width // HEAD_DIM // n_kv
    wq = n_sub * HEAD_DIM
    in_specs = [pl.BlockSpec((None, tq, wq), lambda b, h, i: (b, i, h))]
    args = [qn]
    for k2, v2 in kv_segs:
        n = k2.shape[2]
        spec = pl.BlockSpec((None, None, n, LANES), lambda b, h, i: (b, h, 0, 0))
        in_specs += [spec, spec]
        args += [k2, v2]
    if sink_col is not None:
        in_specs.append(pl.BlockSpec((None, n_sub * tq, 1), lambda b, h, i: (h, 0, 0)))
        args.append(sink_col)
    return pl.pallas_call(
        functools.partial(_seg_attn_kernel, n_seg=len(kv_segs), n_sub=n_sub, tq=tq, has_sink=sink_col is not None),
        out_shape=jax.ShapeDtypeStruct((bsz, s, width), bf16),
        grid=(bsz, n_kv, s // tq),
        in_specs=in_specs,
        out_specs=pl.BlockSpec((None, tq, wq), lambda b, h, i: (b, i, h)),
        compiler_params=_cparams(("parallel", "parallel", "parallel")),
        name="gqa_attn",
    )(*args)


def _na_attn_kernel(q_ref, kc_ref, vc_ref, k_ref, v_ref, bias_ref, o_ref, *, tq, n_grp):
    r = pl.program_id(1)
    rows_total = k_ref.shape[0] // GRID_W
    start = jnp.clip(r * NA_QROWS - NA_ROWS // 2, 0, rows_total - NA_KROWS)
    off = pl.multiple_of(start * GRID_W, GRID_W)
    nk = NA_KROWS * GRID_W
    blk = q_ref[...]
    lane = lax.broadcasted_iota(jnp.int32, blk.shape, 1)
    k_nb, v_nb = k_ref[pl.ds(off, nk), :], v_ref[pl.ds(off, nk), :]
    chains = []
    for half in range(2):
        keep = (lane < HEAD_DIM) if half == 0 else (lane >= HEAD_DIM)
        segs = [(kc_ref[...], vc_ref[...], None, None),
                (k_nb, v_nb, bias_ref[half * tq:(half + 1) * tq, :], None)]
        chains.append((jnp.where(keep, blk, jnp.zeros_like(blk)), segs, None, None))
    outs = _softmax_attend_chains(chains)
    o_ref[...] = jnp.where(lane < HEAD_DIM, outs[0], outs[1]).astype(o_ref.dtype)


def neighbourhood_attention(qn, kn, vb, kcn, vcb, bias_tbl):
    bsz, s, width = qn.shape
    c = kcn.shape[1]
    tq = NA_QROWS * GRID_W
    n_grp = s // tq
    n_pair = width // LANES
    variant = lambda r: jnp.where(r == 0, 0, jnp.where(r == n_grp - 1, 2, 1))
    return pl.pallas_call(
        functools.partial(_na_attn_kernel, tq=tq, n_grp=n_grp),
        out_shape=jax.ShapeDtypeStruct((bsz, s, width), bf16),
        grid=(n_pair, n_grp, bsz),
        in_specs=[pl.BlockSpec((None, tq, LANES), lambda p, r, b: (b, r, p)),
                  pl.BlockSpec((None, c, LANES), lambda p, r, b: (b, 0, p)),
                  pl.BlockSpec((None, c, LANES), lambda p, r, b: (b, 0, p)),
                  pl.BlockSpec((None, s, LANES), lambda p, r, b: (b, 0, p)),
                  pl.BlockSpec((None, s, LANES), lambda p, r, b: (b, 0, p)),
                  pl.BlockSpec((None, None, 2 * tq, NA_KROWS * GRID_W), lambda p, r, b: (variant(r), p, 0, 0))],
        out_specs=pl.BlockSpec((None, tq, LANES), lambda p, r, b: (b, r, p)),
        compiler_params=_cparams(("parallel", "parallel", "parallel")),
        name="na_attn",
    )(qn, kcn, vcb, kn, vb, bias_tbl)


def _na_bias_kernel(rp_ref, o_ref, *, rows_total):
    n_grp = rows_total // NA_QROWS
    qc = lax.broadcasted_iota(jnp.int32, (GRID_W, LANES), 0)
    lane = lax.broadcasted_iota(jnp.int32, (GRID_W, LANES), 1)
    kc = lane % GRID_W
    cs = jnp.clip(qc - NA_COLS // 2, 0, GRID_W - NA_COLS)
    col_ok = (kc >= cs) & (kc < cs + NA_COLS)
    left = lane < GRID_W
    neg = jnp.full((GRID_W, LANES), NEG, f32)
    for v, r0 in enumerate((0, NA_QROWS, (n_grp - 1) * NA_QROWS)):
        start = min(max(r0 - NA_ROWS // 2, 0), rows_total - NA_KROWS)
        for hh in range(2):
            for i in range(NA_QROWS):
                qr = r0 + i
                ws = min(max(qr - NA_ROWS // 2, 0), rows_total - NA_ROWS)
                for jj in range(NA_KROWS // 2):
                    kra, krb = start + 2 * jj, start + 2 * jj + 1
                    ok_a, ok_b = ws <= kra < ws + NA_ROWS, ws <= krb < ws + NA_ROWS
                    r_lo = hh * NA_QROWS * GRID_W + i * GRID_W
                    dst = (v, slice(r_lo, r_lo + GRID_W), slice(jj * LANES, (jj + 1) * LANES))
                    if not (ok_a or ok_b):
                        o_ref[dst] = neg
                        continue
                    dra = min(max(kra - qr + NA_ROWS - 1, 0), 2 * NA_ROWS - 2)
                    drb = min(max(krb - qr + NA_ROWS - 1, 0), 2 * NA_ROWS - 2)
                    row = jnp.where(left[:1], rp_ref[hh, dra:dra + 1, :], rp_ref[hh, drb:drb + 1, :])
                    toe = pltpu.roll(jnp.broadcast_to(row, (GRID_W, LANES)), LANES - (NA_COLS - 1), axis=1,
                                     stride=1, stride_axis=0)
                    ok = col_ok if (ok_a and ok_b) else (col_ok & left if ok_a else col_ok & ~left)
                    o_ref[dst] = jnp.where(ok, toe, neg)


def na_bias_table(rpb, rows_total):
    h, ndr, ndc = rpb.shape
    rp = jnp.zeros((h, 16, LANES), f32).at[:, :ndr, :ndc].set(rpb).at[:, :ndr, GRID_W:GRID_W + ndc].set(rpb)
    return pl.pallas_call(
        functools.partial(_na_bias_kernel, rows_total=rows_total),
        out_shape=jax.ShapeDtypeStruct((3, h // 2, 2 * NA_QROWS * GRID_W, NA_KROWS * GRID_W), f32),
        grid=(h // 2,),
        in_specs=[pl.BlockSpec((2, 16, LANES), lambda p: (p, 0, 0))],
        out_specs=pl.BlockSpec((3, None, 2 * NA_QROWS * GRID_W, NA_KROWS * GRID_W), lambda p: (0, p, 0, 0)),
        compiler_params=_cparams(("parallel",)),
        name="na_bias",
    )(rp)


def _log_sigmoid(x):
    return jnp.minimum(x, 0.0) - jnp.log1p(jnp.exp(-jnp.abs(x)))


MLSTM_L = 128
MLSTM_HP = 4


def _mlstm_chunks(cx_ref, m_ref, eye, chains):
    every = lambda fn: [fn(i, ch) for i, ch in enumerate(chains)]
    cx = every(lambda i, ch: cx_ref[ch["slot"]])
    m = every(lambda i, ch: m_ref[ch["slot"]])
    b_col = every(lambda i, ch: jnp.sum(jnp.where(ch["tri"], ch["lf"], 0.0), axis=1, keepdims=True))
    b_row = every(lambda i, ch: jnp.sum(jnp.where(eye, b_col[i], 0.0), axis=0, keepdims=True))
    b_last = every(lambda i, ch: jnp.sum(ch["lf"], axis=1, keepdims=True))
    dmat = every(lambda i, ch: jnp.where(ch["tri"], b_col[i] - b_row[i] + ch["li"], -jnp.inf))
    inter = every(lambda i, ch: b_col[i] + m[i])
    m_t = every(lambda i, ch: jnp.maximum(inter[i], jnp.max(dmat[i], axis=1, keepdims=True)))
    w = every(lambda i, ch: jnp.exp(dmat[i] - m_t[i]))
    a = every(lambda i, ch: jnp.exp(inter[i] - m_t[i]))
    vx = every(lambda i, ch: jnp.concatenate([ch["v"], jnp.ones_like(ch["v"])], axis=1))
    s = every(lambda i, ch: _dot(ch["q"], ch["k_t"].astype(bf16)) * w[i])
    qc = every(lambda i, ch: _dot(ch["q"], cx[i].astype(bf16)))
    nd = every(lambda i, ch: a[i] * qc[i] + _dot(s[i].astype(bf16), vx[i]))
    h = every(lambda i, ch: nd[i][:, :A_DIM] / jnp.maximum(jnp.abs(nd[i][:, A_DIM:]), jnp.exp(-m_t[i])))
    g_row = every(lambda i, ch: b_last[i] - b_row[i] + ch["li"])
    m_new = every(lambda i, ch: jnp.maximum(b_last[i] + m[i], jnp.max(g_row[i], axis=1, keepdims=True)))
    decay = every(lambda i, ch: jnp.exp(b_last[i] + m[i] - m_new[i]))
    wk_t = every(lambda i, ch: (ch["k_t"] * jnp.exp(g_row[i] - m_new[i])).astype(bf16))
    for i, ch in enumerate(chains):
        cx_ref[ch["slot"]] = decay[i] * cx[i] + _dot(wk_t[i], vx[i])
        m_ref[ch["slot"]] = m_new[i]
    return h


def _mlstm_kernel(gb_ref, qc_ref, kc_ref, vc_ref, oc_ref, gc_ref, ql_ref, kl_ref, vl_ref, ol_ref, gl_ref, hg_ref,
                  outc_ref, outl_ref, cx_scr, m_scr, g_scr, hc_scr, hl_scr):
    hg = pl.program_id(1)
    L = MLSTM_L
    cx_scr[...] = jnp.zeros_like(cx_scr)
    m_scr[...] = jnp.zeros_like(m_scr)
    kscale = A_DIM ** -0.5
    t_idx = lax.broadcasted_iota(jnp.int32, (L, L), 0)
    s_idx = lax.broadcasted_iota(jnp.int32, (L, L), 1)
    eye = s_idx == t_idx
    tri = (s_idx <= t_idx, s_idx >= t_idx)
    n_gates = 4 * A_HEADS

    def run(q_ref, k_ref, v_ref, g_ref, h_ref):
        nc = q_ref.shape[0] // L

        def body(j, accumulate):
            chains = []
            for d in range(2):
                cj = j if d == 0 else nc - 1 - j
                rows = pl.ds(pl.multiple_of(cj * L, L), L)
                g_scr[d] = g_ref[rows, :].T[:n_gates]
                for hh in range(MLSTM_HP):
                    hd = hg * MLSTM_HP + hh
                    ig, fg = (2 * d) * A_HEADS + hd, (2 * d + 1) * A_HEADS + hd
                    cols = slice(hh * A_DIM, (hh + 1) * A_DIM)
                    chains.append(dict(
                        slot=2 * hh + d, tri=tri[d], rows=rows, cols=cols,
                        li=g_scr[d, pl.ds(ig, 1), :] + gb_ref[ig],
                        lf=_log_sigmoid(g_scr[d, pl.ds(fg, 1), :] + gb_ref[fg]),
                        q=q_ref[rows, cols].astype(bf16), v=v_ref[rows, cols].astype(bf16),
                        k_t=(k_ref[rows, cols] * kscale).T))
            for ch, h in zip(chains, _mlstm_chunks(cx_scr, m_scr, eye, chains)):
                if accumulate:
                    h_ref[ch["rows"], ch["cols"]] += h
                else:
                    h_ref[ch["rows"], ch["cols"]] = h

        lax.fori_loop(0, nc // 2, lambda j, c: (body(j, False), c)[1], 0)
        lax.fori_loop(nc // 2, nc, lambda j, c: (body(j, True), c)[1], 0)

    run(qc_ref, kc_ref, vc_ref, gc_ref, hc_scr)
    run(ql_ref, kl_ref, vl_ref, gl_ref, hl_scr)
    for o_ref, h_ref, out_ref in ((oc_ref, hc_scr, outc_ref), (ol_ref, hl_scr, outl_ref)):
        for hh in range(MLSTM_HP):
            cols = slice(hh * A_DIM, (hh + 1) * A_DIM)
            h = h_ref[:, cols]
            hn = (h * lax.rsqrt(jnp.mean(h * h, axis=-1, keepdims=True) + EPS)) * hg_ref[:, cols]
            o = o_ref[:, cols]
            out_ref[:, cols] = (hn / (1.0 + jnp.exp(-o))).astype(out_ref.dtype)


def mlstm_mixer(y_ctx, y_lat, gate_block, gate_b, head_g):
    bsz, tc, _ = y_ctx.shape
    tl = y_lat.shape[1]
    wide = MLSTM_HP * A_DIM
    n_grp = A_HEADS // MLSTM_HP

    def col(t, base):
        return pl.BlockSpec((None, t, wide), functools.partial(lambda b, h, base: (b, 0, base + h), base=base))

    def gate_spec(t):
        return pl.BlockSpec((None, t, LANES), lambda b, h: (b, 0, gate_block))

    def out_spec(t):
        return pl.BlockSpec((None, t, wide), lambda b, h: (b, 0, h))

    return pl.pallas_call(
        _mlstm_kernel,
        out_shape=[jax.ShapeDtypeStruct((bsz, tc, A_HEADS * A_DIM), bf16),
                   jax.ShapeDtypeStruct((bsz, tl, A_HEADS * A_DIM), bf16)],
        grid=(bsz, n_grp),
        in_specs=[pl.BlockSpec(memory_space=pltpu.SMEM)]
        + [col(tc, n_grp * j) for j in range(4)] + [gate_spec(tc)]
        + [col(tl, n_grp * j) for j in range(4)] + [gate_spec(tl)]
        + [pl.BlockSpec((1, wide), lambda b, h: (0, h))],
        out_specs=[out_spec(tc), out_spec(tl)],
        scratch_shapes=[pltpu.VMEM((2 * MLSTM_HP, A_DIM, 2 * A_DIM), f32), pltpu.VMEM((2 * MLSTM_HP, 1, 1), f32),
                        pltpu.VMEM((2, 4 * A_HEADS, LANES), f32),
                        pltpu.VMEM((tc, wide), f32), pltpu.VMEM((tl, wide), f32)],
        compiler_params=_cparams(("parallel", "parallel")),
        name="mlstm",
    )(gate_b, y_ctx, y_ctx, y_ctx, y_ctx, y_ctx, y_lat, y_lat, y_lat, y_lat, y_lat,
      head_g.reshape(1, A_HEADS * A_DIM))


def _outproj_kernel(x_ref, a1_ref, a2_ref, w1_ref, w2_ref, gate_ref, o_ref):
    y = _dot(a1_ref[...], w1_ref[...]) + _dot(a2_ref[...], w2_ref[...])
    o_ref[...] = x_ref[...] + gate_ref[...] * y


def out_proj(x, a1, a2, w, modt, l, row, tm):
    bsz, t, d = x.shape
    k1, k2 = a1.shape[-1], a2.shape[-1]
    return pl.pallas_call(
        _outproj_kernel,
        out_shape=jax.ShapeDtypeStruct((bsz, t, d), f32),
        grid=(bsz, t // tm),
        in_specs=[pl.BlockSpec((None, tm, d), lambda b, i: (b, i, 0)),
                  pl.BlockSpec((None, tm, k1), lambda b, i: (b, i, 0)),
                  pl.BlockSpec((None, tm, k2), lambda b, i: (b, i, 0)),
                  pl.BlockSpec((k1, d), lambda b, i: (0, 0)),
                  pl.BlockSpec((k2, d), lambda b, i: (0, 0)),
                  _mod_spec(l, 2, row, d)],
        out_specs=pl.BlockSpec((None, tm, d), lambda b, i: (b, i, 0)),
        compiler_params=_cparams(("parallel", "parallel")),
        name="out_proj",
    )(x, a1, a2, w[:k1], w[k1:], modt)


FFN_HALO = 8


FFN_CHUNK = 256


def _ffn_kernel(x_ref, xp_ref, xn_ref, sh_ref, sc_ref, gate_ref, g_ref, wup_ref, cw_ref, cb_ref, wd_ref,
                o_ref, h_scr, u_scr, acc_scr, *, tm):
    i = pl.program_id(1)
    g, sh, sc = g_ref[...], sh_ref[...], sc_ref[...]
    hp = _modulated_norm(xp_ref[...], g, sh, sc)
    hn = _modulated_norm(xn_ref[...], g, sh, sc)
    h_scr[:FFN_HALO] = jnp.where(i > 0, hp, 0.0).astype(bf16)
    h_scr[FFN_HALO:FFN_HALO + tm] = _modulated_norm(x_ref[...], g, sh, sc).astype(bf16)
    h_scr[FFN_HALO + tm:] = jnp.where(i < pl.num_programs(1) - 1, hn, 0.0).astype(bf16)
    dff = wd_ref.shape[0]
    n_chunks = dff // FFN_CHUNK

    def cols(ref, c):
        lo = c * FFN_CHUNK
        return ref[:, lo:lo + FFN_CHUNK], ref[:, dff + lo:dff + lo + FFN_CHUNK]

    half = tm // 2
    up_rows = (slice(0, half + 2 * FFN_HALO), slice(half + 2 * FFN_HALO, tm + 2 * FFN_HALO))

    n_buf = u_scr.shape[0]

    def up_proj(c, rows):
        wg, wv = cols(wup_ref, c)
        u_scr[c % n_buf, rows, :FFN_CHUNK] = _dot(h_scr[rows, :], wg)
        u_scr[c % n_buf, rows, FFN_CHUNK:] = _dot(h_scr[rows, :], wv)

    def gated(c, r):
        u = u_scr.at[c % n_buf]
        cw = jnp.concatenate(cols(cw_ref, c), axis=1)
        cb = jnp.concatenate(cols(cb_ref, c), axis=1)
        S = FFN_HALO
        ng = half // S
        u3 = u[r * half:r * half + half + 2 * S].reshape(ng + 2, S, 2 * FFN_CHUNK)
        sub = lax.broadcasted_iota(jnp.int32, (ng, S, 2 * FFN_CHUNK), 1)
        down = pltpu.roll(u3, 1, axis=1)
        up = pltpu.roll(u3, S - 1, axis=1)
        prev = jnp.where(sub == 0, down[:ng], down[1:ng + 1])
        nxt = jnp.where(sub == S - 1, up[2:], up[1:ng + 1])
        conv = (cw[0:1] * prev + cw[1:2] * u3[1:ng + 1] + cw[2:3] * nxt + cb).reshape(half, 2 * FFN_CHUNK)
        cg, cv = conv[:, :FFN_CHUNK], conv[:, FFN_CHUNK:]
        return ((cg / (1.0 + jnp.exp(-cg))) * cv).astype(bf16)

    def down_proj(act, c0, c1, r):
        part = _dot(act, wd_ref[c0 * FFN_CHUNK:c1 * FFN_CHUNK, :])
        out_rows = slice(r * half, (r + 1) * half)
        if c0 == 0:
            acc_scr[out_rows, :] = part
        else:
            acc_scr[out_rows, :] += part

    pairs = [tuple(range(c, min(c + 2, n_chunks))) for c in range(0, n_chunks, 2)]
    for c in pairs[0]:
        for rows in up_rows:
            up_proj(c, rows)
    pending = None
    for k, pair in enumerate(pairs):
        for r in range(2):
            if k + 1 < len(pairs):
                for c in pairs[k + 1]:
                    up_proj(c, up_rows[r])
            if pending is not None:
                down_proj(*pending)
            acts = [gated(c, r) for c in pair]
            pending = (acts[0] if len(acts) == 1 else jnp.concatenate(acts, axis=1), pair[0], pair[-1] + 1, r)
    down_proj(*pending)
    o_ref[...] = x_ref[...] + gate_ref[...] * acc_scr[...]


def ffn_weights(w_up, conv_w, conv_b, w_down):
    return w_up.astype(bf16), conv_w, conv_b.reshape(1, -1), w_down.astype(bf16)


def conv_ffn(x, modt, l, row, g, weights, tm):
    bsz, t, d = x.shape
    wup, cw, cb, wd = weights
    hb = tm // FFN_HALO
    last = t // FFN_HALO - 1

    def resident(a):
        return pl.BlockSpec(a.shape, lambda b, i: (0,) * a.ndim, pipeline_mode=pl.Buffered(1))

    return pl.pallas_call(
        functools.partial(_ffn_kernel, tm=tm),
        out_shape=jax.ShapeDtypeStruct((bsz, t, d), f32),
        grid=(bsz, t // tm),
        in_specs=[pl.BlockSpec((None, tm, d), lambda b, i: (b, i, 0)),
                  pl.BlockSpec((None, FFN_HALO, d), lambda b, i: (b, jnp.maximum(i * hb - 1, 0), 0)),
                  pl.BlockSpec((None, FFN_HALO, d), lambda b, i: (b, jnp.minimum((i + 1) * hb, last), 0)),
                  _mod_spec(l, 3, row, d), _mod_spec(l, 4, row, d), _mod_spec(l, 5, row, d),
                  pl.BlockSpec((1, d), lambda b, i: (0, 0)),
                  resident(wup), resident(cw), resident(cb), resident(wd)],
        out_specs=pl.BlockSpec((None, tm, d), lambda b, i: (b, i, 0)),
        scratch_shapes=[pltpu.VMEM((tm + 2 * FFN_HALO, d), bf16),
                        pltpu.VMEM((4, tm + 2 * FFN_HALO, 2 * FFN_CHUNK), f32),
                        pltpu.VMEM((tm, d), f32)],
        compiler_params=_cparams(("parallel", "parallel")),
        name="conv_ffn",
    )(x, x, x, modt, modt, modt, g.reshape(1, d), wup, cw, cb, wd)


def _rope_tables(n):
    t = jnp.arange(n)
    row = (t // GRID_W).astype(f32)
    colp = (t % GRID_W).astype(f32)
    half = HEAD_DIM // 2
    freq = ROPE_THETA ** (-jnp.arange(0, half, 2, dtype=f32) / half)
    ang_r = row[:, None] * freq[None, :]
    ang_c = colp[:, None] * freq[None, :]
    ang = jnp.concatenate([ang_r, ang_r, ang_c, ang_c] * 2, axis=-1)
    return jnp.cos(ang), jnp.sin(ang)


def _sink_column(sink, n_kv, tq):
    n_sub = sink.shape[0] // n_kv
    return jnp.repeat(sink.reshape(n_kv, n_sub), tq, axis=1)[..., None].astype(f32)


def kernel(x, c, ctx, c_ctx, ada_w, ada_b, norm_g, w_out, ffn_up, ffn_conv_w, ffn_conv_b, ffn_down, even_w_in,
           mlstm_gate_b, mlstm_head_g, swa_qk_g, swa_sink, odd_w_in, gqa_qk_g, na_qk_g, na_rpb):
    bsz, seq, d = x.shape
    n_ctx = ctx.shape[1]
    depth = ada_w.shape[0]
    a_w = A_HEADS * A_DIM
    cos, sin = _rope_tables(seq)
    ones_c, zeros_c = jnp.ones((n_ctx, LANES), f32), jnp.zeros((n_ctx, LANES), f32)

    cc = jnp.zeros((16, d), f32).at[:bsz].set(c).at[bsz].set(c_ctx)
    modt = ada_modulation(cc, ada_w, ada_b).reshape(depth, 16, 6, 1, d)

    x_lat, x_ctx = x, ctx
    tm_l, tm_c = 256, 256
    for l in range(depth):
        need_ctx = l < depth - 1
        w_o = w_out[l].astype(bf16)
        if l % 2 == 0:
            e = l // 2
            wi = even_w_in[e]
            sp = np.cumsum((0, a_w, a_w, a_w, a_w, 4 * A_HEADS, B_HEADS * HEAD_DIM, B_KV * HEAD_DIM, B_KV * HEAD_DIM))
            aq, ak, av, ao, ag, bq, bk, bv = (wi[:, sp[k]:sp[k + 1]] for k in range(8))
            w = jnp.concatenate([aq, ak, av, ao, ag, jnp.zeros((d, LANES - 4 * A_HEADS), f32), bq, bk, bv],
                                axis=1).astype(bf16)
            gate_col = 4 * a_w
            n_raw = gate_col + LANES
            gq, gk = swa_qk_g[e, 0], swa_qk_g[e, 1]
            cols = dict(n_raw=n_raw, q_cols=(n_raw,), kpair_cols=(n_raw + B_HEADS * HEAD_DIM,), kfull_cols=(),
                        vfull_cols=(), gains=(gq, gk))
            y_lat, qn_l, k2_l, v2_l = proj_prep(x_lat, modt, l, None, norm_g[l, 0], w, **cols, rope=(True, True),
                                                cos=cos, sin=sin, tm=tm_l)
            y_ctx, qn_c, k2_c, v2_c = proj_prep(x_ctx, modt, l, bsz, norm_g[l, 0], w, **cols, rope=(False, False),
                                                cos=ones_c, sin=zeros_c, tm=tm_c)
            a_ctx, a_lat = mlstm_mixer(y_ctx, y_lat, gate_col // LANES, mlstm_gate_b[e], mlstm_head_g[e])
            b_lat = window_attention(qn_l, k2_l, v2_l, k2_c, v2_c, _sink_column(swa_sink[e], B_KV, B_WIN))
            mix_l = (a_lat, b_lat)
            if need_ctx:
                b_ctx = gqa_attention(qn_c, [(k2_c, v2_c)], B_KV, 128, _sink_column(swa_sink[e], B_KV, 128))
                mix_c = (a_ctx, b_ctx)
        else:
            o = l // 2
            wi = odd_w_in[o]
            sp = np.cumsum((0, C_HEADS * HEAD_DIM, C_KV * HEAD_DIM, C_KV * HEAD_DIM) + (D_HEADS * HEAD_DIM,) * 3)
            cq, ck, cv, nq, nk, nv = (wi[:, sp[k]:sp[k + 1]] for k in range(6))
            w_ctx = jnp.concatenate([nk, nv, ck, cv], axis=1).astype(bf16)
            w = jnp.concatenate([cq.astype(bf16), nq.astype(bf16), w_ctx], axis=1)
            gq, gk = gqa_qk_g[o, 0], gqa_qk_g[o, 1]
            nq_g, nk_g = na_qk_g[o, 0], na_qk_g[o, 1]
            wide = 4 * LANES
            cqn, nqn, ck2, cv2, nkn, nvb = proj_prep(
                x_lat, modt, l, None, norm_g[l, 0], w, n_raw=0, q_cols=(0, wide), kpair_cols=(4 * wide,),
                kfull_cols=(2 * wide,), vfull_cols=(3 * wide,), gains=(gq, nq_g, gk, nk_g),
                rope=(True, False, True), cos=cos, sin=sin, tm=tm_l)
            if need_ctx:
                raise NotImplementedError("context outputs of an odd layer")
            ck2_c, cv2_c, nkn_c, nvb_c = proj_prep(
                x_ctx, modt, l, bsz, norm_g[l, 0], w_ctx, n_raw=0, q_cols=(), kpair_cols=(2 * wide,),
                kfull_cols=(0,), vfull_cols=(wide,), gains=(gk, nk_g), rope=(False,), cos=ones_c, sin=zeros_c,
                tm=tm_c)
            c_lat = gqa_attention(cqn, [(ck2_c, cv2_c), (ck2, cv2)], C_KV, 256)
            d_lat = neighbourhood_attention(nqn, nkn, nvb, nkn_c, nvb_c, na_bias_table(na_rpb[o], seq // GRID_W))
            mix_l = (c_lat, d_lat)

        ffn_w = ffn_weights(ffn_up[l], ffn_conv_w[l], ffn_conv_b[l], ffn_down[l])
        x_lat = out_proj(x_lat, mix_l[0], mix_l[1], w_o, modt, l, None, tm_l)
        x_lat = conv_ffn(x_lat, modt, l, None, norm_g[l, 1], ffn_w, tm=512)
        if need_ctx:
            x_ctx = out_proj(x_ctx, mix_c[0], mix_c[1], w_o, modt, l, bsz, tm_c)
            x_ctx = conv_ffn(x_ctx, modt, l, bsz, norm_g[l, 1], ffn_w, tm=256)
    return x_lat
```

```python
import functools

import jax
import jax.numpy as jnp
import numpy as np
from jax import lax
from jax.experimental import pallas as pl
from jax.experimental.pallas import tpu as pltpu

f32 = jnp.float32
bf16 = jnp.bfloat16

GRID_W = 64
HEAD_DIM = 64
LANES = 128
A_HEADS = 4
A_DIM = 128
B_HEADS = 8
B_KV = 2
B_WIN = 128
C_HEADS = 8
C_KV = 2
D_HEADS = 8
NA_ROWS = 8
NA_COLS = 16
NA_QROWS = 4
NA_KROWS = 12
ROPE_THETA = 10000.0
EPS = 1e-6
NEG = -1e30
LOG2E = 1.4426950408889634
VMEM_LIMIT = 56 * 1024 * 1024


def _cparams(sem):
    return pltpu.CompilerParams(dimension_semantics=sem, vmem_limit_bytes=VMEM_LIMIT)


def _dot(a, b):
    return jnp.dot(a, b, preferred_element_type=f32)


def _dot_nt(a, b):
    return lax.dot_general(a, b, (((1,), (1,)), ((), ())), preferred_element_type=f32)


def _modulated_norm(x, g, shift, scale):
    y = x * lax.rsqrt(jnp.mean(x * x, axis=-1, keepdims=True) + EPS)
    return (y * g) * (1.0 + scale) + shift


def _mod_spec(l, k, row, d):
    if row is None:
        return pl.BlockSpec((None, None, None, 1, d), lambda b, *_: (l, b, k, 0, 0))
    return pl.BlockSpec((None, None, None, 1, d), lambda b, *_: (l, row, k, 0, 0))


def _ada_kernel(c_ref, w_ref, b_ref, o_ref):
    c = c_ref[...]
    s = c / (1.0 + jnp.exp(-c))
    o_ref[...] = _dot(s.astype(bf16), w_ref[...].astype(bf16)) + b_ref[...]


def ada_modulation(cc, ada_w, ada_b):
    depth, d, n = ada_w.shape
    tn = 1536
    return pl.pallas_call(
        _ada_kernel,
        out_shape=jax.ShapeDtypeStruct((depth, cc.shape[0], n), f32),
        grid=(depth, n // tn),
        in_specs=[pl.BlockSpec(cc.shape, lambda l, j: (0, 0)),
                  pl.BlockSpec((None, d, tn), lambda l, j: (l, 0, j)),
                  pl.BlockSpec((None, 1, tn), lambda l, j: (l, 0, j))],
        out_specs=pl.BlockSpec((None, cc.shape[0], tn), lambda l, j: (l, 0, j)),
        compiler_params=_cparams(("parallel", "parallel")),
        name="ada",
    )(cc, ada_w, ada_b.reshape(depth, 1, n))


def _head_ms(a):
    ri = lax.broadcasted_iota(jnp.int32, (LANES, LANES), 0) // HEAD_DIM
    ci = lax.broadcasted_iota(jnp.int32, (LANES, LANES), 1) // HEAD_DIM
    bd = jnp.where(ri == ci, 1.0, 0.0).astype(bf16)
    ss = a * a
    hi = ss.astype(bf16)
    lo = (ss - hi.astype(f32)).astype(bf16)
    return (_dot(hi, bd) + _dot(lo, bd)) * (1.0 / HEAD_DIM)


def _head_norm(a, g):
    return (a * lax.rsqrt(_head_ms(a) + EPS)) * g


def _rope(a, cos, sin):
    lane = lax.broadcasted_iota(jnp.int32, a.shape, 1)
    quarter = HEAD_DIM // 4
    rot = jnp.where(lane % (2 * quarter) < quarter,
                    -pltpu.roll(a, LANES - quarter, axis=1), pltpu.roll(a, quarter, axis=1))
    return a * cos + rot * sin


def _dup_halves(a):
    lane = lax.broadcasted_iota(jnp.int32, a.shape, 1)
    sw = pltpu.roll(a, HEAD_DIM, axis=1)
    lo = lane < HEAD_DIM
    return jnp.where(lo, a, sw), jnp.where(lo, sw, a)


def _prep_kernel(*refs, n_q, n_kpair, n_kfull, n_vfull, rope, scale):
    it = iter(refs)
    q_refs = [next(it) for _ in range(n_q)]
    kp_refs = [next(it) for _ in range(n_kpair)]
    kf_refs = [next(it) for _ in range(n_kfull)]
    vf_refs = [next(it) for _ in range(n_vfull)]
    g_refs = [next(it) for _ in range(n_q + n_kpair + n_kfull)]
    cos_ref, sin_ref = next(it), next(it)
    qo_refs = [next(it) for _ in range(n_q)]
    kpo_refs = [(next(it), next(it)) for _ in range(n_kpair)]
    kfo_refs = [next(it) for _ in range(n_kfull)]
    vfo_refs = [next(it) for _ in range(n_vfull)]
    gi = iter(g_refs)
    for qi, (q_ref, qo_ref) in enumerate(zip(q_refs, qo_refs)):
        g = next(gi)[...]
        for p in range(q_ref.shape[-1] // LANES):
            a = _head_norm(q_ref[:, p * LANES:(p + 1) * LANES], g)
            if rope[qi]:
                a = _rope(a, cos_ref[...], sin_ref[...])
            qo_ref[:, p * LANES:(p + 1) * LANES] = (a * scale).astype(bf16)
    for ki, (kp_ref, (ko_ref, vo_ref)) in enumerate(zip(kp_refs, kpo_refs)):
        g = next(gi)[...]
        k = _head_norm(kp_ref[:, :LANES], g)
        if rope[n_q + ki]:
            k = _rope(k, cos_ref[...], sin_ref[...])
        k0, k1 = _dup_halves(k)
        ko_ref[0] = k0.astype(bf16)
        ko_ref[1] = k1.astype(bf16)
        v0, v1 = _dup_halves(kp_ref[:, LANES:])
        vo_ref[0] = v0.astype(bf16)
        vo_ref[1] = v1.astype(bf16)
    for kf_ref, kfo_ref in zip(kf_refs, kfo_refs):
        g = next(gi)[...]
        for p in range(kf_ref.shape[-1] // LANES):
            kfo_ref[:, p * LANES:(p + 1) * LANES] = _head_norm(kf_ref[:, p * LANES:(p + 1) * LANES], g).astype(bf16)
    for vf_ref, vfo_ref in zip(vf_refs, vfo_refs):
        vfo_ref[...] = vf_ref[...].astype(bf16)


def _proj_prep_kernel(x_ref, sh_ref, sc_ref, g_ref, w_ref, *rest, n_raw, q_cols, kpair_cols, kfull_cols, vfull_cols,
                      rope):
    n_gain = len(q_cols) + len(kpair_cols) + len(kfull_cols)
    side = rest[:n_gain + 2]
    outs = rest[n_gain + 2:-1]
    y_scr = rest[-1]
    h = _modulated_norm(x_ref[...], g_ref[...], sh_ref[...], sc_ref[...])
    y_scr[...] = _dot(h.astype(bf16), w_ref[...])
    if n_raw:
        outs[0][...] = y_scr[:, :n_raw]
        outs = outs[1:]
    wide, pair = 4 * LANES, 2 * LANES
    views = ([y_scr.at[:, c:c + wide] for c in q_cols] + [y_scr.at[:, c:c + pair] for c in kpair_cols]
             + [y_scr.at[:, c:c + wide] for c in kfull_cols + vfull_cols])
    _prep_kernel(*views, *side, *outs, n_q=len(q_cols), n_kpair=len(kpair_cols), n_kfull=len(kfull_cols),
                 n_vfull=len(vfull_cols), rope=rope, scale=HEAD_DIM ** -0.5 * LOG2E)


def proj_prep(x, modt, l, row, norm_g, w, *, n_raw, q_cols, kpair_cols, kfull_cols, vfull_cols, gains, rope, cos, sin,
              tm):
    bsz, t, d = x.shape
    n = w.shape[1]
    wide = 4 * LANES
    in_specs = [pl.BlockSpec((None, tm, d), lambda b, i: (b, i, 0)),
                _mod_spec(l, 0, row, d), _mod_spec(l, 1, row, d),
                pl.BlockSpec((1, d), lambda b, i: (0, 0)),
                pl.BlockSpec((d, n), lambda b, i: (0, 0))]
    args = [x, modt, modt, norm_g.reshape(1, d), w]
    for g in gains:
        in_specs.append(pl.BlockSpec((1, LANES), lambda b, i: (0, 0)))
        args.append(jnp.tile(g, 2).reshape(1, LANES))
    for tbl in (cos, sin):
        in_specs.append(pl.BlockSpec((tm, LANES), lambda b, i: (i, 0)))
        args.append(tbl)
    out_shape, out_specs = [], []
    if n_raw:
        out_shape.append(jax.ShapeDtypeStruct((bsz, t, n_raw), f32))
        out_specs.append(pl.BlockSpec((None, tm, n_raw), lambda b, i: (b, i, 0)))
    for _ in q_cols:
        out_shape.append(jax.ShapeDtypeStruct((bsz, t, wide), bf16))
        out_specs.append(pl.BlockSpec((None, tm, wide), lambda b, i: (b, i, 0)))
    for _ in kpair_cols:
        for _ in range(2):
            out_shape.append(jax.ShapeDtypeStruct((bsz, 2, t, LANES), bf16))
            out_specs.append(pl.BlockSpec((None, 2, tm, LANES), lambda b, i: (b, 0, i, 0)))
    for _ in kfull_cols + vfull_cols:
        out_shape.append(jax.ShapeDtypeStruct((bsz, t, wide), bf16))
        out_specs.append(pl.BlockSpec((None, tm, wide), lambda b, i: (b, i, 0)))
    kern = functools.partial(_proj_prep_kernel, n_raw=n_raw, q_cols=q_cols, kpair_cols=kpair_cols,
                             kfull_cols=kfull_cols, vfull_cols=vfull_cols, rope=rope)
    return pl.pallas_call(
        kern, out_shape=out_shape, grid=(bsz, t // tm), in_specs=in_specs, out_specs=out_specs,
        scratch_shapes=[pltpu.VMEM((tm, n), f32)],
        compiler_params=_cparams(("parallel", "parallel")), name="proj_prep",
    )(*args)


def _stack_heads(q_ref, n_sub):
    parts = []
    for g in range(n_sub):
        blk = q_ref[:, (g // 2) * LANES:(g // 2 + 1) * LANES]
        lane = lax.broadcasted_iota(jnp.int32, blk.shape, 1)
        keep = (lane < HEAD_DIM) if g % 2 == 0 else (lane >= HEAD_DIM)
        parts.append(jnp.where(keep, blk, jnp.zeros_like(blk)))
    return jnp.concatenate(parts, axis=0)


def _scores(q, segs):
    scores = []
    for k, _, bias, mask in segs:
        s = _dot_nt(q, k)
        if bias is not None:
            s = s + bias
        if mask is not None:
            s = jnp.where(mask, s, NEG)
        scores.append(s)
    return scores


def _softmax_attend(q, segs, sink, scores=None):
    return _softmax_attend_chains([(q, segs, sink, scores)])[0]


def _softmax_attend_chains(chains):
    scores = [_scores(q, segs) if sc is None else sc for q, segs, _, sc in chains]
    maxes = [[s.max(axis=-1, keepdims=True) for s in sc] for sc in scores]
    m = []
    for (_, _, sink, _), mx in zip(chains, maxes):
        mi = functools.reduce(jnp.maximum, mx)
        m.append(mi if sink is None else jnp.maximum(mi, sink))
    p = [[jnp.exp2(s - m[i]) for s in sc] for i, sc in enumerate(scores)]
    outs = []
    for i, (_, segs, sink, _) in enumerate(chains):
        den = functools.reduce(jnp.add, [pj.sum(axis=-1, keepdims=True) for pj in p[i]])
        if sink is not None:
            den = den + jnp.exp2(sink - m[i])
        acc = functools.reduce(jnp.add, [_dot(pj.astype(bf16), seg[1]) for pj, seg in zip(p[i], segs)])
        outs.append(acc / den)
    return outs


def _unstack_heads(o, o_ref, n_sub, tq):
    lane = lax.broadcasted_iota(jnp.int32, (tq, LANES), 1)
    for p in range(n_sub // 2):
        even = o[(2 * p) * tq:(2 * p + 1) * tq]
        odd = o[(2 * p + 1) * tq:(2 * p + 2) * tq]
        o_ref[:, p * LANES:(p + 1) * LANES] = jnp.where(lane < HEAD_DIM, even, odd).astype(o_ref.dtype)


def _window_attn_kernel(q_ref, kc_ref, vc_ref, kp_ref, k0_ref, kn_ref, vp_ref, v0_ref, vn_ref, sink_ref, o_ref, *, tq):
    i = pl.program_id(2)
    nb = pl.num_programs(2)
    n_sub = B_HEADS // B_KV
    n_ctx = kc_ref.shape[0]
    rows, n_keys = n_sub * tq, n_ctx + 3 * tq
    qpos = lax.broadcasted_iota(jnp.int32, (rows, n_keys), 0) % tq
    kpos = lax.broadcasted_iota(jnp.int32, (rows, n_keys), 1) - n_ctx
    in_prev, in_next = (kpos >= 0) & (kpos < tq), kpos >= 2 * tq
    bad = (in_prev & (kpos < qpos)) | (in_next & (kpos - 2 * tq > qpos))
    k_blocks = [kp_ref[...], k0_ref[:tq], k0_ref[tq:], kn_ref[...]]
    v_blocks = [vp_ref[...], v0_ref[:tq], v0_ref[tq:], vn_ref[...]]
    chains = []
    for j, missing in enumerate((in_prev & (i == 0), in_next & (i == nb - 1))):
        k_all = jnp.concatenate([kc_ref[...]] + k_blocks[j:j + 3], axis=0)
        v_all = jnp.concatenate([vc_ref[...]] + v_blocks[j:j + 3], axis=0)
        q = _stack_heads(q_ref.at[j * tq:(j + 1) * tq], n_sub)
        chains.append((q, [(k_all, v_all, None, ~(bad | missing))], sink_ref[...], None))
    for j, o in enumerate(_softmax_attend_chains(chains)):
        _unstack_heads(o, o_ref.at[j * tq:(j + 1) * tq], n_sub, tq)


def window_attention(qn, k2, v2, kc2, vc2, sink_col):
    bsz, s, _ = qn.shape
    c = kc2.shape[2]
    tq = B_WIN
    nb = s // (2 * tq)
    n_half = s // tq
    n_sub = B_HEADS // B_KV
    wq = n_sub * HEAD_DIM
    half_spec = lambda fn: pl.BlockSpec((None, None, tq, LANES), fn)
    prev = lambda b, h, i: (b, h, jnp.maximum(2 * i - 1, 0), 0)
    nxt = lambda b, h, i: (b, h, jnp.minimum(2 * i + 2, n_half - 1), 0)
    cur_spec = pl.BlockSpec((None, None, 2 * tq, LANES), lambda b, h, i: (b, h, i, 0))
    ctx_spec = pl.BlockSpec((None, None, c, LANES), lambda b, h, i: (b, h, 0, 0))
    return pl.pallas_call(
        functools.partial(_window_attn_kernel, tq=tq),
        out_shape=jax.ShapeDtypeStruct((bsz, s, B_HEADS * HEAD_DIM), bf16),
        grid=(bsz, B_KV, nb),
        in_specs=[pl.BlockSpec((None, 2 * tq, wq), lambda b, h, i: (b, i, h)),
                  ctx_spec, ctx_spec,
                  half_spec(prev), cur_spec, half_spec(nxt),
                  half_spec(prev), cur_spec, half_spec(nxt),
                  pl.BlockSpec((None, n_sub * tq, 1), lambda b, h, i: (h, 0, 0))],
        out_specs=pl.BlockSpec((None, 2 * tq, wq), lambda b, h, i: (b, i, h)),
        compiler_params=_cparams(("parallel", "parallel", "parallel")),
        name="window_attn",
    )(qn, kc2, vc2, k2, k2, k2, v2, v2, v2, sink_col)


def _seg_attn_kernel(*refs, n_seg, n_sub, tq, has_sink):
    q_ref = refs[0]
    kv = refs[1:1 + 2 * n_seg]
    sink = refs[1 + 2 * n_seg][...] if has_sink else None
    o_ref = refs[-1]
    segs = [(kv[2 * j][...], kv[2 * j + 1][...], None, None) for j in range(n_seg)]
    n_pair = n_sub // 2
    lanes = [slice(p * LANES, (p + 1) * LANES) for p in range(n_pair)]
    chains = [(_stack_heads(q_ref.at[:, lanes[p]], 2), segs,
               None if sink is None else sink[2 * p * tq:(2 * p + 2) * tq], None) for p in range(n_pair)]
    for p, o in enumerate(_softmax_attend_chains(chains)):
        _unstack_heads(o, o_ref.at[:, lanes[p]], 2, tq)


def gqa_attention(qn, kv_segs, n_kv, tq, sink_col=None):
    bsz, s, width = qn.shape
    n_sub = width // HEAD_DIM // n_kv
    wq = n_sub * HEAD_DIM
    in_specs = [pl.BlockSpec((None, tq, wq), lambda b, h, i: (b, i, h))]
    args = [qn]
    for k2, v2 in kv_segs:
        n = k2.shape[2]
        spec = pl.BlockSpec((None, None, n, LANES), lambda b, h, i: (b, h, 0, 0))
        in_specs += [spec, spec]
        args += [k2, v2]
    if sink_col is not None:
        in_specs.append(pl.BlockSpec((None, n_sub * tq, 1), lambda b, h, i: (h, 0, 0)))
        args.append(sink_col)
    return pl.pallas_call(
        functools.partial(_seg_attn_kernel, n_seg=len(kv_segs), n_sub=n_sub, tq=tq, has_sink=sink_col is not None),
        out_shape=jax.ShapeDtypeStruct((bsz, s, width), bf16),
        grid=(bsz, n_kv, s // tq),
        in_specs=in_specs,
        out_specs=pl.BlockSpec((None, tq, wq), lambda b, h, i: (b, i, h)),
        compiler_params=_cparams(("parallel", "parallel", "parallel")),
        name="gqa_attn",
    )(*args)


def _na_attn_kernel(q_ref, kc_ref, vc_ref, k_ref, v_ref, bias_ref, o_ref, *, tq, n_grp):
    r = pl.program_id(1)
    rows_total = k_ref.shape[0] // GRID_W
    start = jnp.clip(r * NA_QROWS - NA_ROWS // 2, 0, rows_total - NA_KROWS)
    off = pl.multiple_of(start * GRID_W, GRID_W)
    nk = NA_KROWS * GRID_W
    blk = q_ref[...]
    lane = lax.broadcasted_iota(jnp.int32, blk.shape, 1)
    k_nb, v_nb = k_ref[pl.ds(off, nk), :], v_ref[pl.ds(off, nk), :]
    chains = []
    for half in range(2):
        keep = (lane < HEAD_DIM) if half == 0 else (lane >= HEAD_DIM)
        segs = [(kc_ref[...], vc_ref[...], None, None),
                (k_nb, v_nb, bias_ref[half * tq:(half + 1) * tq, :], None)]
        chains.append((jnp.where(keep, blk, jnp.zeros_like(blk)), segs, None, None))
    outs = _softmax_attend_chains(chains)
    o_ref[...] = jnp.where(lane < HEAD_DIM, outs[0], outs[1]).astype(o_ref.dtype)


def neighbourhood_attention(qn, kn, vb, kcn, vcb, bias_tbl):
    bsz, s, width = qn.shape
    c = kcn.shape[1]
    tq = NA_QROWS * GRID_W
    n_grp = s // tq
    n_pair = width // LANES
    variant = lambda r: jnp.where(r == 0, 0, jnp.where(r == n_grp - 1, 2, 1))
    return pl.pallas_call(
        functools.partial(_na_attn_kernel, tq=tq, n_grp=n_grp),
        out_shape=jax.ShapeDtypeStruct((bsz, s, width), bf16),
        grid=(n_pair, n_grp, bsz),
        in_specs=[pl.BlockSpec((None, tq, LANES), lambda p, r, b: (b, r, p)),
                  pl.BlockSpec((None, c, LANES), lambda p, r, b: (b, 0, p)),
                  pl.BlockSpec((None, c, LANES), lambda p, r, b: (b, 0, p)),
                  pl.BlockSpec((None, s, LANES), lambda p, r, b: (b, 0, p)),
                  pl.BlockSpec((None, s, LANES), lambda p, r, b: (b, 0, p)),
                  pl.BlockSpec((None, None, 2 * tq, NA_KROWS * GRID_W), lambda p, r, b: (variant(r), p, 0, 0))],
        out_specs=pl.BlockSpec((None, tq, LANES), lambda p, r, b: (b, r, p)),
        compiler_params=_cparams(("parallel", "parallel", "parallel")),
        name="na_attn",
    )(qn, kcn, vcb, kn, vb, bias_tbl)


def _na_bias_kernel(rp_ref, o_ref, *, rows_total):
    n_grp = rows_total // NA_QROWS
    qc = lax.broadcasted_iota(jnp.int32, (GRID_W, LANES), 0)
    lane = lax.broadcasted_iota(jnp.int32, (GRID_W, LANES), 1)
    kc = lane % GRID_W
    cs = jnp.clip(qc - NA_COLS // 2, 0, GRID_W - NA_COLS)
    col_ok = (kc >= cs) & (kc < cs + NA_COLS)
    left = lane < GRID_W
    neg = jnp.full((GRID_W, LANES), NEG, f32)
    for v, r0 in enumerate((0, NA_QROWS, (n_grp - 1) * NA_QROWS)):
        start = min(max(r0 - NA_ROWS // 2, 0), rows_total - NA_KROWS)
        for hh in range(2):
            for i in range(NA_QROWS):
                qr = r0 + i
                ws = min(max(qr - NA_ROWS // 2, 0), rows_total - NA_ROWS)
                for jj in range(NA_KROWS // 2):
                    kra, krb = start + 2 * jj, start + 2 * jj + 1
                    ok_a, ok_b = ws <= kra < ws + NA_ROWS, ws <= krb < ws + NA_ROWS
                    r_lo = hh * NA_QROWS * GRID_W + i * GRID_W
                    dst = (v, slice(r_lo, r_lo + GRID_W), slice(jj * LANES, (jj + 1) * LANES))
                    if not (ok_a or ok_b):
                        o_ref[dst] = neg
                        continue
                    dra = min(max(kra - qr + NA_ROWS - 1, 0), 2 * NA_ROWS - 2)
                    drb = min(max(krb - qr + NA_ROWS - 1, 0), 2 * NA_ROWS - 2)
                    row = jnp.where(left[:1], rp_ref[hh, dra:dra + 1, :], rp_ref[hh, drb:drb + 1, :])
                    toe = pltpu.roll(jnp.broadcast_to(row, (GRID_W, LANES)), LANES - (NA_COLS - 1), axis=1,
                                     stride=1, stride_axis=0)
                    ok = col_ok if (ok_a and ok_b) else (col_ok & left if ok_a else col_ok & ~left)
                    o_ref[dst] = jnp.where(ok, toe * LOG2E, neg)


def na_bias_table(rpb, rows_total):
    h, ndr, ndc = rpb.shape
    rp = jnp.zeros((h, 16, LANES), f32).at[:, :ndr, :ndc].set(rpb).at[:, :ndr, GRID_W:GRID_W + ndc].set(rpb)
    return pl.pallas_call(
        functools.partial(_na_bias_kernel, rows_total=rows_total),
        out_shape=jax.ShapeDtypeStruct((3, h // 2, 2 * NA_QROWS * GRID_W, NA_KROWS * GRID_W), f32),
        grid=(h // 2,),
        in_specs=[pl.BlockSpec((2, 16, LANES), lambda p: (p, 0, 0))],
        out_specs=pl.BlockSpec((3, None, 2 * NA_QROWS * GRID_W, NA_KROWS * GRID_W), lambda p: (0, p, 0, 0)),
        compiler_params=_cparams(("parallel",)),
        name="na_bias",
    )(rp)


def _log_sigmoid(x):
    return jnp.minimum(x, 0.0) - jnp.log1p(jnp.exp(-jnp.abs(x)))


MLSTM_L = 128
MLSTM_HP = 4


def _mlstm_chunks(cx_ref, m_ref, eye, chains):
    every = lambda fn: [fn(i, ch) for i, ch in enumerate(chains)]
    cx = every(lambda i, ch: cx_ref[ch["slot"]])
    m = every(lambda i, ch: m_ref[ch["slot"]])
    b_col = every(lambda i, ch: jnp.sum(jnp.where(ch["tri"], ch["lf"], 0.0), axis=1, keepdims=True))
    b_row = every(lambda i, ch: jnp.sum(jnp.where(eye, b_col[i], 0.0), axis=0, keepdims=True))
    b_last = every(lambda i, ch: jnp.sum(ch["lf"], axis=1, keepdims=True))
    dmat = every(lambda i, ch: jnp.where(ch["tri"], b_col[i] - b_row[i] + ch["li"], -jnp.inf))
    inter = every(lambda i, ch: b_col[i] + m[i])
    m_t = every(lambda i, ch: jnp.maximum(inter[i], jnp.max(dmat[i], axis=1, keepdims=True)))
    w = every(lambda i, ch: jnp.exp(dmat[i] - m_t[i]))
    a = every(lambda i, ch: jnp.exp(inter[i] - m_t[i]))
    vx = every(lambda i, ch: jnp.concatenate([ch["v"], jnp.ones_like(ch["v"])], axis=1))
    s = every(lambda i, ch: _dot(ch["q"], ch["k_t"].astype(bf16)) * w[i])
    qc = every(lambda i, ch: _dot(ch["q"], cx[i].astype(bf16)))
    nd = every(lambda i, ch: a[i] * qc[i] + _dot(s[i].astype(bf16), vx[i]))
    h = every(lambda i, ch: nd[i][:, :A_DIM] / jnp.maximum(jnp.abs(nd[i][:, A_DIM:]), jnp.exp(-m_t[i])))
    g_row = every(lambda i, ch: b_last[i] - b_row[i] + ch["li"])
    m_new = every(lambda i, ch: jnp.maximum(b_last[i] + m[i], jnp.max(g_row[i], axis=1, keepdims=True)))
    decay = every(lambda i, ch: jnp.exp(b_last[i] + m[i] - m_new[i]))
    wk_t = every(lambda i, ch: (ch["k_t"] * jnp.exp(g_row[i] - m_new[i])).astype(bf16))
    for i, ch in enumerate(chains):
        cx_ref[ch["slot"]] = decay[i] * cx[i] + _dot(wk_t[i], vx[i])
        m_ref[ch["slot"]] = m_new[i]
    return h


def _mlstm_kernel(gb_ref, qc_ref, kc_ref, vc_ref, oc_ref, gc_ref, ql_ref, kl_ref, vl_ref, ol_ref, gl_ref, hg_ref,
                  outc_ref, outl_ref, cx_scr, m_scr, g_scr, hc_scr, hl_scr):
    hg = pl.program_id(1)
    L = MLSTM_L
    cx_scr[...] = jnp.zeros_like(cx_scr)
    m_scr[...] = jnp.zeros_like(m_scr)
    kscale = A_DIM ** -0.5
    t_idx = lax.broadcasted_iota(jnp.int32, (L, L), 0)
    s_idx = lax.broadcasted_iota(jnp.int32, (L, L), 1)
    eye = s_idx == t_idx
    tri = (s_idx <= t_idx, s_idx >= t_idx)
    n_gates = 4 * A_HEADS

    def run(q_ref, k_ref, v_ref, g_ref, h_ref):
        nc = q_ref.shape[0] // L

        def body(j, accumulate):
            chains = []
            for d in range(2):
                cj = j if d == 0 else nc - 1 - j
                rows = pl.ds(pl.multiple_of(cj * L, L), L)
                g_scr[d] = g_ref[rows, :].T[:n_gates]
                for hh in range(MLSTM_HP):
                    hd = hg * MLSTM_HP + hh
                    ig, fg = (2 * d) * A_HEADS + hd, (2 * d + 1) * A_HEADS + hd
                    cols = slice(hh * A_DIM, (hh + 1) * A_DIM)
                    chains.append(dict(
                        slot=2 * hh + d, tri=tri[d], rows=rows, cols=cols,
                        li=g_scr[d, pl.ds(ig, 1), :] + gb_ref[ig],
                        lf=_log_sigmoid(g_scr[d, pl.ds(fg, 1), :] + gb_ref[fg]),
                        q=q_ref[rows, cols].astype(bf16), v=v_ref[rows, cols].astype(bf16),
                        k_t=(k_ref[rows, cols] * kscale).T))
            for ch, h in zip(chains, _mlstm_chunks(cx_scr, m_scr, eye, chains)):
                if accumulate:
                    h_ref[ch["rows"], ch["cols"]] += h
                else:
                    h_ref[ch["rows"], ch["cols"]] = h

        lax.fori_loop(0, nc // 2, lambda j, c: (body(j, False), c)[1], 0)
        lax.fori_loop(nc // 2, nc, lambda j, c: (body(j, True), c)[1], 0)

    run(qc_ref, kc_ref, vc_ref, gc_ref, hc_scr)
    run(ql_ref, kl_ref, vl_ref, gl_ref, hl_scr)
    for o_ref, h_ref, out_ref in ((oc_ref, hc_scr, outc_ref), (ol_ref, hl_scr, outl_ref)):
        for hh in range(MLSTM_HP):
            cols = slice(hh * A_DIM, (hh + 1) * A_DIM)
            h = h_ref[:, cols]
            hn = (h * lax.rsqrt(jnp.mean(h * h, axis=-1, keepdims=True) + EPS)) * hg_ref[:, cols]
            o = o_ref[:, cols]
            out_ref[:, cols] = (hn / (1.0 + jnp.exp(-o))).astype(out_ref.dtype)


def mlstm_mixer(y_ctx, y_lat, gate_block, gate_b, head_g):
    bsz, tc, _ = y_ctx.shape
    tl = y_lat.shape[1]
    wide = MLSTM_HP * A_DIM
    n_grp = A_HEADS // MLSTM_HP

    def col(t, base):
        return pl.BlockSpec((None, t, wide), functools.partial(lambda b, h, base: (b, 0, base + h), base=base))

    def gate_spec(t):
        return pl.BlockSpec((None, t, LANES), lambda b, h: (b, 0, gate_block))

    def out_spec(t):
        return pl.BlockSpec((None, t, wide), lambda b, h: (b, 0, h))

    return pl.pallas_call(
        _mlstm_kernel,
        out_shape=[jax.ShapeDtypeStruct((bsz, tc, A_HEADS * A_DIM), bf16),
                   jax.ShapeDtypeStruct((bsz, tl, A_HEADS * A_DIM), bf16)],
        grid=(bsz, n_grp),
        in_specs=[pl.BlockSpec(memory_space=pltpu.SMEM)]
        + [col(tc, n_grp * j) for j in range(4)] + [gate_spec(tc)]
        + [col(tl, n_grp * j) for j in range(4)] + [gate_spec(tl)]
        + [pl.BlockSpec((1, wide), lambda b, h: (0, h))],
        out_specs=[out_spec(tc), out_spec(tl)],
        scratch_shapes=[pltpu.VMEM((2 * MLSTM_HP, A_DIM, 2 * A_DIM), f32), pltpu.VMEM((2 * MLSTM_HP, 1, 1), f32),
                        pltpu.VMEM((2, 4 * A_HEADS, LANES), f32),
                        pltpu.VMEM((tc, wide), f32), pltpu.VMEM((tl, wide), f32)],
        compiler_params=_cparams(("parallel", "parallel")),
        name="mlstm",
    )(gate_b, y_ctx, y_ctx, y_ctx, y_ctx, y_ctx, y_lat, y_lat, y_lat, y_lat, y_lat,
      head_g.reshape(1, A_HEADS * A_DIM))


def _outproj_kernel(x_ref, a1_ref, a2_ref, w1_ref, w2_ref, gate_ref, o_ref):
    y = _dot(a1_ref[...], w1_ref[...]) + _dot(a2_ref[...], w2_ref[...])
    o_ref[...] = x_ref[...] + gate_ref[...] * y


def out_proj(x, a1, a2, w, modt, l, row, tm):
    bsz, t, d = x.shape
    k1, k2 = a1.shape[-1], a2.shape[-1]
    return pl.pallas_call(
        _outproj_kernel,
        out_shape=jax.ShapeDtypeStruct((bsz, t, d), f32),
        grid=(bsz, t // tm),
        in_specs=[pl.BlockSpec((None, tm, d), lambda b, i: (b, i, 0)),
                  pl.BlockSpec((None, tm, k1), lambda b, i: (b, i, 0)),
                  pl.BlockSpec((None, tm, k2), lambda b, i: (b, i, 0)),
                  pl.BlockSpec((k1, d), lambda b, i: (0, 0)),
                  pl.BlockSpec((k2, d), lambda b, i: (0, 0)),
                  _mod_spec(l, 2, row, d)],
        out_specs=pl.BlockSpec((None, tm, d), lambda b, i: (b, i, 0)),
        compiler_params=_cparams(("parallel", "parallel")),
        name="out_proj",
    )(x, a1, a2, w[:k1], w[k1:], modt)


FFN_HALO = 8


FFN_CHUNK = 256


def _ffn_kernel(x_ref, xp_ref, xn_ref, sh_ref, sc_ref, gate_ref, g_ref, wup_ref, cw_ref, cb_ref, wd_ref,
                o_ref, h_scr, u_scr, acc_scr, *, tm):
    i = pl.program_id(1)
    g, sh, sc = g_ref[...], sh_ref[...], sc_ref[...]
    hp = _modulated_norm(xp_ref[...], g, sh, sc)
    hn = _modulated_norm(xn_ref[...], g, sh, sc)
    h_scr[:FFN_HALO] = jnp.where(i > 0, hp, 0.0).astype(bf16)
    h_scr[FFN_HALO:FFN_HALO + tm] = _modulated_norm(x_ref[...], g, sh, sc).astype(bf16)
    h_scr[FFN_HALO + tm:] = jnp.where(i < pl.num_programs(1) - 1, hn, 0.0).astype(bf16)
    dff = wd_ref.shape[0]
    n_chunks = dff // FFN_CHUNK

    def cols(ref, c):
        lo = c * FFN_CHUNK
        return ref[:, lo:lo + FFN_CHUNK], ref[:, dff + lo:dff + lo + FFN_CHUNK]

    half = tm // 2
    up_rows = (slice(0, half + 2 * FFN_HALO), slice(half + 2 * FFN_HALO, tm + 2 * FFN_HALO))

    n_buf = u_scr.shape[0]

    def up_proj(c, rows):
        wg, wv = cols(wup_ref, c)
        u_scr[c % n_buf, rows, :FFN_CHUNK] = _dot(h_scr[rows, :], wg)
        u_scr[c % n_buf, rows, FFN_CHUNK:] = _dot(h_scr[rows, :], wv)

    def gated(c, r):
        u = u_scr.at[c % n_buf]
        cw = jnp.concatenate(cols(cw_ref, c), axis=1)
        cb = jnp.concatenate(cols(cb_ref, c), axis=1)
        S = FFN_HALO
        ng = half // S
        u3 = u[r * half:r * half + half + 2 * S].reshape(ng + 2, S, 2 * FFN_CHUNK)
        sub = lax.broadcasted_iota(jnp.int32, (ng, S, 2 * FFN_CHUNK), 1)
        down = pltpu.roll(u3, 1, axis=1)
        up = pltpu.roll(u3, S - 1, axis=1)
        prev = jnp.where(sub == 0, down[:ng], down[1:ng + 1])
        nxt = jnp.where(sub == S - 1, up[2:], up[1:ng + 1])
        conv = (cw[0:1] * prev + cw[1:2] * u3[1:ng + 1] + cw[2:3] * nxt + cb).reshape(half, 2 * FFN_CHUNK)
        cg, cv = conv[:, :FFN_CHUNK], conv[:, FFN_CHUNK:]
        return ((cg / (1.0 + jnp.exp(-cg))) * cv).astype(bf16)

    def down_proj(act, c0, c1, r):
        part = _dot(act, wd_ref[c0 * FFN_CHUNK:c1 * FFN_CHUNK, :])
        out_rows = slice(r * half, (r + 1) * half)
        if c0 == 0:
            acc_scr[out_rows, :] = part
        else:
            acc_scr[out_rows, :] += part

    pairs = [tuple(range(c, min(c + 2, n_chunks))) for c in range(0, n_chunks, 2)]
    for c in pairs[0]:
        for rows in up_rows:
            up_proj(c, rows)
    pending = None
    for k, pair in enumerate(pairs):
        for r in range(2):
            if k + 1 < len(pairs):
                for c in pairs[k + 1]:
                    up_proj(c, up_rows[r])
            if pending is not None:
                down_proj(*pending)
            acts = [gated(c, r) for c in pair]
            pending = (acts[0] if len(acts) == 1 else jnp.concatenate(acts, axis=1), pair[0], pair[-1] + 1, r)
    down_proj(*pending)
    o_ref[...] = x_ref[...] + gate_ref[...] * acc_scr[...]


def ffn_weights(w_up, conv_w, conv_b, w_down):
    return w_up.astype(bf16), conv_w, conv_b.reshape(1, -1), w_down.astype(bf16)


def conv_ffn(x, modt, l, row, g, weights, tm):
    bsz, t, d = x.shape
    wup, cw, cb, wd = weights
    hb = tm // FFN_HALO
    last = t // FFN_HALO - 1

    def resident(a):
        return pl.BlockSpec(a.shape, lambda b, i: (0,) * a.ndim, pipeline_mode=pl.Buffered(1))

    return pl.pallas_call(
        functools.partial(_ffn_kernel, tm=tm),
        out_shape=jax.ShapeDtypeStruct((bsz, t, d), f32),
        grid=(bsz, t // tm),
        in_specs=[pl.BlockSpec((None, tm, d), lambda b, i: (b, i, 0)),
                  pl.BlockSpec((None, FFN_HALO, d), lambda b, i: (b, jnp.maximum(i * hb - 1, 0), 0)),
                  pl.BlockSpec((None, FFN_HALO, d), lambda b, i: (b, jnp.minimum((i + 1) * hb, last), 0)),
                  _mod_spec(l, 3, row, d), _mod_spec(l, 4, row, d), _mod_spec(l, 5, row, d),
                  pl.BlockSpec((1, d), lambda b, i: (0, 0)),
                  resident(wup), resident(cw), resident(cb), resident(wd)],
        out_specs=pl.BlockSpec((None, tm, d), lambda b, i: (b, i, 0)),
        scratch_shapes=[pltpu.VMEM((tm + 2 * FFN_HALO, d), bf16),
                        pltpu.VMEM((4, tm + 2 * FFN_HALO, 2 * FFN_CHUNK), f32),
                        pltpu.VMEM((tm, d), f32)],
        compiler_params=_cparams(("parallel", "parallel")),
        name="conv_ffn",
    )(x, x, x, modt, modt, modt, g.reshape(1, d), wup, cw, cb, wd)


def _rope_tables(n):
    t = jnp.arange(n)
    row = (t // GRID_W).astype(f32)
    colp = (t % GRID_W).astype(f32)
    half = HEAD_DIM // 2
    freq = ROPE_THETA ** (-jnp.arange(0, half, 2, dtype=f32) / half)
    ang_r = row[:, None] * freq[None, :]
    ang_c = colp[:, None] * freq[None, :]
    ang = jnp.concatenate([ang_r, ang_r, ang_c, ang_c] * 2, axis=-1)
    return jnp.cos(ang), jnp.sin(ang)


def _sink_column(sink, n_kv, tq):
    n_sub = sink.shape[0] // n_kv
    return jnp.repeat((sink * LOG2E).reshape(n_kv, n_sub), tq, axis=1)[..., None].astype(f32)


def kernel(x, c, ctx, c_ctx, ada_w, ada_b, norm_g, w_out, ffn_up, ffn_conv_w, ffn_conv_b, ffn_down, even_w_in,
           mlstm_gate_b, mlstm_head_g, swa_qk_g, swa_sink, odd_w_in, gqa_qk_g, na_qk_g, na_rpb):
    bsz, seq, d = x.shape
    n_ctx = ctx.shape[1]
    depth = ada_w.shape[0]
    a_w = A_HEADS * A_DIM
    cos, sin = _rope_tables(seq)
    ones_c, zeros_c = jnp.ones((n_ctx, LANES), f32), jnp.zeros((n_ctx, LANES), f32)

    cc = jnp.zeros((16, d), f32).at[:bsz].set(c).at[bsz].set(c_ctx)
    modt = ada_modulation(cc, ada_w, ada_b).reshape(depth, 16, 6, 1, d)

    x_lat, x_ctx = x, ctx
    tm_l, tm_c = 512, 256
    for l in range(depth):
        need_ctx = l < depth - 1
        w_o = w_out[l].astype(bf16)
        if l % 2 == 0:
            e = l // 2
            wi = even_w_in[e]
            sp = np.cumsum((0, a_w, a_w, a_w, a_w, 4 * A_HEADS, B_HEADS * HEAD_DIM, B_KV * HEAD_DIM, B_KV * HEAD_DIM))
            aq, ak, av, ao, ag, bq, bk, bv = (wi[:, sp[k]:sp[k + 1]] for k in range(8))
            w = jnp.concatenate([aq, ak, av, ao, ag, jnp.zeros((d, LANES - 4 * A_HEADS), f32), bq, bk, bv],
                                axis=1).astype(bf16)
            gate_col = 4 * a_w
            n_raw = gate_col + LANES
            gq, gk = swa_qk_g[e, 0], swa_qk_g[e, 1]
            cols = dict(n_raw=n_raw, q_cols=(n_raw,), kpair_cols=(n_raw + B_HEADS * HEAD_DIM,), kfull_cols=(),
                        vfull_cols=(), gains=(gq, gk))
            y_lat, qn_l, k2_l, v2_l = proj_prep(x_lat, modt, l, None, norm_g[l, 0], w, **cols, rope=(True, True),
                                                cos=cos, sin=sin, tm=tm_l)
            y_ctx, qn_c, k2_c, v2_c = proj_prep(x_ctx, modt, l, bsz, norm_g[l, 0], w, **cols, rope=(False, False),
                                                cos=ones_c, sin=zeros_c, tm=tm_c)
            a_ctx, a_lat = mlstm_mixer(y_ctx, y_lat, gate_col // LANES, mlstm_gate_b[e], mlstm_head_g[e])
            b_lat = window_attention(qn_l, k2_l, v2_l, k2_c, v2_c, _sink_column(swa_sink[e], B_KV, B_WIN))
            mix_l = (a_lat, b_lat)
            if need_ctx:
                b_ctx = gqa_attention(qn_c, [(k2_c, v2_c)], B_KV, 128, _sink_column(swa_sink[e], B_KV, 128))
                mix_c = (a_ctx, b_ctx)
        else:
            o = l // 2
            wi = odd_w_in[o]
            sp = np.cumsum((0, C_HEADS * HEAD_DIM, C_KV * HEAD_DIM, C_KV * HEAD_DIM) + (D_HEADS * HEAD_DIM,) * 3)
            cq, ck, cv, nq, nk, nv = (wi[:, sp[k]:sp[k + 1]] for k in range(6))
            w_ctx = jnp.concatenate([nk, nv, ck, cv], axis=1).astype(bf16)
            w = jnp.concatenate([cq.astype(bf16), nq.astype(bf16), w_ctx], axis=1)
            gq, gk = gqa_qk_g[o, 0], gqa_qk_g[o, 1]
            nq_g, nk_g = na_qk_g[o, 0], na_qk_g[o, 1]
            wide = 4 * LANES
            cqn, nqn, ck2, cv2, nkn, nvb = proj_prep(
                x_lat, modt, l, None, norm_g[l, 0], w, n_raw=0, q_cols=(0, wide), kpair_cols=(4 * wide,),
                kfull_cols=(2 * wide,), vfull_cols=(3 * wide,), gains=(gq, nq_g, gk, nk_g),
                rope=(True, False, True), cos=cos, sin=sin, tm=tm_l)
            if need_ctx:
                raise NotImplementedError("context outputs of an odd layer")
            ck2_c, cv2_c, nkn_c, nvb_c = proj_prep(
                x_ctx, modt, l, bsz, norm_g[l, 0], w_ctx, n_raw=0, q_cols=(), kpair_cols=(2 * wide,),
                kfull_cols=(0,), vfull_cols=(wide,), gains=(gk, nk_g), rope=(False,), cos=ones_c, sin=zeros_c,
                tm=tm_c)
            c_lat = gqa_attention(cqn, [(ck2_c, cv2_c), (ck2, cv2)], C_KV, 256)
            d_lat = neighbourhood_attention(nqn, nkn, nvb, nkn_c, nvb_c, na_bias_table(na_rpb[o], seq // GRID_W))
            mix_l = (c_lat, d_lat)

        ffn_w = ffn_weights(ffn_up[l], ffn_conv_w[l], ffn_conv_b[l], ffn_down[l])
        x_lat = out_proj(x_lat, mix_l[0], mix_l[1], w_o, modt, l, None, tm_l)
        x_lat = conv_ffn(x_lat, modt, l, None, norm_g[l, 1], ffn_w, tm=512)
        if need_ctx:
            x_ctx = out_proj(x_ctx, mix_c[0], mix_c[1], w_o, modt, l, bsz, tm_c)
            x_ctx = conv_ffn(x_ctx, modt, l, bsz, norm_g[l, 1], ffn_w, tm=256)
    return x_lat
```

```python
import functools

import jax
import jax.numpy as jnp
import numpy as np
from jax import lax
from jax.experimental import pallas as pl
from jax.experimental.pallas import tpu as pltpu

f32 = jnp.float32
bf16 = jnp.bfloat16

GRID_W = 64
HEAD_DIM = 64
LANES = 128
A_HEADS = 4
A_DIM = 128
B_HEADS = 8
B_KV = 2
B_WIN = 128
C_HEADS = 8
C_KV = 2
D_HEADS = 8
NA_ROWS = 8
NA_COLS = 16
NA_QROWS = 4
NA_PAIRS = 2
NA_KROWS = 12
ROPE_THETA = 10000.0
EPS = 1e-6
NEG = -1e30
LOG2E = 1.4426950408889634
VMEM_LIMIT = 56 * 1024 * 1024


def _cparams(sem):
    return pltpu.CompilerParams(dimension_semantics=sem, vmem_limit_bytes=VMEM_LIMIT)


def _dot(a, b):
    return jnp.dot(a, b, preferred_element_type=f32)


def _dot_nt(a, b):
    return lax.dot_general(a, b, (((1,), (1,)), ((), ())), preferred_element_type=f32)


def _modulated_norm(x, g, shift, scale):
    y = x * lax.rsqrt(jnp.mean(x * x, axis=-1, keepdims=True) + EPS)
    return (y * g) * (1.0 + scale) + shift


def _mod_spec(l, k, row, d):
    if row is None:
        return pl.BlockSpec((None, None, None, 1, d), lambda b, *_: (l, b, k, 0, 0))
    return pl.BlockSpec((None, None, None, 1, d), lambda b, *_: (l, row, k, 0, 0))


def _ada_kernel(c_ref, w_ref, b_ref, o_ref):
    c = c_ref[...]
    s = c / (1.0 + jnp.exp(-c))
    o_ref[...] = _dot(s.astype(bf16), w_ref[...].astype(bf16)) + b_ref[...]


def ada_modulation(cc, ada_w, ada_b):
    depth, d, n = ada_w.shape
    tn = 1536
    return pl.pallas_call(
        _ada_kernel,
        out_shape=jax.ShapeDtypeStruct((depth, cc.shape[0], n), f32),
        grid=(depth, n // tn),
        in_specs=[pl.BlockSpec(cc.shape, lambda l, j: (0, 0)),
                  pl.BlockSpec((None, d, tn), lambda l, j: (l, 0, j)),
                  pl.BlockSpec((None, 1, tn), lambda l, j: (l, 0, j))],
        out_specs=pl.BlockSpec((None, cc.shape[0], tn), lambda l, j: (l, 0, j)),
        compiler_params=_cparams(("parallel", "parallel")),
        name="ada",
    )(cc, ada_w, ada_b.reshape(depth, 1, n))


def _head_ms(a):
    ri = lax.broadcasted_iota(jnp.int32, (LANES, LANES), 0) // HEAD_DIM
    ci = lax.broadcasted_iota(jnp.int32, (LANES, LANES), 1) // HEAD_DIM
    bd = jnp.where(ri == ci, 1.0, 0.0).astype(bf16)
    ss = a * a
    hi = ss.astype(bf16)
    lo = (ss - hi.astype(f32)).astype(bf16)
    return (_dot(hi, bd) + _dot(lo, bd)) * (1.0 / HEAD_DIM)


def _head_norm(a, g):
    return (a * lax.rsqrt(_head_ms(a) + EPS)) * g


def _rope(a, cos, sin):
    lane = lax.broadcasted_iota(jnp.int32, a.shape, 1)
    quarter = HEAD_DIM // 4
    rot = jnp.where(lane % (2 * quarter) < quarter,
                    -pltpu.roll(a, LANES - quarter, axis=1), pltpu.roll(a, quarter, axis=1))
    return a * cos + rot * sin


def _dup_halves(a):
    lane = lax.broadcasted_iota(jnp.int32, a.shape, 1)
    sw = pltpu.roll(a, HEAD_DIM, axis=1)
    lo = lane < HEAD_DIM
    return jnp.where(lo, a, sw), jnp.where(lo, sw, a)


def _prep_kernel(*refs, n_q, n_kpair, n_kfull, n_vfull, rope, scale):
    it = iter(refs)
    q_refs = [next(it) for _ in range(n_q)]
    kp_refs = [next(it) for _ in range(n_kpair)]
    kf_refs = [next(it) for _ in range(n_kfull)]
    vf_refs = [next(it) for _ in range(n_vfull)]
    g_refs = [next(it) for _ in range(n_q + n_kpair + n_kfull)]
    cos_ref, sin_ref = next(it), next(it)
    qo_refs = [next(it) for _ in range(n_q)]
    kpo_refs = [(next(it), next(it)) for _ in range(n_kpair)]
    kfo_refs = [next(it) for _ in range(n_kfull)]
    vfo_refs = [next(it) for _ in range(n_vfull)]
    gi = iter(g_refs)
    for qi, (q_ref, qo_ref) in enumerate(zip(q_refs, qo_refs)):
        g = next(gi)[...]
        for p in range(q_ref.shape[-1] // LANES):
            a = _head_norm(q_ref[:, p * LANES:(p + 1) * LANES], g)
            if rope[qi]:
                a = _rope(a, cos_ref[...], sin_ref[...])
            qo_ref[:, p * LANES:(p + 1) * LANES] = (a * scale).astype(bf16)
    for ki, (kp_ref, (ko_ref, vo_ref)) in enumerate(zip(kp_refs, kpo_refs)):
        g = next(gi)[...]
        k = _head_norm(kp_ref[:, :LANES], g)
        if rope[n_q + ki]:
            k = _rope(k, cos_ref[...], sin_ref[...])
        k0, k1 = _dup_halves(k)
        ko_ref[0] = k0.astype(bf16)
        ko_ref[1] = k1.astype(bf16)
        v0, v1 = _dup_halves(kp_ref[:, LANES:])
        vo_ref[0] = v0.astype(bf16)
        vo_ref[1] = v1.astype(bf16)
    for kf_ref, kfo_ref in zip(kf_refs, kfo_refs):
        g = next(gi)[...]
        for p in range(kf_ref.shape[-1] // LANES):
            kfo_ref[:, p * LANES:(p + 1) * LANES] = _head_norm(kf_ref[:, p * LANES:(p + 1) * LANES], g).astype(bf16)
    for vf_ref, vfo_ref in zip(vf_refs, vfo_refs):
        vfo_ref[...] = vf_ref[...].astype(bf16)


def _proj_prep_kernel(x_ref, sh_ref, sc_ref, g_ref, w_ref, *rest, n_raw, q_cols, kpair_cols, kfull_cols, vfull_cols,
                      rope):
    n_gain = len(q_cols) + len(kpair_cols) + len(kfull_cols)
    side = rest[:n_gain + 2]
    outs = rest[n_gain + 2:-1]
    y_scr = rest[-1]
    h = _modulated_norm(x_ref[...], g_ref[...], sh_ref[...], sc_ref[...]).astype(bf16)
    y_scr[:, n_raw:] = _dot(h, w_ref[:, n_raw:])
    if n_raw:
        outs[0][...] = _dot(h, w_ref[:, :n_raw])
        outs = outs[1:]
    wide, pair = 4 * LANES, 2 * LANES
    views = ([y_scr.at[:, c:c + wide] for c in q_cols] + [y_scr.at[:, c:c + pair] for c in kpair_cols]
             + [y_scr.at[:, c:c + wide] for c in kfull_cols + vfull_cols])
    _prep_kernel(*views, *side, *outs, n_q=len(q_cols), n_kpair=len(kpair_cols), n_kfull=len(kfull_cols),
                 n_vfull=len(vfull_cols), rope=rope, scale=HEAD_DIM ** -0.5 * LOG2E)


def proj_prep(x, modt, l, row, norm_g, w, *, n_raw, q_cols, kpair_cols, kfull_cols, vfull_cols, gains, rope, cos, sin,
              tm):
    bsz, t, d = x.shape
    n = w.shape[1]
    wide = 4 * LANES
    in_specs = [pl.BlockSpec((None, tm, d), lambda b, i: (b, i, 0)),
                _mod_spec(l, 0, row, d), _mod_spec(l, 1, row, d),
                pl.BlockSpec((1, d), lambda b, i: (0, 0)),
                pl.BlockSpec((d, n), lambda b, i: (0, 0))]
    args = [x, modt, modt, norm_g.reshape(1, d), w]
    for g in gains:
        in_specs.append(pl.BlockSpec((1, LANES), lambda b, i: (0, 0)))
        args.append(jnp.tile(g, 2).reshape(1, LANES))
    for tbl in (cos, sin):
        in_specs.append(pl.BlockSpec((tm, LANES), lambda b, i: (i, 0)))
        args.append(tbl)
    out_shape, out_specs = [], []
    if n_raw:
        out_shape.append(jax.ShapeDtypeStruct((bsz, t, n_raw), f32))
        out_specs.append(pl.BlockSpec((None, tm, n_raw), lambda b, i: (b, i, 0)))
    for _ in q_cols:
        out_shape.append(jax.ShapeDtypeStruct((bsz, t, wide), bf16))
        out_specs.append(pl.BlockSpec((None, tm, wide), lambda b, i: (b, i, 0)))
    for _ in kpair_cols:
        for _ in range(2):
            out_shape.append(jax.ShapeDtypeStruct((bsz, 2, t, LANES), bf16))
            out_specs.append(pl.BlockSpec((None, 2, tm, LANES), lambda b, i: (b, 0, i, 0)))
    for _ in kfull_cols + vfull_cols:
        out_shape.append(jax.ShapeDtypeStruct((bsz, t, wide), bf16))
        out_specs.append(pl.BlockSpec((None, tm, wide), lambda b, i: (b, i, 0)))
    kern = functools.partial(_proj_prep_kernel, n_raw=n_raw, q_cols=q_cols, kpair_cols=kpair_cols,
                             kfull_cols=kfull_cols, vfull_cols=vfull_cols, rope=rope)
    return pl.pallas_call(
        kern, out_shape=out_shape, grid=(bsz, t // tm), in_specs=in_specs, out_specs=out_specs,
        scratch_shapes=[pltpu.VMEM((tm, n), f32)],
        compiler_params=_cparams(("parallel", "parallel")), name="proj_prep",
    )(*args)


def _stack_heads(q_ref, n_sub):
    parts = []
    for g in range(n_sub):
        blk = q_ref[:, (g // 2) * LANES:(g // 2 + 1) * LANES]
        lane = lax.broadcasted_iota(jnp.int32, blk.shape, 1)
        keep = (lane < HEAD_DIM) if g % 2 == 0 else (lane >= HEAD_DIM)
        parts.append(jnp.where(keep, blk, jnp.zeros_like(blk)))
    return jnp.concatenate(parts, axis=0)


def _scores(q, segs):
    scores = []
    for k, _, bias, mask in segs:
        s = _dot_nt(q, k)
        if bias is not None:
            s = s + bias
        if mask is not None:
            s = jnp.where(mask, s, NEG)
        scores.append(s)
    return scores


def _softmax_attend(q, segs, sink, scores=None):
    return _softmax_attend_chains([(q, segs, sink, scores)])[0]


def _softmax_attend_chains(chains):
    scores = [_scores(q, segs) if sc is None else sc for q, segs, _, sc in chains]
    maxes = [[s.max(axis=-1, keepdims=True) for s in sc] for sc in scores]
    m = []
    for (_, _, sink, _), mx in zip(chains, maxes):
        mi = functools.reduce(jnp.maximum, mx)
        m.append(mi if sink is None else jnp.maximum(mi, sink))
    p = [[jnp.exp2(s - m[i]) for s in sc] for i, sc in enumerate(scores)]
    outs = []
    for i, (_, segs, sink, _) in enumerate(chains):
        den = functools.reduce(jnp.add, [pj.sum(axis=-1, keepdims=True) for pj in p[i]])
        if sink is not None:
            den = den + jnp.exp2(sink - m[i])
        acc = functools.reduce(jnp.add, [_dot(pj.astype(bf16), seg[1]) for pj, seg in zip(p[i], segs)])
        outs.append(acc / den)
    return outs


def _unstack_heads(o, o_ref, n_sub, tq):
    lane = lax.broadcasted_iota(jnp.int32, (tq, LANES), 1)
    for p in range(n_sub // 2):
        even = o[(2 * p) * tq:(2 * p + 1) * tq]
        odd = o[(2 * p + 1) * tq:(2 * p + 2) * tq]
        o_ref[:, p * LANES:(p + 1) * LANES] = jnp.where(lane < HEAD_DIM, even, odd).astype(o_ref.dtype)


def _window_attn_kernel(q_ref, kc_ref, vc_ref, kp_ref, k0_ref, kn_ref, vp_ref, v0_ref, vn_ref, sink_ref, o_ref, *, tq):
    i = pl.program_id(2)
    nb = pl.num_programs(2)
    n_sub = B_HEADS // B_KV
    n_ctx = kc_ref.shape[0]
    rows, n_keys = n_sub * tq, n_ctx + 3 * tq
    qpos = lax.broadcasted_iota(jnp.int32, (rows, n_keys), 0) % tq
    kpos = lax.broadcasted_iota(jnp.int32, (rows, n_keys), 1) - n_ctx
    in_prev, in_next = (kpos >= 0) & (kpos < tq), kpos >= 2 * tq
    bad = (in_prev & (kpos < qpos)) | (in_next & (kpos - 2 * tq > qpos))
    k_blocks = [kp_ref[...], k0_ref[:tq], k0_ref[tq:], kn_ref[...]]
    v_blocks = [vp_ref[...], v0_ref[:tq], v0_ref[tq:], vn_ref[...]]
    chains = []
    for j, missing in enumerate((in_prev & (i == 0), in_next & (i == nb - 1))):
        k_all = jnp.concatenate([kc_ref[...]] + k_blocks[j:j + 3], axis=0)
        v_all = jnp.concatenate([vc_ref[...]] + v_blocks[j:j + 3], axis=0)
        q = _stack_heads(q_ref.at[j * tq:(j + 1) * tq], n_sub)
        chains.append((q, [(k_all, v_all, None, ~(bad | missing))], sink_ref[...], None))
    for j, o in enumerate(_softmax_attend_chains(chains)):
        _unstack_heads(o, o_ref.at[j * tq:(j + 1) * tq], n_sub, tq)


def window_attention(qn, k2, v2, kc2, vc2, sink_col):
    bsz, s, _ = qn.shape
    c = kc2.shape[2]
    tq = B_WIN
    nb = s // (2 * tq)
    n_half = s // tq
    n_sub = B_HEADS // B_KV
    wq = n_sub * HEAD_DIM
    half_spec = lambda fn: pl.BlockSpec((None, None, tq, LANES), fn)
    prev = lambda b, h, i: (b, h, jnp.maximum(2 * i - 1, 0), 0)
    nxt = lambda b, h, i: (b, h, jnp.minimum(2 * i + 2, n_half - 1), 0)
    cur_spec = pl.BlockSpec((None, None, 2 * tq, LANES), lambda b, h, i: (b, h, i, 0))
    ctx_spec = pl.BlockSpec((None, None, c, LANES), lambda b, h, i: (b, h, 0, 0))
    return pl.pallas_call(
        functools.partial(_window_attn_kernel, tq=tq),
        out_shape=jax.ShapeDtypeStruct((bsz, s, B_HEADS * HEAD_DIM), bf16),
        grid=(bsz, B_KV, nb),
        in_specs=[pl.BlockSpec((None, 2 * tq, wq), lambda b, h, i: (b, i, h)),
                  ctx_spec, ctx_spec,
                  half_spec(prev), cur_spec, half_spec(nxt),
                  half_spec(prev), cur_spec, half_spec(nxt),
                  pl.BlockSpec((None, n_sub * tq, 1), lambda b, h, i: (h, 0, 0))],
        out_specs=pl.BlockSpec((None, 2 * tq, wq), lambda b, h, i: (b, i, h)),
        compiler_params=_cparams(("parallel", "parallel", "parallel")),
        name="window_attn",
    )(qn, kc2, vc2, k2, k2, k2, v2, v2, v2, sink_col)


def _seg_attn_kernel(*refs, n_seg, n_sub, tq, has_sink):
    q_ref = refs[0]
    kv = refs[1:1 + 2 * n_seg]
    sink = refs[1 + 2 * n_seg][...] if has_sink else None
    o_ref = refs[-1]
    segs = [(kv[2 * j][...], kv[2 * j + 1][...], None, None) for j in range(n_seg)]
    n_pair = n_sub // 2
    lanes = [slice(p * LANES, (p + 1) * LANES) for p in range(n_pair)]
    chains = [(_stack_heads(q_ref.at[:, lanes[p]], 2), segs,
               None if sink is None else sink[2 * p * tq:(2 * p + 2) * tq], None) for p in range(n_pair)]
    for p, o in enumerate(_softmax_attend_chains(chains)):
        _unstack_heads(o, o_ref.at[:, lanes[p]], 2, tq)


def gqa_attention(qn, kv_segs, n_kv, tq, sink_col=None):
    bsz, s, width = qn.shape
    n_sub = width // HEAD_DIM // n_kv
    wq = n_sub * HEAD_DIM
    in_specs = [pl.BlockSpec((None, tq, wq), lambda b, h, i: (b, i, h))]
    args = [qn]
    for k2, v2 in kv_segs:
        n = k2.shape[2]
        spec = pl.BlockSpec((None, None, n, LANES), lambda b, h, i: (b, h, 0, 0))
        in_specs += [spec, spec]
        args += [k2, v2]
    if sink_col is not None:
        in_specs.append(pl.BlockSpec((None, n_sub * tq, 1), lambda b, h, i: (h, 0, 0)))
        args.append(sink_col)
    return pl.pallas_call(
        functools.partial(_seg_attn_kernel, n_seg=len(kv_segs), n_sub=n_sub, tq=tq, has_sink=sink_col is not None),
        out_shape=jax.ShapeDtypeStruct((bsz, s, width), bf16),
        grid=(bsz, n_kv, s // tq),
        in_specs=in_specs,
        out_specs=pl.BlockSpec((None, tq, wq), lambda b, h, i: (b, i, h)),
        compiler_params=_cparams(("parallel", "parallel", "parallel")),
        name="gqa_attn",
    )(*args)


def _na_attn_kernel(q_ref, kc_ref, vc_ref, k_ref, v_ref, bias_ref, o_ref, *, tq, n_grp):
    r = pl.program_id(2)
    rows_total = k_ref.shape[0] // GRID_W
    start = jnp.clip(r * NA_QROWS - NA_ROWS // 2, 0, rows_total - NA_KROWS)
    off = pl.multiple_of(start * GRID_W, GRID_W)
    nk = NA_KROWS * GRID_W
    lane = lax.broadcasted_iota(jnp.int32, (tq, LANES), 1)
    chains = []
    for p in range(NA_PAIRS):
        lanes = slice(p * LANES, (p + 1) * LANES)
        blk = q_ref[:, lanes]
        k_nb, v_nb = k_ref[pl.ds(off, nk), lanes], v_ref[pl.ds(off, nk), lanes]
        for half in range(2):
            keep = (lane < HEAD_DIM) if half == 0 else (lane >= HEAD_DIM)
            segs = [(kc_ref[:, lanes], vc_ref[:, lanes], None, None),
                    (k_nb, v_nb, bias_ref[p, half * tq:(half + 1) * tq, :], None)]
            chains.append((jnp.where(keep, blk, jnp.zeros_like(blk)), segs, None, None))
    outs = _softmax_attend_chains(chains)
    for p in range(NA_PAIRS):
        o_ref[:, p * LANES:(p + 1) * LANES] = jnp.where(lane < HEAD_DIM, outs[2 * p], outs[2 * p + 1]).astype(o_ref.dtype)


def neighbourhood_attention(qn, kn, vb, kcn, vcb, bias_tbl):
    bsz, s, width = qn.shape
    c = kcn.shape[1]
    tq = NA_QROWS * GRID_W
    n_grp = s // tq
    wide = NA_PAIRS * LANES
    variant = lambda r: jnp.where(r == 0, 0, jnp.where(r == n_grp - 1, 2, 1))
    return pl.pallas_call(
        functools.partial(_na_attn_kernel, tq=tq, n_grp=n_grp),
        out_shape=jax.ShapeDtypeStruct((bsz, s, width), bf16),
        grid=(width // wide, bsz, n_grp),
        in_specs=[pl.BlockSpec((None, tq, wide), lambda p, b, r: (b, r, p)),
                  pl.BlockSpec((None, c, wide), lambda p, b, r: (b, 0, p)),
                  pl.BlockSpec((None, c, wide), lambda p, b, r: (b, 0, p)),
                  pl.BlockSpec((None, s, wide), lambda p, b, r: (b, 0, p)),
                  pl.BlockSpec((None, s, wide), lambda p, b, r: (b, 0, p)),
                  pl.BlockSpec((None, NA_PAIRS, 2 * tq, NA_KROWS * GRID_W), lambda p, b, r: (variant(r), p, 0, 0))],
        out_specs=pl.BlockSpec((None, tq, wide), lambda p, b, r: (b, r, p)),
        compiler_params=_cparams(("parallel", "parallel", "parallel")),
        name="na_attn",
    )(qn, kcn, vcb, kn, vb, bias_tbl)


def _na_bias_kernel(rp_ref, o_ref, *, rows_total):
    n_grp = rows_total // NA_QROWS
    qc = lax.broadcasted_iota(jnp.int32, (GRID_W, LANES), 0)
    lane = lax.broadcasted_iota(jnp.int32, (GRID_W, LANES), 1)
    kc = lane % GRID_W
    cs = jnp.clip(qc - NA_COLS // 2, 0, GRID_W - NA_COLS)
    col_ok = (kc >= cs) & (kc < cs + NA_COLS)
    left = lane < GRID_W
    neg = jnp.full((GRID_W, LANES), NEG, f32)
    for v, r0 in enumerate((0, NA_QROWS, (n_grp - 1) * NA_QROWS)):
        start = min(max(r0 - NA_ROWS // 2, 0), rows_total - NA_KROWS)
        for hh in range(2):
            for i in range(NA_QROWS):
                qr = r0 + i
                ws = min(max(qr - NA_ROWS // 2, 0), rows_total - NA_ROWS)
                for jj in range(NA_KROWS // 2):
                    kra, krb = start + 2 * jj, start + 2 * jj + 1
                    ok_a, ok_b = ws <= kra < ws + NA_ROWS, ws <= krb < ws + NA_ROWS
                    r_lo = hh * NA_QROWS * GRID_W + i * GRID_W
                    dst = (v, slice(r_lo, r_lo + GRID_W), slice(jj * LANES, (jj + 1) * LANES))
                    if not (ok_a or ok_b):
                        o_ref[dst] = neg
                        continue
                    dra = min(max(kra - qr + NA_ROWS - 1, 0), 2 * NA_ROWS - 2)
                    drb = min(max(krb - qr + NA_ROWS - 1, 0), 2 * NA_ROWS - 2)
                    row = jnp.where(left[:1], rp_ref[hh, dra:dra + 1, :], rp_ref[hh, drb:drb + 1, :])
                    toe = pltpu.roll(jnp.broadcast_to(row, (GRID_W, LANES)), LANES - (NA_COLS - 1), axis=1,
                                     stride=1, stride_axis=0)
                    ok = col_ok if (ok_a and ok_b) else (col_ok & left if ok_a else col_ok & ~left)
                    o_ref[dst] = jnp.where(ok, toe * LOG2E, neg)


def na_bias_table(rpb, rows_total):
    h, ndr, ndc = rpb.shape
    rp = jnp.zeros((h, 16, LANES), f32).at[:, :ndr, :ndc].set(rpb).at[:, :ndr, GRID_W:GRID_W + ndc].set(rpb)
    return pl.pallas_call(
        functools.partial(_na_bias_kernel, rows_total=rows_total),
        out_shape=jax.ShapeDtypeStruct((3, h // 2, 2 * NA_QROWS * GRID_W, NA_KROWS * GRID_W), f32),
        grid=(h // 2,),
        in_specs=[pl.BlockSpec((2, 16, LANES), lambda p: (p, 0, 0))],
        out_specs=pl.BlockSpec((3, None, 2 * NA_QROWS * GRID_W, NA_KROWS * GRID_W), lambda p: (0, p, 0, 0)),
        compiler_params=_cparams(("parallel",)),
        name="na_bias",
    )(rp)


def _log_sigmoid(x):
    return jnp.minimum(x, 0.0) - jnp.log1p(jnp.exp(-jnp.abs(x)))


MLSTM_L = 128
MLSTM_HP = 4


def _mlstm_chunks(cx_ref, m_ref, eye, chains):
    every = lambda fn: [fn(i, ch) for i, ch in enumerate(chains)]
    cx = every(lambda i, ch: cx_ref[ch["slot"]])
    m = every(lambda i, ch: m_ref[ch["slot"]])
    b_col = every(lambda i, ch: jnp.sum(jnp.where(ch["tri"], ch["lf"], 0.0), axis=1, keepdims=True))
    b_row = every(lambda i, ch: jnp.sum(jnp.where(eye, b_col[i], 0.0), axis=0, keepdims=True))
    b_last = every(lambda i, ch: jnp.sum(ch["lf"], axis=1, keepdims=True))
    dmat = every(lambda i, ch: jnp.where(ch["tri"], b_col[i] - b_row[i] + ch["li"], -jnp.inf))
    inter = every(lambda i, ch: b_col[i] + m[i])
    m_t = every(lambda i, ch: jnp.maximum(inter[i], jnp.max(dmat[i], axis=1, keepdims=True)))
    w = every(lambda i, ch: jnp.exp(dmat[i] - m_t[i]))
    a = every(lambda i, ch: jnp.exp(inter[i] - m_t[i]))
    vx = every(lambda i, ch: jnp.concatenate([ch["v"], jnp.ones_like(ch["v"])], axis=1))
    s = every(lambda i, ch: _dot(ch["q"], ch["k_t"].astype(bf16)) * w[i])
    qc = every(lambda i, ch: _dot(ch["q"], cx[i].astype(bf16)))
    nd = every(lambda i, ch: a[i] * qc[i] + _dot(s[i].astype(bf16), vx[i]))
    h = every(lambda i, ch: nd[i][:, :A_DIM] / jnp.maximum(jnp.abs(nd[i][:, A_DIM:]), jnp.exp(-m_t[i])))
    g_row = every(lambda i, ch: b_last[i] - b_row[i] + ch["li"])
    m_new = every(lambda i, ch: jnp.maximum(b_last[i] + m[i], jnp.max(g_row[i], axis=1, keepdims=True)))
    decay = every(lambda i, ch: jnp.exp(b_last[i] + m[i] - m_new[i]))
    wk_t = every(lambda i, ch: (ch["k_t"] * jnp.exp(g_row[i] - m_new[i])).astype(bf16))
    for i, ch in enumerate(chains):
        cx_ref[ch["slot"]] = decay[i] * cx[i] + _dot(wk_t[i], vx[i])
        m_ref[ch["slot"]] = m_new[i]
    return h


def _mlstm_kernel(gb_ref, qc_ref, kc_ref, vc_ref, oc_ref, gc_ref, ql_ref, kl_ref, vl_ref, ol_ref, gl_ref, hg_ref,
                  outc_ref, outl_ref, cx_scr, m_scr, g_scr, hc_scr, hl_scr):
    hg = pl.program_id(1)
    L = MLSTM_L
    cx_scr[...] = jnp.zeros_like(cx_scr)
    m_scr[...] = jnp.zeros_like(m_scr)
    kscale = A_DIM ** -0.5
    t_idx = lax.broadcasted_iota(jnp.int32, (L, L), 0)
    s_idx = lax.broadcasted_iota(jnp.int32, (L, L), 1)
    eye = s_idx == t_idx
    tri = (s_idx <= t_idx, s_idx >= t_idx)
    n_gates = 4 * A_HEADS

    def run(q_ref, k_ref, v_ref, g_ref, h_ref):
        nc = q_ref.shape[0] // L

        def body(j, accumulate):
            chains = []
            for d in range(2):
                cj = j if d == 0 else nc - 1 - j
                rows = pl.ds(pl.multiple_of(cj * L, L), L)
                g_scr[d] = g_ref[rows, :].T[:n_gates]
                for hh in range(MLSTM_HP):
                    hd = hg * MLSTM_HP + hh
                    ig, fg = (2 * d) * A_HEADS + hd, (2 * d + 1) * A_HEADS + hd
                    cols = slice(hh * A_DIM, (hh + 1) * A_DIM)
                    chains.append(dict(
                        slot=2 * hh + d, tri=tri[d], rows=rows, cols=cols,
                        li=g_scr[d, pl.ds(ig, 1), :] + gb_ref[ig],
                        lf=_log_sigmoid(g_scr[d, pl.ds(fg, 1), :] + gb_ref[fg]),
                        q=q_ref[rows, cols].astype(bf16), v=v_ref[rows, cols].astype(bf16),
                        k_t=(k_ref[rows, cols] * kscale).T))
            for ch, h in zip(chains, _mlstm_chunks(cx_scr, m_scr, eye, chains)):
                if accumulate:
                    h_ref[ch["rows"], ch["cols"]] += h
                else:
                    h_ref[ch["rows"], ch["cols"]] = h

        lax.fori_loop(0, nc // 2, lambda j, c: (body(j, False), c)[1], 0)
        lax.fori_loop(nc // 2, nc, lambda j, c: (body(j, True), c)[1], 0)

    run(qc_ref, kc_ref, vc_ref, gc_ref, hc_scr)
    run(ql_ref, kl_ref, vl_ref, gl_ref, hl_scr)
    for o_ref, h_ref, out_ref in ((oc_ref, hc_scr, outc_ref), (ol_ref, hl_scr, outl_ref)):
        for hh in range(MLSTM_HP):
            cols = slice(hh * A_DIM, (hh + 1) * A_DIM)
            h = h_ref[:, cols]
            hn = (h * lax.rsqrt(jnp.mean(h * h, axis=-1, keepdims=True) + EPS)) * hg_ref[:, cols]
            o = o_ref[:, cols]
            out_ref[:, cols] = (hn / (1.0 + jnp.exp(-o))).astype(out_ref.dtype)


def mlstm_mixer(y_ctx, y_lat, gate_block, gate_b, head_g):
    bsz, tc, _ = y_ctx.shape
    tl = y_lat.shape[1]
    wide = MLSTM_HP * A_DIM
    n_grp = A_HEADS // MLSTM_HP

    def col(t, base):
        return pl.BlockSpec((None, t, wide), functools.partial(lambda b, h, base: (b, 0, base + h), base=base))

    def gate_spec(t):
        return pl.BlockSpec((None, t, LANES), lambda b, h: (b, 0, gate_block))

    def out_spec(t):
        return pl.BlockSpec((None, t, wide), lambda b, h: (b, 0, h))

    return pl.pallas_call(
        _mlstm_kernel,
        out_shape=[jax.ShapeDtypeStruct((bsz, tc, A_HEADS * A_DIM), bf16),
                   jax.ShapeDtypeStruct((bsz, tl, A_HEADS * A_DIM), bf16)],
        grid=(bsz, n_grp),
        in_specs=[pl.BlockSpec(memory_space=pltpu.SMEM)]
        + [col(tc, n_grp * j) for j in range(4)] + [gate_spec(tc)]
        + [col(tl, n_grp * j) for j in range(4)] + [gate_spec(tl)]
        + [pl.BlockSpec((1, wide), lambda b, h: (0, h))],
        out_specs=[out_spec(tc), out_spec(tl)],
        scratch_shapes=[pltpu.VMEM((2 * MLSTM_HP, A_DIM, 2 * A_DIM), f32), pltpu.VMEM((2 * MLSTM_HP, 1, 1), f32),
                        pltpu.VMEM((2, 4 * A_HEADS, LANES), f32),
                        pltpu.VMEM((tc, wide), f32), pltpu.VMEM((tl, wide), f32)],
        compiler_params=_cparams(("parallel", "parallel")),
        name="mlstm",
    )(gate_b, y_ctx, y_ctx, y_ctx, y_ctx, y_ctx, y_lat, y_lat, y_lat, y_lat, y_lat,
      head_g.reshape(1, A_HEADS * A_DIM))


def _outproj_kernel(x_ref, a1_ref, a2_ref, w1_ref, w2_ref, gate_ref, o_ref):
    y = _dot(a1_ref[...], w1_ref[...]) + _dot(a2_ref[...], w2_ref[...])
    o_ref[...] = x_ref[...] + gate_ref[...] * y


def out_proj(x, a1, a2, w, modt, l, row, tm):
    bsz, t, d = x.shape
    k1, k2 = a1.shape[-1], a2.shape[-1]
    return pl.pallas_call(
        _outproj_kernel,
        out_shape=jax.ShapeDtypeStruct((bsz, t, d), f32),
        grid=(bsz, t // tm),
        in_specs=[pl.BlockSpec((None, tm, d), lambda b, i: (b, i, 0)),
                  pl.BlockSpec((None, tm, k1), lambda b, i: (b, i, 0)),
                  pl.BlockSpec((None, tm, k2), lambda b, i: (b, i, 0)),
                  pl.BlockSpec((k1, d), lambda b, i: (0, 0)),
                  pl.BlockSpec((k2, d), lambda b, i: (0, 0)),
                  _mod_spec(l, 2, row, d)],
        out_specs=pl.BlockSpec((None, tm, d), lambda b, i: (b, i, 0)),
        compiler_params=_cparams(("parallel", "parallel")),
        name="out_proj",
    )(x, a1, a2, w[:k1], w[k1:], modt)


FFN_HALO = 8


FFN_CHUNK = 256


def _ffn_kernel(x_ref, xp_ref, xn_ref, sh_ref, sc_ref, gate_ref, g_ref, wup_ref, cw_ref, cb_ref, wd_ref,
                o_ref, h_scr, u_scr, acc_scr, *, tm):
    i = pl.program_id(1)
    g, sh, sc = g_ref[...], sh_ref[...], sc_ref[...]
    hp = _modulated_norm(xp_ref[...], g, sh, sc)
    hn = _modulated_norm(xn_ref[...], g, sh, sc)
    h_scr[:FFN_HALO] = jnp.where(i > 0, hp, 0.0).astype(bf16)
    h_scr[FFN_HALO:FFN_HALO + tm] = _modulated_norm(x_ref[...], g, sh, sc).astype(bf16)
    h_scr[FFN_HALO + tm:] = jnp.where(i < pl.num_programs(1) - 1, hn, 0.0).astype(bf16)
    dff = wd_ref.shape[0]
    n_chunks = dff // FFN_CHUNK

    def cols(ref, c):
        lo = c * FFN_CHUNK
        return ref[:, lo:lo + FFN_CHUNK], ref[:, dff + lo:dff + lo + FFN_CHUNK]

    half = tm // 2
    up_rows = (slice(0, half + 2 * FFN_HALO), slice(half + 2 * FFN_HALO, tm + 2 * FFN_HALO))

    n_buf = u_scr.shape[0]

    def up_proj(c, rows):
        wg, wv = cols(wup_ref, c)
        u_scr[c % n_buf, rows, :FFN_CHUNK] = _dot(h_scr[rows, :], wg)
        u_scr[c % n_buf, rows, FFN_CHUNK:] = _dot(h_scr[rows, :], wv)

    def gated(c, r):
        u = u_scr.at[c % n_buf]
        cw = jnp.concatenate(cols(cw_ref, c), axis=1)
        cb = jnp.concatenate(cols(cb_ref, c), axis=1)
        S = FFN_HALO
        ng = half // S
        u3 = u[r * half:r * half + half + 2 * S].reshape(ng + 2, S, 2 * FFN_CHUNK)
        sub = lax.broadcasted_iota(jnp.int32, (ng, S, 2 * FFN_CHUNK), 1)
        down = pltpu.roll(u3, 1, axis=1)
        up = pltpu.roll(u3, S - 1, axis=1)
        prev = jnp.where(sub == 0, down[:ng], down[1:ng + 1])
        nxt = jnp.where(sub == S - 1, up[2:], up[1:ng + 1])
        conv = (cw[0:1] * prev + cw[1:2] * u3[1:ng + 1] + cw[2:3] * nxt + cb).reshape(half, 2 * FFN_CHUNK)
        cg, cv = conv[:, :FFN_CHUNK], conv[:, FFN_CHUNK:]
        return ((cg / (1.0 + jnp.exp(-cg))) * cv).astype(bf16)

    def down_proj(act, c0, c1, r):
        part = _dot(act, wd_ref[c0 * FFN_CHUNK:c1 * FFN_CHUNK, :])
        out_rows = slice(r * half, (r + 1) * half)
        if c0 == 0:
            acc_scr[out_rows, :] = part
        else:
            acc_scr[out_rows, :] += part

    pairs = [tuple(range(c, min(c + 2, n_chunks))) for c in range(0, n_chunks, 2)]
    for c in pairs[0]:
        for rows in up_rows:
            up_proj(c, rows)
    pending = None
    for k, pair in enumerate(pairs):
        for r in range(2):
            if k + 1 < len(pairs):
                for c in pairs[k + 1]:
                    up_proj(c, up_rows[r])
            if pending is not None:
                down_proj(*pending)
            acts = [gated(c, r) for c in pair]
            pending = (acts[0] if len(acts) == 1 else jnp.concatenate(acts, axis=1), pair[0], pair[-1] + 1, r)
    down_proj(*pending)
    o_ref[...] = x_ref[...] + gate_ref[...] * acc_scr[...]


def ffn_weights(w_up, conv_w, conv_b, w_down):
    return w_up.astype(bf16), conv_w, conv_b.reshape(1, -1), w_down.astype(bf16)


def conv_ffn(x, modt, l, row, g, weights, tm):
    bsz, t, d = x.shape
    wup, cw, cb, wd = weights
    hb = tm // FFN_HALO
    last = t // FFN_HALO - 1

    def resident(a):
        return pl.BlockSpec(a.shape, lambda b, i: (0,) * a.ndim, pipeline_mode=pl.Buffered(1))

    return pl.pallas_call(
        functools.partial(_ffn_kernel, tm=tm),
        out_shape=jax.ShapeDtypeStruct((bsz, t, d), f32),
        grid=(bsz, t // tm),
        in_specs=[pl.BlockSpec((None, tm, d), lambda b, i: (b, i, 0)),
                  pl.BlockSpec((None, FFN_HALO, d), lambda b, i: (b, jnp.maximum(i * hb - 1, 0), 0)),
                  pl.BlockSpec((None, FFN_HALO, d), lambda b, i: (b, jnp.minimum((i + 1) * hb, last), 0)),
                  _mod_spec(l, 3, row, d), _mod_spec(l, 4, row, d), _mod_spec(l, 5, row, d),
                  pl.BlockSpec((1, d), lambda b, i: (0, 0)),
                  resident(wup), resident(cw), resident(cb), resident(wd)],
        out_specs=pl.BlockSpec((None, tm, d), lambda b, i: (b, i, 0)),
        scratch_shapes=[pltpu.VMEM((tm + 2 * FFN_HALO, d), bf16),
                        pltpu.VMEM((4, tm + 2 * FFN_HALO, 2 * FFN_CHUNK), f32),
                        pltpu.VMEM((tm, d), f32)],
        compiler_params=_cparams(("parallel", "parallel")),
        name="conv_ffn",
    )(x, x, x, modt, modt, modt, g.reshape(1, d), wup, cw, cb, wd)


def _rope_tables(n):
    t = jnp.arange(n)
    row = (t // GRID_W).astype(f32)
    colp = (t % GRID_W).astype(f32)
    half = HEAD_DIM // 2
    freq = ROPE_THETA ** (-jnp.arange(0, half, 2, dtype=f32) / half)
    ang_r = row[:, None] * freq[None, :]
    ang_c = colp[:, None] * freq[None, :]
    ang = jnp.concatenate([ang_r, ang_r, ang_c, ang_c] * 2, axis=-1)
    return jnp.cos(ang), jnp.sin(ang)


def _sink_column(sink, n_kv, tq):
    n_sub = sink.shape[0] // n_kv
    return jnp.repeat((sink * LOG2E).reshape(n_kv, n_sub), tq, axis=1)[..., None].astype(f32)


def kernel(x, c, ctx, c_ctx, ada_w, ada_b, norm_g, w_out, ffn_up, ffn_conv_w, ffn_conv_b, ffn_down, even_w_in,
           mlstm_gate_b, mlstm_head_g, swa_qk_g, swa_sink, odd_w_in, gqa_qk_g, na_qk_g, na_rpb):
    bsz, seq, d = x.shape
    n_ctx = ctx.shape[1]
    depth = ada_w.shape[0]
    a_w = A_HEADS * A_DIM
    cos, sin = _rope_tables(seq)
    ones_c, zeros_c = jnp.ones((n_ctx, LANES), f32), jnp.zeros((n_ctx, LANES), f32)

    cc = jnp.zeros((16, d), f32).at[:bsz].set(c).at[bsz].set(c_ctx)
    modt = ada_modulation(cc, ada_w, ada_b).reshape(depth, 16, 6, 1, d)

    x_lat, x_ctx = x, ctx
    tm_l, tm_c = 512, 256
    for l in range(depth):
        need_ctx = l < depth - 1
        w_o = w_out[l].astype(bf16)
        if l % 2 == 0:
            e = l // 2
            wi = even_w_in[e]
            sp = np.cumsum((0, a_w, a_w, a_w, a_w, 4 * A_HEADS, B_HEADS * HEAD_DIM, B_KV * HEAD_DIM, B_KV * HEAD_DIM))
            aq, ak, av, ao, ag, bq, bk, bv = (wi[:, sp[k]:sp[k + 1]] for k in range(8))
            w = jnp.concatenate([aq, ak, av, ao, ag, jnp.zeros((d, LANES - 4 * A_HEADS), f32), bq, bk, bv],
                                axis=1).astype(bf16)
            gate_col = 4 * a_w
            n_raw = gate_col + LANES
            gq, gk = swa_qk_g[e, 0], swa_qk_g[e, 1]
            cols = dict(n_raw=n_raw, q_cols=(n_raw,), kpair_cols=(n_raw + B_HEADS * HEAD_DIM,), kfull_cols=(),
                        vfull_cols=(), gains=(gq, gk))
            y_lat, qn_l, k2_l, v2_l = proj_prep(x_lat, modt, l, None, norm_g[l, 0], w, **cols, rope=(True, True),
                                                cos=cos, sin=sin, tm=tm_l)
            y_ctx, qn_c, k2_c, v2_c = proj_prep(x_ctx, modt, l, bsz, norm_g[l, 0], w, **cols, rope=(False, False),
                                                cos=ones_c, sin=zeros_c, tm=tm_c)
            a_ctx, a_lat = mlstm_mixer(y_ctx, y_lat, gate_col // LANES, mlstm_gate_b[e], mlstm_head_g[e])
            b_lat = window_attention(qn_l, k2_l, v2_l, k2_c, v2_c, _sink_column(swa_sink[e], B_KV, B_WIN))
            mix_l = (a_lat, b_lat)
            if need_ctx:
                b_ctx = gqa_attention(qn_c, [(k2_c, v2_c)], B_KV, 128, _sink_column(swa_sink[e], B_KV, 128))
                mix_c = (a_ctx, b_ctx)
        else:
            o = l // 2
            wi = odd_w_in[o]
            sp = np.cumsum((0, C_HEADS * HEAD_DIM, C_KV * HEAD_DIM, C_KV * HEAD_DIM) + (D_HEADS * HEAD_DIM,) * 3)
            cq, ck, cv, nq, nk, nv = (wi[:, sp[k]:sp[k + 1]] for k in range(6))
            w_ctx = jnp.concatenate([nk, nv, ck, cv], axis=1).astype(bf16)
            w = jnp.concatenate([cq.astype(bf16), nq.astype(bf16), w_ctx], axis=1)
            gq, gk = gqa_qk_g[o, 0], gqa_qk_g[o, 1]
            nq_g, nk_g = na_qk_g[o, 0], na_qk_g[o, 1]
            wide = 4 * LANES
            cqn, nqn, ck2, cv2, nkn, nvb = proj_prep(
                x_lat, modt, l, None, norm_g[l, 0], w, n_raw=0, q_cols=(0, wide), kpair_cols=(4 * wide,),
                kfull_cols=(2 * wide,), vfull_cols=(3 * wide,), gains=(gq, nq_g, gk, nk_g),
                rope=(True, False, True), cos=cos, sin=sin, tm=tm_l)
            if need_ctx:
                raise NotImplementedError("context outputs of an odd layer")
            ck2_c, cv2_c, nkn_c, nvb_c = proj_prep(
                x_ctx, modt, l, bsz, norm_g[l, 0], w_ctx, n_raw=0, q_cols=(), kpair_cols=(2 * wide,),
                kfull_cols=(0,), vfull_cols=(wide,), gains=(gk, nk_g), rope=(False,), cos=ones_c, sin=zeros_c,
                tm=tm_c)
            c_lat = gqa_attention(cqn, [(ck2_c, cv2_c), (ck2, cv2)], C_KV, 256)
            d_lat = neighbourhood_attention(nqn, nkn, nvb, nkn_c, nvb_c, na_bias_table(na_rpb[o], seq // GRID_W))
            mix_l = (c_lat, d_lat)

        ffn_w = ffn_weights(ffn_up[l], ffn_conv_w[l], ffn_conv_b[l], ffn_down[l])
        x_lat = out_proj(x_lat, mix_l[0], mix_l[1], w_o, modt, l, None, tm_l)
        x_lat = conv_ffn(x_lat, modt, l, None, norm_g[l, 1], ffn_w, tm=512)
        if need_ctx:
            x_ctx = out_proj(x_ctx, mix_c[0], mix_c[1], w_o, modt, l, bsz, tm_c)
            x_ctx = conv_ffn(x_ctx, modt, l, bsz, norm_g[l, 1], ffn_w, tm=256)
    return x_lat
```

```python
import functools

import jax
import jax.numpy as jnp
import numpy as np
from jax import lax
from jax.experimental import pallas as pl
from jax.experimental.pallas import tpu as pltpu

f32 = jnp.float32
bf16 = jnp.bfloat16

GRID_W = 64
HEAD_DIM = 64
LANES = 128
A_HEADS = 4
A_DIM = 128
B_HEADS = 8
B_KV = 2
B_WIN = 128
C_HEADS = 8
C_KV = 2
D_HEADS = 8
NA_ROWS = 8
NA_COLS = 16
NA_QROWS = 4
NA_PAIRS = 2
NA_KROWS = 12
ROPE_THETA = 10000.0
EPS = 1e-6
NEG = -1e30
LOG2E = 1.4426950408889634
VMEM_LIMIT = 56 * 1024 * 1024


def _cparams(sem):
    return pltpu.CompilerParams(dimension_semantics=sem, vmem_limit_bytes=VMEM_LIMIT)


def _dot(a, b):
    return jnp.dot(a, b, preferred_element_type=f32)


def _dot_nt(a, b):
    return lax.dot_general(a, b, (((1,), (1,)), ((), ())), preferred_element_type=f32)


def _modulated_norm(x, g, shift, scale):
    y = x * lax.rsqrt(jnp.mean(x * x, axis=-1, keepdims=True) + EPS)
    return (y * g) * (1.0 + scale) + shift


def _mod_spec(l, k, row, d):
    if row is None:
        return pl.BlockSpec((None, None, None, 1, d), lambda b, *_: (l, b, k, 0, 0))
    return pl.BlockSpec((None, None, None, 1, d), lambda b, *_: (l, row, k, 0, 0))


def _ada_kernel(c_ref, w_ref, b_ref, o_ref):
    c = c_ref[...]
    s = c / (1.0 + jnp.exp(-c))
    o_ref[...] = _dot(s.astype(bf16), w_ref[...].astype(bf16)) + b_ref[...]


def ada_modulation(cc, ada_w, ada_b):
    depth, d, n = ada_w.shape
    tn = 1536
    return pl.pallas_call(
        _ada_kernel,
        out_shape=jax.ShapeDtypeStruct((depth, cc.shape[0], n), f32),
        grid=(depth, n // tn),
        in_specs=[pl.BlockSpec(cc.shape, lambda l, j: (0, 0)),
                  pl.BlockSpec((None, d, tn), lambda l, j: (l, 0, j)),
                  pl.BlockSpec((None, 1, tn), lambda l, j: (l, 0, j))],
        out_specs=pl.BlockSpec((None, cc.shape[0], tn), lambda l, j: (l, 0, j)),
        compiler_params=_cparams(("parallel", "parallel")),
        name="ada",
    )(cc, ada_w, ada_b.reshape(depth, 1, n))


def _head_ms(a):
    ri = lax.broadcasted_iota(jnp.int32, (LANES, LANES), 0) // HEAD_DIM
    ci = lax.broadcasted_iota(jnp.int32, (LANES, LANES), 1) // HEAD_DIM
    bd = jnp.where(ri == ci, 1.0, 0.0).astype(bf16)
    ss = a * a
    hi = ss.astype(bf16)
    lo = (ss - hi.astype(f32)).astype(bf16)
    return (_dot(hi, bd) + _dot(lo, bd)) * (1.0 / HEAD_DIM)


def _head_norm(a, g):
    return (a * lax.rsqrt(_head_ms(a) + EPS)) * g


def _rope(a, cos, sin):
    lane = lax.broadcasted_iota(jnp.int32, a.shape, 1)
    quarter = HEAD_DIM // 4
    rot = jnp.where(lane % (2 * quarter) < quarter,
                    -pltpu.roll(a, LANES - quarter, axis=1), pltpu.roll(a, quarter, axis=1))
    return a * cos + rot * sin


def _dup_halves(a):
    lane = lax.broadcasted_iota(jnp.int32, a.shape, 1)
    sw = pltpu.roll(a, HEAD_DIM, axis=1)
    lo = lane < HEAD_DIM
    return jnp.where(lo, a, sw), jnp.where(lo, sw, a)


def _prep_kernel(*refs, n_q, n_kpair, n_kfull, n_vfull, rope, scale):
    it = iter(refs)
    q_refs = [next(it) for _ in range(n_q)]
    kp_refs = [next(it) for _ in range(n_kpair)]
    kf_refs = [next(it) for _ in range(n_kfull)]
    vf_refs = [next(it) for _ in range(n_vfull)]
    g_refs = [next(it) for _ in range(n_q + n_kpair + n_kfull)]
    cos_ref, sin_ref = next(it), next(it)
    qo_refs = [next(it) for _ in range(n_q)]
    kpo_refs = [(next(it), next(it)) for _ in range(n_kpair)]
    kfo_refs = [next(it) for _ in range(n_kfull)]
    vfo_refs = [next(it) for _ in range(n_vfull)]
    gi = iter(g_refs)
    for qi, (q_ref, qo_ref) in enumerate(zip(q_refs, qo_refs)):
        g = next(gi)[...]
        for p in range(q_ref.shape[-1] // LANES):
            a = _head_norm(q_ref[:, p * LANES:(p + 1) * LANES], g)
            if rope[qi]:
                a = _rope(a, cos_ref[...], sin_ref[...])
            qo_ref[:, p * LANES:(p + 1) * LANES] = (a * scale).astype(bf16)
    for ki, (kp_ref, (ko_ref, vo_ref)) in enumerate(zip(kp_refs, kpo_refs)):
        g = next(gi)[...]
        k = _head_norm(kp_ref[:, :LANES], g)
        if rope[n_q + ki]:
            k = _rope(k, cos_ref[...], sin_ref[...])
        k0, k1 = _dup_halves(k)
        ko_ref[0] = k0.astype(bf16)
        ko_ref[1] = k1.astype(bf16)
        v0, v1 = _dup_halves(kp_ref[:, LANES:])
        vo_ref[0] = v0.astype(bf16)
        vo_ref[1] = v1.astype(bf16)
    for kf_ref, kfo_ref in zip(kf_refs, kfo_refs):
        g = next(gi)[...]
        for p in range(kf_ref.shape[-1] // LANES):
            kfo_ref[:, p * LANES:(p + 1) * LANES] = _head_norm(kf_ref[:, p * LANES:(p + 1) * LANES], g).astype(bf16)
    for vf_ref, vfo_ref in zip(vf_refs, vfo_refs):
        vfo_ref[...] = vf_ref[...].astype(bf16)


def _proj_prep_kernel(x_ref, sh_ref, sc_ref, g_ref, w_ref, *rest, n_raw, has_t, q_cols, kpair_cols, kfull_cols,
                      vfull_cols, rope):
    if has_t:
        wt_ref, rest = rest[0], rest[1:]
    n_gain = len(q_cols) + len(kpair_cols) + len(kfull_cols)
    side = rest[:n_gain + 2]
    outs = rest[n_gain + 2:-1]
    y_scr = rest[-1]
    h = _modulated_norm(x_ref[...], g_ref[...], sh_ref[...], sc_ref[...]).astype(bf16)
    y_scr[:, n_raw:] = _dot(h, w_ref[:, n_raw:])
    if n_raw:
        outs[0][...] = _dot(h, w_ref[:, :n_raw])
        outs = outs[1:]
    if has_t:
        y_t = _dot_nt(wt_ref[...], h)
        for c in range(outs[0].shape[0]):
            outs[0][c] = y_t[:, c * LANES:(c + 1) * LANES]
        outs = outs[1:]
    wide, pair = 4 * LANES, 2 * LANES
    views = ([y_scr.at[:, c:c + wide] for c in q_cols] + [y_scr.at[:, c:c + pair] for c in kpair_cols]
             + [y_scr.at[:, c:c + wide] for c in kfull_cols + vfull_cols])
    _prep_kernel(*views, *side, *outs, n_q=len(q_cols), n_kpair=len(kpair_cols), n_kfull=len(kfull_cols),
                 n_vfull=len(vfull_cols), rope=rope, scale=HEAD_DIM ** -0.5 * LOG2E)


def proj_prep(x, modt, l, row, norm_g, w, *, n_raw, q_cols, kpair_cols, kfull_cols, vfull_cols, gains, rope, cos, sin,
              tm, w_t=None):
    bsz, t, d = x.shape
    n = w.shape[1]
    wide = 4 * LANES
    in_specs = [pl.BlockSpec((None, tm, d), lambda b, i: (b, i, 0)),
                _mod_spec(l, 0, row, d), _mod_spec(l, 1, row, d),
                pl.BlockSpec((1, d), lambda b, i: (0, 0)),
                pl.BlockSpec((d, n), lambda b, i: (0, 0))]
    args = [x, modt, modt, norm_g.reshape(1, d), w]
    if w_t is not None:
        in_specs.append(pl.BlockSpec(w_t.shape, lambda b, i: (0, 0)))
        args.append(w_t)
    for g in gains:
        in_specs.append(pl.BlockSpec((1, LANES), lambda b, i: (0, 0)))
        args.append(jnp.tile(g, 2).reshape(1, LANES))
    for tbl in (cos, sin):
        in_specs.append(pl.BlockSpec((tm, LANES), lambda b, i: (i, 0)))
        args.append(tbl)
    out_shape, out_specs = [], []
    if n_raw:
        out_shape.append(jax.ShapeDtypeStruct((bsz, t, n_raw), f32))
        out_specs.append(pl.BlockSpec((None, tm, n_raw), lambda b, i: (b, i, 0)))
    if w_t is not None:
        out_shape.append(jax.ShapeDtypeStruct((bsz, t // LANES, w_t.shape[0], LANES), f32))
        out_specs.append(pl.BlockSpec((None, tm // LANES, w_t.shape[0], LANES), lambda b, i: (b, i, 0, 0)))
    for _ in q_cols:
        out_shape.append(jax.ShapeDtypeStruct((bsz, t, wide), bf16))
        out_specs.append(pl.BlockSpec((None, tm, wide), lambda b, i: (b, i, 0)))
    for _ in kpair_cols:
        for _ in range(2):
            out_shape.append(jax.ShapeDtypeStruct((bsz, 2, t, LANES), bf16))
            out_specs.append(pl.BlockSpec((None, 2, tm, LANES), lambda b, i: (b, 0, i, 0)))
    for _ in kfull_cols + vfull_cols:
        out_shape.append(jax.ShapeDtypeStruct((bsz, t, wide), bf16))
        out_specs.append(pl.BlockSpec((None, tm, wide), lambda b, i: (b, i, 0)))
    kern = functools.partial(_proj_prep_kernel, n_raw=n_raw, has_t=w_t is not None, q_cols=q_cols,
                             kpair_cols=kpair_cols, kfull_cols=kfull_cols, vfull_cols=vfull_cols, rope=rope)
    return pl.pallas_call(
        kern, out_shape=out_shape, grid=(bsz, t // tm), in_specs=in_specs, out_specs=out_specs,
        scratch_shapes=[pltpu.VMEM((tm, n), f32)],
        compiler_params=_cparams(("parallel", "parallel")), name="proj_prep",
    )(*args)


def _stack_heads(q_ref, n_sub):
    parts = []
    for g in range(n_sub):
        blk = q_ref[:, (g // 2) * LANES:(g // 2 + 1) * LANES]
        lane = lax.broadcasted_iota(jnp.int32, blk.shape, 1)
        keep = (lane < HEAD_DIM) if g % 2 == 0 else (lane >= HEAD_DIM)
        parts.append(jnp.where(keep, blk, jnp.zeros_like(blk)))
    return jnp.concatenate(parts, axis=0)


def _scores(q, segs):
    scores = []
    for k, _, bias, mask in segs:
        s = _dot_nt(q, k)
        if bias is not None:
            s = s + bias
        if mask is not None:
            s = jnp.where(mask, s, NEG)
        scores.append(s)
    return scores


def _softmax_attend(q, segs, sink, scores=None):
    return _softmax_attend_chains([(q, segs, sink, scores)])[0]


def _softmax_attend_chains(chains):
    scores = [_scores(q, segs) if sc is None else sc for q, segs, _, sc in chains]
    maxes = [[s.max(axis=-1, keepdims=True) for s in sc] for sc in scores]
    m = []
    for (_, _, sink, _), mx in zip(chains, maxes):
        mi = functools.reduce(jnp.maximum, mx)
        m.append(mi if sink is None else jnp.maximum(mi, sink))
    p = [[jnp.exp2(s - m[i]) for s in sc] for i, sc in enumerate(scores)]
    outs = []
    for i, (_, segs, sink, _) in enumerate(chains):
        den = functools.reduce(jnp.add, [pj.sum(axis=-1, keepdims=True) for pj in p[i]])
        if sink is not None:
            den = den + jnp.exp2(sink - m[i])
        acc = functools.reduce(jnp.add, [_dot(pj.astype(bf16), seg[1]) for pj, seg in zip(p[i], segs)])
        outs.append(acc / den)
    return outs


def _unstack_heads(o, o_ref, n_sub, tq):
    lane = lax.broadcasted_iota(jnp.int32, (tq, LANES), 1)
    for p in range(n_sub // 2):
        even = o[(2 * p) * tq:(2 * p + 1) * tq]
        odd = o[(2 * p + 1) * tq:(2 * p + 2) * tq]
        o_ref[:, p * LANES:(p + 1) * LANES] = jnp.where(lane < HEAD_DIM, even, odd).astype(o_ref.dtype)


def _window_attn_kernel(q_ref, kc_ref, vc_ref, kp_ref, k0_ref, kn_ref, vp_ref, v0_ref, vn_ref, sink_ref, o_ref, *, tq):
    i = pl.program_id(2)
    nb = pl.num_programs(2)
    n_sub = B_HEADS // B_KV
    n_ctx = kc_ref.shape[0]
    rows, n_keys = n_sub * tq, n_ctx + 3 * tq
    qpos = lax.broadcasted_iota(jnp.int32, (rows, n_keys), 0) % tq
    kpos = lax.broadcasted_iota(jnp.int32, (rows, n_keys), 1) - n_ctx
    in_prev, in_next = (kpos >= 0) & (kpos < tq), kpos >= 2 * tq
    bad = (in_prev & (kpos < qpos)) | (in_next & (kpos - 2 * tq > qpos))
    k_blocks = [kp_ref[...], k0_ref[:tq], k0_ref[tq:], kn_ref[...]]
    v_blocks = [vp_ref[...], v0_ref[:tq], v0_ref[tq:], vn_ref[...]]
    chains = []
    for j, missing in enumerate((in_prev & (i == 0), in_next & (i == nb - 1))):
        k_all = jnp.concatenate([kc_ref[...]] + k_blocks[j:j + 3], axis=0)
        v_all = jnp.concatenate([vc_ref[...]] + v_blocks[j:j + 3], axis=0)
        q = _stack_heads(q_ref.at[j * tq:(j + 1) * tq], n_sub)
        chains.append((q, [(k_all, v_all, None, ~(bad | missing))], sink_ref[...], None))
    for j, o in enumerate(_softmax_attend_chains(chains)):
        _unstack_heads(o, o_ref.at[j * tq:(j + 1) * tq], n_sub, tq)


def window_attention(qn, k2, v2, kc2, vc2, sink_col):
    bsz, s, _ = qn.shape
    c = kc2.shape[2]
    tq = B_WIN
    nb = s // (2 * tq)
    n_half = s // tq
    n_sub = B_HEADS // B_KV
    wq = n_sub * HEAD_DIM
    half_spec = lambda fn: pl.BlockSpec((None, None, tq, LANES), fn)
    prev = lambda b, h, i: (b, h, jnp.maximum(2 * i - 1, 0), 0)
    nxt = lambda b, h, i: (b, h, jnp.minimum(2 * i + 2, n_half - 1), 0)
    cur_spec = pl.BlockSpec((None, None, 2 * tq, LANES), lambda b, h, i: (b, h, i, 0))
    ctx_spec = pl.BlockSpec((None, None, c, LANES), lambda b, h, i: (b, h, 0, 0))
    return pl.pallas_call(
        functools.partial(_window_attn_kernel, tq=tq),
        out_shape=jax.ShapeDtypeStruct((bsz, s, B_HEADS * HEAD_DIM), bf16),
        grid=(bsz, B_KV, nb),
        in_specs=[pl.BlockSpec((None, 2 * tq, wq), lambda b, h, i: (b, i, h)),
                  ctx_spec, ctx_spec,
                  half_spec(prev), cur_spec, half_spec(nxt),
                  half_spec(prev), cur_spec, half_spec(nxt),
                  pl.BlockSpec((None, n_sub * tq, 1), lambda b, h, i: (h, 0, 0))],
        out_specs=pl.BlockSpec((None, 2 * tq, wq), lambda b, h, i: (b, i, h)),
        compiler_params=_cparams(("parallel", "parallel", "parallel")),
        name="window_attn",
    )(qn, kc2, vc2, k2, k2, k2, v2, v2, v2, sink_col)


def _seg_attn_kernel(*refs, n_seg, n_sub, tq, has_sink):
    q_ref = refs[0]
    kv = refs[1:1 + 2 * n_seg]
    sink = refs[1 + 2 * n_seg][...] if has_sink else None
    o_ref = refs[-1]
    segs = [(kv[2 * j][...], kv[2 * j + 1][...], None, None) for j in range(n_seg)]
    n_pair = n_sub // 2
    lanes = [slice(p * LANES, (p + 1) * LANES) for p in range(n_pair)]
    sub = min(tq, 128)
    blocks = [(p, slice(r, r + sub)) for p in range(n_pair) for r in range(0, tq, sub)]
    chains = []
    for p, rows in blocks:
        sink_p = None
        if sink is not None:
            sink_p = jnp.concatenate([sink[(2 * p + g) * tq:(2 * p + g) * tq + sub] for g in range(2)], axis=0)
        chains.append((_stack_heads(q_ref.at[rows, lanes[p]], 2), segs, sink_p, None))
    for (p, rows), o in zip(blocks, _softmax_attend_chains(chains)):
        _unstack_heads(o, o_ref.at[rows, lanes[p]], 2, sub)


def gqa_attention(qn, kv_segs, n_kv, tq, sink_col=None):
    bsz, s, width = qn.shape
    n_sub = width // HEAD_DIM // n_kv
    wq = n_sub * HEAD_DIM
    in_specs = [pl.BlockSpec((None, tq, wq), lambda b, h, i: (b, i, h))]
    args = [qn]
    for k2, v2 in kv_segs:
        n = k2.shape[2]
        spec = pl.BlockSpec((None, None, n, LANES), lambda b, h, i: (b, h, 0, 0))
        in_specs += [spec, spec]
        args += [k2, v2]
    if sink_col is not None:
        in_specs.append(pl.BlockSpec((None, n_sub * tq, 1), lambda b, h, i: (h, 0, 0)))
        args.append(sink_col)
    return pl.pallas_call(
        functools.partial(_seg_attn_kernel, n_seg=len(kv_segs), n_sub=n_sub, tq=tq, has_sink=sink_col is not None),
        out_shape=jax.ShapeDtypeStruct((bsz, s, width), bf16),
        grid=(bsz, n_kv, s // tq),
        in_specs=in_specs,
        out_specs=pl.BlockSpec((None, tq, wq), lambda b, h, i: (b, i, h)),
        compiler_params=_cparams(("parallel", "parallel", "parallel")),
        name="gqa_attn",
    )(*args)


def _na_attn_kernel(q_ref, kc_ref, vc_ref, k_ref, v_ref, bias_ref, o_ref, *, tq, n_grp):
    r = pl.program_id(2)
    rows_total = k_ref.shape[0] // GRID_W
    start = jnp.clip(r * NA_QROWS - NA_ROWS // 2, 0, rows_total - NA_KROWS)
    off = pl.multiple_of(start * GRID_W, GRID_W)
    nk = NA_KROWS * GRID_W
    lane = lax.broadcasted_iota(jnp.int32, (tq, LANES), 1)
    chains = []
    for p in range(NA_PAIRS):
        lanes = slice(p * LANES, (p + 1) * LANES)
        blk = q_ref[:, lanes]
        k_nb, v_nb = k_ref[pl.ds(off, nk), lanes], v_ref[pl.ds(off, nk), lanes]
        for half in range(2):
            keep = (lane < HEAD_DIM) if half == 0 else (lane >= HEAD_DIM)
            segs = [(kc_ref[:, lanes], vc_ref[:, lanes], None, None),
                    (k_nb, v_nb, bias_ref[p, half * tq:(half + 1) * tq, :], None)]
            chains.append((jnp.where(keep, blk, jnp.zeros_like(blk)), segs, None, None))
    outs = _softmax_attend_chains(chains)
    for p in range(NA_PAIRS):
        o_ref[:, p * LANES:(p + 1) * LANES] = jnp.where(lane < HEAD_DIM, outs[2 * p], outs[2 * p + 1]).astype(o_ref.dtype)


def neighbourhood_attention(qn, kn, vb, kcn, vcb, bias_tbl):
    bsz, s, width = qn.shape
    c = kcn.shape[1]
    tq = NA_QROWS * GRID_W
    n_grp = s // tq
    wide = NA_PAIRS * LANES
    variant = lambda r: jnp.where(r == 0, 0, jnp.where(r == n_grp - 1, 2, 1))
    return pl.pallas_call(
        functools.partial(_na_attn_kernel, tq=tq, n_grp=n_grp),
        out_shape=jax.ShapeDtypeStruct((bsz, s, width), bf16),
        grid=(width // wide, bsz, n_grp),
        in_specs=[pl.BlockSpec((None, tq, wide), lambda p, b, r: (b, r, p)),
                  pl.BlockSpec((None, c, wide), lambda p, b, r: (b, 0, p)),
                  pl.BlockSpec((None, c, wide), lambda p, b, r: (b, 0, p)),
                  pl.BlockSpec((None, s, wide), lambda p, b, r: (b, 0, p)),
                  pl.BlockSpec((None, s, wide), lambda p, b, r: (b, 0, p)),
                  pl.BlockSpec((None, NA_PAIRS, 2 * tq, NA_KROWS * GRID_W), lambda p, b, r: (variant(r), p, 0, 0))],
        out_specs=pl.BlockSpec((None, tq, wide), lambda p, b, r: (b, r, p)),
        compiler_params=_cparams(("parallel", "parallel", "parallel")),
        name="na_attn",
    )(qn, kcn, vcb, kn, vb, bias_tbl)


def _na_bias_kernel(rp_ref, o_ref, *, rows_total):
    n_grp = rows_total // NA_QROWS
    qc = lax.broadcasted_iota(jnp.int32, (GRID_W, LANES), 0)
    lane = lax.broadcasted_iota(jnp.int32, (GRID_W, LANES), 1)
    kc = lane % GRID_W
    cs = jnp.clip(qc - NA_COLS // 2, 0, GRID_W - NA_COLS)
    col_ok = (kc >= cs) & (kc < cs + NA_COLS)
    left = lane < GRID_W
    neg = jnp.full((GRID_W, LANES), NEG, f32)
    for v, r0 in enumerate((0, NA_QROWS, (n_grp - 1) * NA_QROWS)):
        start = min(max(r0 - NA_ROWS // 2, 0), rows_total - NA_KROWS)
        for hh in range(2):
            for i in range(NA_QROWS):
                qr = r0 + i
                ws = min(max(qr - NA_ROWS // 2, 0), rows_total - NA_ROWS)
                for jj in range(NA_KROWS // 2):
                    kra, krb = start + 2 * jj, start + 2 * jj + 1
                    ok_a, ok_b = ws <= kra < ws + NA_ROWS, ws <= krb < ws + NA_ROWS
                    r_lo = hh * NA_QROWS * GRID_W + i * GRID_W
                    dst = (v, slice(r_lo, r_lo + GRID_W), slice(jj * LANES, (jj + 1) * LANES))
                    if not (ok_a or ok_b):
                        o_ref[dst] = neg
                        continue
                    dra = min(max(kra - qr + NA_ROWS - 1, 0), 2 * NA_ROWS - 2)
                    drb = min(max(krb - qr + NA_ROWS - 1, 0), 2 * NA_ROWS - 2)
                    row = jnp.where(left[:1], rp_ref[hh, dra:dra + 1, :], rp_ref[hh, drb:drb + 1, :])
                    toe = pltpu.roll(jnp.broadcast_to(row, (GRID_W, LANES)), LANES - (NA_COLS - 1), axis=1,
                                     stride=1, stride_axis=0)
                    ok = col_ok if (ok_a and ok_b) else (col_ok & left if ok_a else col_ok & ~left)
                    o_ref[dst] = jnp.where(ok, toe * LOG2E, neg)


def na_bias_table(rpb, rows_total):
    h, ndr, ndc = rpb.shape
    rp = jnp.zeros((h, 16, LANES), f32).at[:, :ndr, :ndc].set(rpb).at[:, :ndr, GRID_W:GRID_W + ndc].set(rpb)
    return pl.pallas_call(
        functools.partial(_na_bias_kernel, rows_total=rows_total),
        out_shape=jax.ShapeDtypeStruct((3, h // 2, 2 * NA_QROWS * GRID_W, NA_KROWS * GRID_W), f32),
        grid=(h // 2,),
        in_specs=[pl.BlockSpec((2, 16, LANES), lambda p: (p, 0, 0))],
        out_specs=pl.BlockSpec((3, None, 2 * NA_QROWS * GRID_W, NA_KROWS * GRID_W), lambda p: (0, p, 0, 0)),
        compiler_params=_cparams(("parallel",)),
        name="na_bias",
    )(rp)


def _log_sigmoid(x):
    return jnp.minimum(x, 0.0) - jnp.log1p(jnp.exp(-jnp.abs(x)))


MLSTM_L = 128
MLSTM_HP = 4


def _mlstm_chunks(cx_ref, m_ref, eye, chains):
    every = lambda fn: [fn(i, ch) for i, ch in enumerate(chains)]
    cx = every(lambda i, ch: cx_ref[ch["slot"]])
    m = every(lambda i, ch: m_ref[ch["slot"]])
    b_col = every(lambda i, ch: jnp.sum(jnp.where(ch["tri"], ch["lf"], 0.0), axis=1, keepdims=True))
    b_row = every(lambda i, ch: jnp.sum(jnp.where(eye, b_col[i], 0.0), axis=0, keepdims=True))
    b_last = every(lambda i, ch: jnp.sum(ch["lf"], axis=1, keepdims=True))
    dmat = every(lambda i, ch: jnp.where(ch["tri"], b_col[i] - b_row[i] + ch["li"], -jnp.inf))
    inter = every(lambda i, ch: b_col[i] + m[i])
    m_t = every(lambda i, ch: jnp.maximum(inter[i], jnp.max(dmat[i], axis=1, keepdims=True)))
    w = every(lambda i, ch: jnp.exp(dmat[i] - m_t[i]))
    a = every(lambda i, ch: jnp.exp(inter[i] - m_t[i]))
    vx = every(lambda i, ch: jnp.concatenate([ch["v"], jnp.ones_like(ch["v"])], axis=1))
    s = every(lambda i, ch: _dot(ch["q"], ch["k_t"].astype(bf16)) * w[i])
    qc = every(lambda i, ch: _dot(ch["q"], cx[i].astype(bf16)))
    nd = every(lambda i, ch: a[i] * qc[i] + _dot(s[i].astype(bf16), vx[i]))
    h = every(lambda i, ch: nd[i][:, :A_DIM] / jnp.maximum(jnp.abs(nd[i][:, A_DIM:]), jnp.exp(-m_t[i])))
    g_row = every(lambda i, ch: b_last[i] - b_row[i] + ch["li"])
    m_new = every(lambda i, ch: jnp.maximum(b_last[i] + m[i], jnp.max(g_row[i], axis=1, keepdims=True)))
    decay = every(lambda i, ch: jnp.exp(b_last[i] + m[i] - m_new[i]))
    wk_t = every(lambda i, ch: (ch["k_t"] * jnp.exp(g_row[i] - m_new[i])).astype(bf16))
    for i, ch in enumerate(chains):
        cx_ref[ch["slot"]] = decay[i] * cx[i] + _dot(wk_t[i], vx[i])
        m_ref[ch["slot"]] = m_new[i]
    return h


def _mlstm_kernel(gb_ref, qc_ref, kc_ref, vc_ref, oc_ref, gc_ref, ql_ref, kl_ref, vl_ref, ol_ref, gl_ref, hg_ref,
                  outc_ref, outl_ref, cx_scr, m_scr, g_scr, hc_scr, hl_scr):
    hg = pl.program_id(1)
    L = MLSTM_L
    cx_scr[...] = jnp.zeros_like(cx_scr)
    m_scr[...] = jnp.zeros_like(m_scr)
    kscale = A_DIM ** -0.5
    t_idx = lax.broadcasted_iota(jnp.int32, (L, L), 0)
    s_idx = lax.broadcasted_iota(jnp.int32, (L, L), 1)
    eye = s_idx == t_idx
    tri = (s_idx <= t_idx, s_idx >= t_idx)
    n_gates = 4 * A_HEADS

    def run(q_ref, k_ref, v_ref, g_ref, h_ref):
        nc = q_ref.shape[0] // L

        def body(j, accumulate):
            chains = []
            for d in range(2):
                cj = j if d == 0 else nc - 1 - j
                rows = pl.ds(pl.multiple_of(cj * L, L), L)
                g_scr[d] = g_ref[rows, :].T[:n_gates]
                for hh in range(MLSTM_HP):
                    hd = hg * MLSTM_HP + hh
                    ig, fg = (2 * d) * A_HEADS + hd, (2 * d + 1) * A_HEADS + hd
                    cols = slice(hh * A_DIM, (hh + 1) * A_DIM)
                    chains.append(dict(
                        slot=2 * hh + d, tri=tri[d], rows=rows, cols=cols,
                        li=g_scr[d, pl.ds(ig, 1), :] + gb_ref[ig],
                        lf=_log_sigmoid(g_scr[d, pl.ds(fg, 1), :] + gb_ref[fg]),
                        q=q_ref[rows, cols].astype(bf16), v=v_ref[rows, cols].astype(bf16),
                        k_t=k_ref[cj, cols, :] * kscale))
            half = len(chains) // 2
            hs = _mlstm_chunks(cx_scr, m_scr, eye, chains[:half]) + _mlstm_chunks(cx_scr, m_scr, eye, chains[half:])
            for ch, h in zip(chains, hs):
                if accumulate:
                    h_ref[ch["rows"], ch["cols"]] += h
                else:
                    h_ref[ch["rows"], ch["cols"]] = h

        lax.fori_loop(0, nc // 2, lambda j, c: (body(j, False), c)[1], 0)
        lax.fori_loop(nc // 2, nc, lambda j, c: (body(j, True), c)[1], 0)

    run(qc_ref, kc_ref, vc_ref, gc_ref, hc_scr)
    run(ql_ref, kl_ref, vl_ref, gl_ref, hl_scr)
    for o_ref, h_ref, out_ref in ((oc_ref, hc_scr, outc_ref), (ol_ref, hl_scr, outl_ref)):
        for hh in range(MLSTM_HP):
            cols = slice(hh * A_DIM, (hh + 1) * A_DIM)
            h = h_ref[:, cols]
            hn = (h * lax.rsqrt(jnp.mean(h * h, axis=-1, keepdims=True) + EPS)) * hg_ref[:, cols]
            o = o_ref[:, cols]
            out_ref[:, cols] = (hn / (1.0 + jnp.exp(-o))).astype(out_ref.dtype)


def mlstm_mixer(y_ctx, kt_ctx, y_lat, kt_lat, gate_block, gate_b, head_g):
    bsz, tc, _ = y_ctx.shape
    tl = y_lat.shape[1]
    wide = MLSTM_HP * A_DIM
    n_grp = A_HEADS // MLSTM_HP

    def col(t, base):
        if base is None:
            return pl.BlockSpec((None, t // LANES, wide, LANES), lambda b, h: (b, 0, h, 0))
        return pl.BlockSpec((None, t, wide), functools.partial(lambda b, h, base: (b, 0, base + h), base=base))

    def gate_spec(t):
        return pl.BlockSpec((None, t, LANES), lambda b, h: (b, 0, gate_block))

    def out_spec(t):
        return pl.BlockSpec((None, t, wide), lambda b, h: (b, 0, h))

    return pl.pallas_call(
        _mlstm_kernel,
        out_shape=[jax.ShapeDtypeStruct((bsz, tc, A_HEADS * A_DIM), bf16),
                   jax.ShapeDtypeStruct((bsz, tl, A_HEADS * A_DIM), bf16)],
        grid=(bsz, n_grp),
        in_specs=[pl.BlockSpec(memory_space=pltpu.SMEM)]
        + [col(tc, base) for base in (0, None, n_grp, 2 * n_grp)] + [gate_spec(tc)]
        + [col(tl, base) for base in (0, None, n_grp, 2 * n_grp)] + [gate_spec(tl)]
        + [pl.BlockSpec((1, wide), lambda b, h: (0, h))],
        out_specs=[out_spec(tc), out_spec(tl)],
        scratch_shapes=[pltpu.VMEM((2 * MLSTM_HP, A_DIM, 2 * A_DIM), f32), pltpu.VMEM((2 * MLSTM_HP, 1, 1), f32),
                        pltpu.VMEM((2, 4 * A_HEADS, LANES), f32),
                        pltpu.VMEM((tc, wide), f32), pltpu.VMEM((tl, wide), f32)],
        compiler_params=_cparams(("parallel", "parallel")),
        name="mlstm",
    )(gate_b, y_ctx, kt_ctx, y_ctx, y_ctx, y_ctx, y_lat, kt_lat, y_lat, y_lat, y_lat,
      head_g.reshape(1, A_HEADS * A_DIM))


def _outproj_kernel(x_ref, a1_ref, a2_ref, w1_ref, w2_ref, gate_ref, o_ref):
    y = _dot(a1_ref[...], w1_ref[...]) + _dot(a2_ref[...], w2_ref[...])
    o_ref[...] = x_ref[...] + gate_ref[...] * y


def out_proj(x, a1, a2, w, modt, l, row, tm):
    bsz, t, d = x.shape
    k1, k2 = a1.shape[-1], a2.shape[-1]
    return pl.pallas_call(
        _outproj_kernel,
        out_shape=jax.ShapeDtypeStruct((bsz, t, d), f32),
        grid=(bsz, t // tm),
        in_specs=[pl.BlockSpec((None, tm, d), lambda b, i: (b, i, 0)),
                  pl.BlockSpec((None, tm, k1), lambda b, i: (b, i, 0)),
                  pl.BlockSpec((None, tm, k2), lambda b, i: (b, i, 0)),
                  pl.BlockSpec((k1, d), lambda b, i: (0, 0)),
                  pl.BlockSpec((k2, d), lambda b, i: (0, 0)),
                  _mod_spec(l, 2, row, d)],
        out_specs=pl.BlockSpec((None, tm, d), lambda b, i: (b, i, 0)),
        compiler_params=_cparams(("parallel", "parallel")),
        name="out_proj",
    )(x, a1, a2, w[:k1], w[k1:], modt)


FFN_HALO = 8


FFN_CHUNK = 256


def _ffn_kernel(x_ref, xp_ref, xn_ref, sh_ref, sc_ref, gate_ref, g_ref, wup_ref, cw_ref, cb_ref, wd_ref,
                o_ref, h_scr, u_scr, acc_scr, *, tm):
    i = pl.program_id(1)
    g, sh, sc = g_ref[...], sh_ref[...], sc_ref[...]
    hp = _modulated_norm(xp_ref[...], g, sh, sc)
    hn = _modulated_norm(xn_ref[...], g, sh, sc)
    h_scr[:FFN_HALO] = jnp.where(i > 0, hp, 0.0).astype(bf16)
    h_scr[FFN_HALO:FFN_HALO + tm] = _modulated_norm(x_ref[...], g, sh, sc).astype(bf16)
    h_scr[FFN_HALO + tm:] = jnp.where(i < pl.num_programs(1) - 1, hn, 0.0).astype(bf16)
    dff = wd_ref.shape[0]
    n_chunks = dff // FFN_CHUNK

    def cols(ref, c):
        lo = c * FFN_CHUNK
        return ref[:, lo:lo + FFN_CHUNK], ref[:, dff + lo:dff + lo + FFN_CHUNK]

    half = tm // 2
    up_rows = (slice(0, half + 2 * FFN_HALO), slice(half + 2 * FFN_HALO, tm + 2 * FFN_HALO))

    n_buf = u_scr.shape[0]

    def up_proj(c, rows):
        wg, wv = cols(wup_ref, c)
        u_scr[c % n_buf, rows, :FFN_CHUNK] = _dot(h_scr[rows, :], wg)
        u_scr[c % n_buf, rows, FFN_CHUNK:] = _dot(h_scr[rows, :], wv)

    def gated(c, r):
        u = u_scr.at[c % n_buf]
        cw = jnp.concatenate(cols(cw_ref, c), axis=1)
        cb = jnp.concatenate(cols(cb_ref, c), axis=1)
        S = FFN_HALO
        ng = half // S
        u3 = u[r * half:r * half + half + 2 * S].reshape(ng + 2, S, 2 * FFN_CHUNK)
        sub = lax.broadcasted_iota(jnp.int32, (ng, S, 2 * FFN_CHUNK), 1)
        down = pltpu.roll(u3, 1, axis=1)
        up = pltpu.roll(u3, S - 1, axis=1)
        prev = jnp.where(sub == 0, down[:ng], down[1:ng + 1])
        nxt = jnp.where(sub == S - 1, up[2:], up[1:ng + 1])
        conv = (cw[0:1] * prev + cw[1:2] * u3[1:ng + 1] + cw[2:3] * nxt + cb).reshape(half, 2 * FFN_CHUNK)
        cg, cv = conv[:, :FFN_CHUNK], conv[:, FFN_CHUNK:]
        return ((cg / (1.0 + jnp.exp(-cg))) * cv).astype(bf16)

    def down_proj(act, c0, c1, r):
        part = _dot(act, wd_ref[c0 * FFN_CHUNK:c1 * FFN_CHUNK, :])
        out_rows = slice(r * half, (r + 1) * half)
        if c0 == 0:
            acc_scr[out_rows, :] = part
        else:
            acc_scr[out_rows, :] += part

    pairs = [tuple(range(c, min(c + 2, n_chunks))) for c in range(0, n_chunks, 2)]
    for c in pairs[0]:
        for rows in up_rows:
            up_proj(c, rows)
    pending = None
    for k, pair in enumerate(pairs):
        for r in range(2):
            if k + 1 < len(pairs):
                for c in pairs[k + 1]:
                    up_proj(c, up_rows[r])
            if pending is not None:
                down_proj(*pending)
            acts = [gated(c, r) for c in pair]
            pending = (acts[0] if len(acts) == 1 else jnp.concatenate(acts, axis=1), pair[0], pair[-1] + 1, r)
    down_proj(*pending)
    o_ref[...] = x_ref[...] + gate_ref[...] * acc_scr[...]


def ffn_weights(w_up, conv_w, conv_b, w_down):
    return w_up.astype(bf16), conv_w, conv_b.reshape(1, -1), w_down.astype(bf16)


def conv_ffn(x, modt, l, row, g, weights, tm):
    bsz, t, d = x.shape
    wup, cw, cb, wd = weights
    hb = tm // FFN_HALO
    last = t // FFN_HALO - 1

    def resident(a):
        return pl.BlockSpec(a.shape, lambda b, i: (0,) * a.ndim, pipeline_mode=pl.Buffered(1))

    return pl.pallas_call(
        functools.partial(_ffn_kernel, tm=tm),
        out_shape=jax.ShapeDtypeStruct((bsz, t, d), f32),
        grid=(bsz, t // tm),
        in_specs=[pl.BlockSpec((None, tm, d), lambda b, i: (b, i, 0)),
                  pl.BlockSpec((None, FFN_HALO, d), lambda b, i: (b, jnp.maximum(i * hb - 1, 0), 0)),
                  pl.BlockSpec((None, FFN_HALO, d), lambda b, i: (b, jnp.minimum((i + 1) * hb, last), 0)),
                  _mod_spec(l, 3, row, d), _mod_spec(l, 4, row, d), _mod_spec(l, 5, row, d),
                  pl.BlockSpec((1, d), lambda b, i: (0, 0)),
                  resident(wup), resident(cw), resident(cb), resident(wd)],
        out_specs=pl.BlockSpec((None, tm, d), lambda b, i: (b, i, 0)),
        scratch_shapes=[pltpu.VMEM((tm + 2 * FFN_HALO, d), bf16),
                        pltpu.VMEM((4, tm + 2 * FFN_HALO, 2 * FFN_CHUNK), f32),
                        pltpu.VMEM((tm, d), f32)],
        compiler_params=_cparams(("parallel", "parallel")),
        name="conv_ffn",
    )(x, x, x, modt, modt, modt, g.reshape(1, d), wup, cw, cb, wd)


def _rope_tables(n):
    t = jnp.arange(n)
    row = (t // GRID_W).astype(f32)
    colp = (t % GRID_W).astype(f32)
    half = HEAD_DIM // 2
    freq = ROPE_THETA ** (-jnp.arange(0, half, 2, dtype=f32) / half)
    ang_r = row[:, None] * freq[None, :]
    ang_c = colp[:, None] * freq[None, :]
    ang = jnp.concatenate([ang_r, ang_r, ang_c, ang_c] * 2, axis=-1)
    return jnp.cos(ang), jnp.sin(ang)


def _sink_column(sink, n_kv, tq):
    n_sub = sink.shape[0] // n_kv
    return jnp.repeat((sink * LOG2E).reshape(n_kv, n_sub), tq, axis=1)[..., None].astype(f32)


def kernel(x, c, ctx, c_ctx, ada_w, ada_b, norm_g, w_out, ffn_up, ffn_conv_w, ffn_conv_b, ffn_down, even_w_in,
           mlstm_gate_b, mlstm_head_g, swa_qk_g, swa_sink, odd_w_in, gqa_qk_g, na_qk_g, na_rpb):
    bsz, seq, d = x.shape
    n_ctx = ctx.shape[1]
    depth = ada_w.shape[0]
    a_w = A_HEADS * A_DIM
    cos, sin = _rope_tables(seq)
    ones_c, zeros_c = jnp.ones((n_ctx, LANES), f32), jnp.zeros((n_ctx, LANES), f32)

    cc = jnp.zeros((16, d), f32).at[:bsz].set(c).at[bsz].set(c_ctx)
    modt = ada_modulation(cc, ada_w, ada_b).reshape(depth, 16, 6, 1, d)

    x_lat, x_ctx = x, ctx
    tm_l, tm_c = 512, 256
    for l in range(depth):
        need_ctx = l < depth - 1
        w_o = w_out[l].astype(bf16)
        if l % 2 == 0:
            e = l // 2
            wi = even_w_in[e]
            sp = np.cumsum((0, a_w, a_w, a_w, a_w, 4 * A_HEADS, B_HEADS * HEAD_DIM, B_KV * HEAD_DIM, B_KV * HEAD_DIM))
            aq, ak, av, ao, ag, bq, bk, bv = (wi[:, sp[k]:sp[k + 1]] for k in range(8))
            w = jnp.concatenate([aq, av, ao, ag, jnp.zeros((d, LANES - 4 * A_HEADS), f32), bq, bk, bv],
                                axis=1).astype(bf16)
            gate_col = 3 * a_w
            n_raw = gate_col + LANES
            gq, gk = swa_qk_g[e, 0], swa_qk_g[e, 1]
            cols = dict(n_raw=n_raw, q_cols=(n_raw,), kpair_cols=(n_raw + B_HEADS * HEAD_DIM,), kfull_cols=(),
                        vfull_cols=(), gains=(gq, gk), w_t=ak.T.astype(bf16))
            y_lat, kt_lat, qn_l, k2_l, v2_l = proj_prep(x_lat, modt, l, None, norm_g[l, 0], w, **cols,
                                                        rope=(True, True), cos=cos, sin=sin, tm=tm_l)
            y_ctx, kt_ctx, qn_c, k2_c, v2_c = proj_prep(x_ctx, modt, l, bsz, norm_g[l, 0], w, **cols,
                                                        rope=(False, False), cos=ones_c, sin=zeros_c, tm=tm_c)
            a_ctx, a_lat = mlstm_mixer(y_ctx, kt_ctx, y_lat, kt_lat, gate_col // LANES, mlstm_gate_b[e],
                                       mlstm_head_g[e])
            b_lat = window_attention(qn_l, k2_l, v2_l, k2_c, v2_c, _sink_column(swa_sink[e], B_KV, B_WIN))
            mix_l = (a_lat, b_lat)
            if need_ctx:
                b_ctx = gqa_attention(qn_c, [(k2_c, v2_c)], B_KV, 128, _sink_column(swa_sink[e], B_KV, 128))
                mix_c = (a_ctx, b_ctx)
        else:
            o = l // 2
            wi = odd_w_in[o]
            sp = np.cumsum((0, C_HEADS * HEAD_DIM, C_KV * HEAD_DIM, C_KV * HEAD_DIM) + (D_HEADS * HEAD_DIM,) * 3)
            cq, ck, cv, nq, nk, nv = (wi[:, sp[k]:sp[k + 1]] for k in range(6))
            w_ctx = jnp.concatenate([nk, nv, ck, cv], axis=1).astype(bf16)
            w = jnp.concatenate([cq.astype(bf16), nq.astype(bf16), w_ctx], axis=1)
            gq, gk = gqa_qk_g[o, 0], gqa_qk_g[o, 1]
            nq_g, nk_g = na_qk_g[o, 0], na_qk_g[o, 1]
            wide = 4 * LANES
            cqn, nqn, ck2, cv2, nkn, nvb = proj_prep(
                x_lat, modt, l, None, norm_g[l, 0], w, n_raw=0, q_cols=(0, wide), kpair_cols=(4 * wide,),
                kfull_cols=(2 * wide,), vfull_cols=(3 * wide,), gains=(gq, nq_g, gk, nk_g),
                rope=(True, False, True), cos=cos, sin=sin, tm=tm_l)
            if need_ctx:
                raise NotImplementedError("context outputs of an odd layer")
            ck2_c, cv2_c, nkn_c, nvb_c = proj_prep(
                x_ctx, modt, l, bsz, norm_g[l, 0], w_ctx, n_raw=0, q_cols=(), kpair_cols=(2 * wide,),
                kfull_cols=(0,), vfull_cols=(wide,), gains=(gk, nk_g), rope=(False,), cos=ones_c, sin=zeros_c,
                tm=tm_c)
            c_lat = gqa_attention(cqn, [(ck2_c, cv2_c), (ck2, cv2)], C_KV, 256)
            d_lat = neighbourhood_attention(nqn, nkn, nvb, nkn_c, nvb_c, na_bias_table(na_rpb[o], seq // GRID_W))
            mix_l = (c_lat, d_lat)

        ffn_w = ffn_weights(ffn_up[l], ffn_conv_w[l], ffn_conv_b[l], ffn_down[l])
        x_lat = out_proj(x_lat, mix_l[0], mix_l[1], w_o, modt, l, None, tm_l)
        x_lat = conv_ffn(x_lat, modt, l, None, norm_g[l, 1], ffn_w, tm=512)
        if need_ctx:
            x_ctx = out_proj(x_ctx, mix_c[0], mix_c[1], w_o, modt, l, bsz, tm_c)
            x_ctx = conv_ffn(x_ctx, modt, l, bsz, norm_g[l, 1], ffn_w, tm=256)
    return x_lat
```

```python
import functools

import jax
import jax.numpy as jnp
import numpy as np
from jax import lax
from jax.experimental import pallas as pl
from jax.experimental.pallas import tpu as pltpu

f32 = jnp.float32
bf16 = jnp.bfloat16

GRID_W = 64
HEAD_DIM = 64
LANES = 128
A_HEADS = 4
A_DIM = 128
B_HEADS = 8
B_KV = 2
B_WIN = 128
C_HEADS = 8
C_KV = 2
D_HEADS = 8
NA_ROWS = 8
NA_COLS = 16
NA_QROWS = 4
WIN_BLOCKS = 8
NA_PAIRS = 4
NA_KROWS = 12
ROPE_THETA = 10000.0
EPS = 1e-6
NEG = -1e30
LOG2E = 1.4426950408889634
VMEM_LIMIT = 56 * 1024 * 1024


def _cparams(sem):
    return pltpu.CompilerParams(dimension_semantics=sem, vmem_limit_bytes=VMEM_LIMIT)


def _dot(a, b):
    return jnp.dot(a, b, preferred_element_type=f32)


def _dot_nt(a, b):
    return lax.dot_general(a, b, (((1,), (1,)), ((), ())), preferred_element_type=f32)


def _modulated_norm(x, g, shift, scale):
    y = x * lax.rsqrt(jnp.mean(x * x, axis=-1, keepdims=True) + EPS)
    return (y * g) * (1.0 + scale) + shift


def _mod_spec(l, k, row, d):
    if row is None:
        return pl.BlockSpec((None, None, None, 1, d), lambda b, *_: (l, b, k, 0, 0))
    return pl.BlockSpec((None, None, None, 1, d), lambda b, *_: (l, row, k, 0, 0))


def _ada_kernel(c_ref, w_ref, b_ref, o_ref):
    c = c_ref[...]
    s = c / (1.0 + jnp.exp(-c))
    o_ref[...] = _dot(s.astype(bf16), w_ref[...].astype(bf16)) + b_ref[...]


def ada_modulation(cc, ada_w, ada_b):
    depth, d, n = ada_w.shape
    tn = 1536
    return pl.pallas_call(
        _ada_kernel,
        out_shape=jax.ShapeDtypeStruct((depth, cc.shape[0], n), f32),
        grid=(depth, n // tn),
        in_specs=[pl.BlockSpec(cc.shape, lambda l, j: (0, 0)),
                  pl.BlockSpec((None, d, tn), lambda l, j: (l, 0, j)),
                  pl.BlockSpec((None, 1, tn), lambda l, j: (l, 0, j))],
        out_specs=pl.BlockSpec((None, cc.shape[0], tn), lambda l, j: (l, 0, j)),
        compiler_params=_cparams(("parallel", "parallel")),
        name="ada",
    )(cc, ada_w, ada_b.reshape(depth, 1, n))


def _head_ms(a):
    ri = lax.broadcasted_iota(jnp.int32, (LANES, LANES), 0) // HEAD_DIM
    ci = lax.broadcasted_iota(jnp.int32, (LANES, LANES), 1) // HEAD_DIM
    bd = jnp.where(ri == ci, 1.0, 0.0).astype(bf16)
    ss = a * a
    hi = ss.astype(bf16)
    lo = (ss - hi.astype(f32)).astype(bf16)
    return (_dot(hi, bd) + _dot(lo, bd)) * (1.0 / HEAD_DIM)


def _head_norm(a, g):
    return (a * lax.rsqrt(_head_ms(a) + EPS)) * g


def _rope(a, cos, sin):
    lane = lax.broadcasted_iota(jnp.int32, a.shape, 1)
    quarter = HEAD_DIM // 4
    rot = jnp.where(lane % (2 * quarter) < quarter,
                    -pltpu.roll(a, LANES - quarter, axis=1), pltpu.roll(a, quarter, axis=1))
    return a * cos + rot * sin


def _dup_halves(a):
    lane = lax.broadcasted_iota(jnp.int32, a.shape, 1)
    sw = pltpu.roll(a, HEAD_DIM, axis=1)
    lo = lane < HEAD_DIM
    return jnp.where(lo, a, sw), jnp.where(lo, sw, a)


def _prep_kernel(*refs, n_q, n_kpair, n_kfull, n_vfull, rope, scale):
    it = iter(refs)
    q_refs = [next(it) for _ in range(n_q)]
    kp_refs = [next(it) for _ in range(n_kpair)]
    kf_refs = [next(it) for _ in range(n_kfull)]
    vf_refs = [next(it) for _ in range(n_vfull)]
    g_refs = [next(it) for _ in range(n_q + n_kpair + n_kfull)]
    cos_ref, sin_ref = next(it), next(it)
    qo_refs = [next(it) for _ in range(n_q)]
    kpo_refs = [(next(it), next(it)) for _ in range(n_kpair)]
    kfo_refs = [next(it) for _ in range(n_kfull)]
    vfo_refs = [next(it) for _ in range(n_vfull)]
    gi = iter(g_refs)
    for qi, (q_ref, qo_ref) in enumerate(zip(q_refs, qo_refs)):
        g = next(gi)[...]
        for p in range(q_ref.shape[-1] // LANES):
            a = _head_norm(q_ref[:, p * LANES:(p + 1) * LANES], g)
            if rope[qi]:
                a = _rope(a, cos_ref[...], sin_ref[...])
            qo_ref[:, p * LANES:(p + 1) * LANES] = (a * scale).astype(bf16)
    for ki, (kp_ref, (ko_ref, vo_ref)) in enumerate(zip(kp_refs, kpo_refs)):
        g = next(gi)[...]
        k = _head_norm(kp_ref[:, :LANES], g)
        if rope[n_q + ki]:
            k = _rope(k, cos_ref[...], sin_ref[...])
        k0, k1 = _dup_halves(k)
        ko_ref[0] = k0.astype(bf16)
        ko_ref[1] = k1.astype(bf16)
        v0, v1 = _dup_halves(kp_ref[:, LANES:])
        vo_ref[0] = v0.astype(bf16)
        vo_ref[1] = v1.astype(bf16)
    for kf_ref, kfo_ref in zip(kf_refs, kfo_refs):
        g = next(gi)[...]
        for p in range(kf_ref.shape[-1] // LANES):
            kfo_ref[:, p * LANES:(p + 1) * LANES] = _head_norm(kf_ref[:, p * LANES:(p + 1) * LANES], g).astype(bf16)
    for vf_ref, vfo_ref in zip(vf_refs, vfo_refs):
        vfo_ref[...] = vf_ref[...].astype(bf16)


def _proj_prep_kernel(x_ref, sh_ref, sc_ref, g_ref, w_ref, *rest, n_raw, has_t, q_cols, kpair_cols, kfull_cols,
                      vfull_cols, rope):
    if has_t:
        wt_ref, rest = rest[0], rest[1:]
    n_gain = len(q_cols) + len(kpair_cols) + len(kfull_cols)
    side = rest[:n_gain + 2]
    outs = rest[n_gain + 2:-1]
    y_scr = rest[-1]
    h = _modulated_norm(x_ref[...], g_ref[...], sh_ref[...], sc_ref[...]).astype(bf16)
    y_scr[:, n_raw:] = _dot(h, w_ref[:, n_raw:])
    if n_raw:
        outs[0][...] = _dot(h, w_ref[:, :n_raw])
        outs = outs[1:]
    if has_t:
        y_t = _dot_nt(wt_ref[...], h)
        for c in range(outs[0].shape[0]):
            outs[0][c] = y_t[:, c * LANES:(c + 1) * LANES]
        outs = outs[1:]
    wide, pair = 4 * LANES, 2 * LANES
    views = ([y_scr.at[:, c:c + wide] for c in q_cols] + [y_scr.at[:, c:c + pair] for c in kpair_cols]
             + [y_scr.at[:, c:c + wide] for c in kfull_cols + vfull_cols])
    _prep_kernel(*views, *side, *outs, n_q=len(q_cols), n_kpair=len(kpair_cols), n_kfull=len(kfull_cols),
                 n_vfull=len(vfull_cols), rope=rope, scale=HEAD_DIM ** -0.5 * LOG2E)


def proj_prep(x, modt, l, row, norm_g, w, *, n_raw, q_cols, kpair_cols, kfull_cols, vfull_cols, gains, rope, cos, sin,
              tm, w_t=None):
    bsz, t, d = x.shape
    n = w.shape[1]
    wide = 4 * LANES
    in_specs = [pl.BlockSpec((None, tm, d), lambda b, i: (b, i, 0)),
                _mod_spec(l, 0, row, d), _mod_spec(l, 1, row, d),
                pl.BlockSpec((1, d), lambda b, i: (0, 0)),
                pl.BlockSpec((d, n), lambda b, i: (0, 0))]
    args = [x, modt, modt, norm_g.reshape(1, d), w]
    if w_t is not None:
        in_specs.append(pl.BlockSpec(w_t.shape, lambda b, i: (0, 0)))
        args.append(w_t)
    for g in gains:
        in_specs.append(pl.BlockSpec((1, LANES), lambda b, i: (0, 0)))
        args.append(jnp.tile(g, 2).reshape(1, LANES))
    for tbl in (cos, sin):
        in_specs.append(pl.BlockSpec((tm, LANES), lambda b, i: (i, 0)))
        args.append(tbl)
    out_shape, out_specs = [], []
    if n_raw:
        out_shape.append(jax.ShapeDtypeStruct((bsz, t, n_raw), f32))
        out_specs.append(pl.BlockSpec((None, tm, n_raw), lambda b, i: (b, i, 0)))
    if w_t is not None:
        out_shape.append(jax.ShapeDtypeStruct((bsz, t // LANES, w_t.shape[0], LANES), f32))
        out_specs.append(pl.BlockSpec((None, tm // LANES, w_t.shape[0], LANES), lambda b, i: (b, i, 0, 0)))
    for _ in q_cols:
        out_shape.append(jax.ShapeDtypeStruct((bsz, t, wide), bf16))
        out_specs.append(pl.BlockSpec((None, tm, wide), lambda b, i: (b, i, 0)))
    for _ in kpair_cols:
        for _ in range(2):
            out_shape.append(jax.ShapeDtypeStruct((bsz, 2, t, LANES), bf16))
            out_specs.append(pl.BlockSpec((None, 2, tm, LANES), lambda b, i: (b, 0, i, 0)))
    for _ in kfull_cols + vfull_cols:
        out_shape.append(jax.ShapeDtypeStruct((bsz, t, wide), bf16))
        out_specs.append(pl.BlockSpec((None, tm, wide), lambda b, i: (b, i, 0)))
    kern = functools.partial(_proj_prep_kernel, n_raw=n_raw, has_t=w_t is not None, q_cols=q_cols,
                             kpair_cols=kpair_cols, kfull_cols=kfull_cols, vfull_cols=vfull_cols, rope=rope)
    return pl.pallas_call(
        kern, out_shape=out_shape, grid=(bsz, t // tm), in_specs=in_specs, out_specs=out_specs,
        scratch_shapes=[pltpu.VMEM((tm, n), f32)],
        compiler_params=_cparams(("parallel", "parallel")), name="proj_prep",
    )(*args)


def _stack_heads(q_ref, n_sub):
    parts = []
    for g in range(n_sub):
        blk = q_ref[:, (g // 2) * LANES:(g // 2 + 1) * LANES]
        lane = lax.broadcasted_iota(jnp.int32, blk.shape, 1)
        keep = (lane < HEAD_DIM) if g % 2 == 0 else (lane >= HEAD_DIM)
        parts.append(jnp.where(keep, blk, jnp.zeros_like(blk)))
    return jnp.concatenate(parts, axis=0)


def _scores(q, segs):
    scores = []
    for k, _, bias, mask in segs:
        s = _dot_nt(q, k)
        if bias is not None:
            s = s + bias
        if mask is not None:
            s = jnp.where(mask, s, NEG)
        scores.append(s)
    return scores


def _softmax_attend(q, segs, sink, scores=None):
    return _softmax_attend_chains([(q, segs, sink, scores)])[0]


def _softmax_attend_chains(chains):
    scores = [_scores(q, segs) if sc is None else sc for q, segs, _, sc in chains]
    maxes = [[s.max(axis=-1, keepdims=True) for s in sc] for sc in scores]
    m = []
    for (_, _, sink, _), mx in zip(chains, maxes):
        mi = functools.reduce(jnp.maximum, mx)
        m.append(mi if sink is None else jnp.maximum(mi, sink))
    p = [[jnp.exp2(s - m[i]) for s in sc] for i, sc in enumerate(scores)]
    outs = []
    for i, (_, segs, sink, _) in enumerate(chains):
        den = functools.reduce(jnp.add, [pj.sum(axis=-1, keepdims=True) for pj in p[i]])
        if sink is not None:
            den = den + jnp.exp2(sink - m[i])
        acc = functools.reduce(jnp.add, [_dot(pj.astype(bf16), seg[1]) for pj, seg in zip(p[i], segs)])
        outs.append(acc / den)
    return outs


def _unstack_heads(o, o_ref, n_sub, tq):
    lane = lax.broadcasted_iota(jnp.int32, (tq, LANES), 1)
    for p in range(n_sub // 2):
        even = o[(2 * p) * tq:(2 * p + 1) * tq]
        odd = o[(2 * p + 1) * tq:(2 * p + 2) * tq]
        o_ref[:, p * LANES:(p + 1) * LANES] = jnp.where(lane < HEAD_DIM, even, odd).astype(o_ref.dtype)


def _window_attn_kernel(q_ref, kc_ref, vc_ref, kp_ref, k0_ref, kn_ref, vp_ref, v0_ref, vn_ref, sink_ref, o_ref, *, tq):
    i = pl.program_id(2)
    nb = pl.num_programs(2)
    n_sub = B_HEADS // B_KV
    n_ctx = kc_ref.shape[0]
    rows, n_keys = n_sub * tq, n_ctx + 3 * tq
    qpos = lax.broadcasted_iota(jnp.int32, (rows, n_keys), 0) % tq
    kpos = lax.broadcasted_iota(jnp.int32, (rows, n_keys), 1) - n_ctx
    in_prev, in_next = (kpos >= 0) & (kpos < tq), kpos >= 2 * tq
    bad = (in_prev & (kpos < qpos)) | (in_next & (kpos - 2 * tq > qpos))
    own = [slice(j * tq, (j + 1) * tq) for j in range(WIN_BLOCKS)]
    k_blocks = [kp_ref[...]] + [k0_ref[r] for r in own] + [kn_ref[...]]
    v_blocks = [vp_ref[...]] + [v0_ref[r] for r in own] + [vn_ref[...]]
    chains = []
    for j in range(WIN_BLOCKS):
        invalid = bad
        if j == 0:
            invalid = invalid | (in_prev & (i == 0))
        if j == WIN_BLOCKS - 1:
            invalid = invalid | (in_next & (i == nb - 1))
        k_all = jnp.concatenate([kc_ref[...]] + k_blocks[j:j + 3], axis=0)
        v_all = jnp.concatenate([vc_ref[...]] + v_blocks[j:j + 3], axis=0)
        chains.append((_stack_heads(q_ref.at[own[j]], n_sub), [(k_all, v_all, None, ~invalid)], sink_ref[...], None))
    for j, o in enumerate(_softmax_attend_chains(chains)):
        _unstack_heads(o, o_ref.at[own[j]], n_sub, tq)


def window_attention(qn, k2, v2, kc2, vc2, sink_col):
    bsz, s, _ = qn.shape
    c = kc2.shape[2]
    tq = B_WIN
    step = WIN_BLOCKS * tq
    nb = s // step
    n_half = s // tq
    n_sub = B_HEADS // B_KV
    wq = n_sub * HEAD_DIM
    half_spec = lambda fn: pl.BlockSpec((None, None, tq, LANES), fn)
    prev = lambda b, h, i: (b, h, jnp.maximum(WIN_BLOCKS * i - 1, 0), 0)
    nxt = lambda b, h, i: (b, h, jnp.minimum(WIN_BLOCKS * (i + 1), n_half - 1), 0)
    cur_spec = pl.BlockSpec((None, None, step, LANES), lambda b, h, i: (b, h, i, 0))
    ctx_spec = pl.BlockSpec((None, None, c, LANES), lambda b, h, i: (b, h, 0, 0))
    return pl.pallas_call(
        functools.partial(_window_attn_kernel, tq=tq),
        out_shape=jax.ShapeDtypeStruct((bsz, s, B_HEADS * HEAD_DIM), bf16),
        grid=(bsz, B_KV, nb),
        in_specs=[pl.BlockSpec((None, step, wq), lambda b, h, i: (b, i, h)),
                  ctx_spec, ctx_spec,
                  half_spec(prev), cur_spec, half_spec(nxt),
                  half_spec(prev), cur_spec, half_spec(nxt),
                  pl.BlockSpec((None, n_sub * tq, 1), lambda b, h, i: (h, 0, 0))],
        out_specs=pl.BlockSpec((None, step, wq), lambda b, h, i: (b, i, h)),
        compiler_params=_cparams(("parallel", "parallel", "parallel")),
        name="window_attn",
    )(qn, kc2, vc2, k2, k2, k2, v2, v2, v2, sink_col)


def _seg_attn_kernel(*refs, n_seg, n_sub, tq, has_sink):
    q_ref = refs[0]
    kv = refs[1:1 + 2 * n_seg]
    sink = refs[1 + 2 * n_seg][...] if has_sink else None
    o_ref = refs[-1]
    segs = [(kv[2 * j][...], kv[2 * j + 1][...], None, None) for j in range(n_seg)]
    n_pair = n_sub // 2
    lanes = [slice(p * LANES, (p + 1) * LANES) for p in range(n_pair)]
    sub = min(tq, 128)
    blocks = [(p, slice(r, r + sub)) for p in range(n_pair) for r in range(0, tq, sub)]
    chains = []
    for p, rows in blocks:
        sink_p = None
        if sink is not None:
            sink_p = jnp.concatenate([sink[(2 * p + g) * tq:(2 * p + g) * tq + sub] for g in range(2)], axis=0)
        chains.append((_stack_heads(q_ref.at[rows, lanes[p]], 2), segs, sink_p, None))
    for (p, rows), o in zip(blocks, _softmax_attend_chains(chains)):
        _unstack_heads(o, o_ref.at[rows, lanes[p]], 2, sub)


def gqa_attention(qn, kv_segs, n_kv, tq, sink_col=None):
    bsz, s, width = qn.shape
    n_sub = width // HEAD_DIM // n_kv
    wq = n_sub * HEAD_DIM
    in_specs = [pl.BlockSpec((None, tq, wq), lambda b, h, i: (b, i, h))]
    args = [qn]
    for k2, v2 in kv_segs:
        n = k2.shape[2]
        spec = pl.BlockSpec((None, None, n, LANES), lambda b, h, i: (b, h, 0, 0))
        in_specs += [spec, spec]
        args += [k2, v2]
    if sink_col is not None:
        in_specs.append(pl.BlockSpec((None, n_sub * tq, 1), lambda b, h, i: (h, 0, 0)))
        args.append(sink_col)
    return pl.pallas_call(
        functools.partial(_seg_attn_kernel, n_seg=len(kv_segs), n_sub=n_sub, tq=tq, has_sink=sink_col is not None),
        out_shape=jax.ShapeDtypeStruct((bsz, s, width), bf16),
        grid=(bsz, n_kv, s // tq),
        in_specs=in_specs,
        out_specs=pl.BlockSpec((None, tq, wq), lambda b, h, i: (b, i, h)),
        compiler_params=_cparams(("parallel", "parallel", "parallel")),
        name="gqa_attn",
    )(*args)


def _na_attn_kernel(q_ref, kc_ref, vc_ref, k_ref, v_ref, bias_ref, o_ref, *, tq, n_grp):
    r = pl.program_id(2)
    rows_total = k_ref.shape[0] // GRID_W
    start = jnp.clip(r * NA_QROWS - NA_ROWS // 2, 0, rows_total - NA_KROWS)
    off = pl.multiple_of(start * GRID_W, GRID_W)
    nk = NA_KROWS * GRID_W
    lane = lax.broadcasted_iota(jnp.int32, (tq, LANES), 1)
    chains = []
    for p in range(NA_PAIRS):
        lanes = slice(p * LANES, (p + 1) * LANES)
        blk = q_ref[:, lanes]
        k_nb, v_nb = k_ref[pl.ds(off, nk), lanes], v_ref[pl.ds(off, nk), lanes]
        for half in range(2):
            keep = (lane < HEAD_DIM) if half == 0 else (lane >= HEAD_DIM)
            segs = [(kc_ref[:, lanes], vc_ref[:, lanes], None, None),
                    (k_nb, v_nb, bias_ref[p, half * tq:(half + 1) * tq, :], None)]
            chains.append((jnp.where(keep, blk, jnp.zeros_like(blk)), segs, None, None))
    outs = _softmax_attend_chains(chains)
    for p in range(NA_PAIRS):
        o_ref[:, p * LANES:(p + 1) * LANES] = jnp.where(lane < HEAD_DIM, outs[2 * p], outs[2 * p + 1]).astype(o_ref.dtype)


def neighbourhood_attention(qn, kn, vb, kcn, vcb, bias_tbl):
    bsz, s, width = qn.shape
    c = kcn.shape[1]
    tq = NA_QROWS * GRID_W
    n_grp = s // tq
    wide = NA_PAIRS * LANES
    variant = lambda r: jnp.where(r == 0, 0, jnp.where(r == n_grp - 1, 2, 1))
    return pl.pallas_call(
        functools.partial(_na_attn_kernel, tq=tq, n_grp=n_grp),
        out_shape=jax.ShapeDtypeStruct((bsz, s, width), bf16),
        grid=(width // wide, bsz, n_grp),
        in_specs=[pl.BlockSpec((None, tq, wide), lambda p, b, r: (b, r, p)),
                  pl.BlockSpec((None, c, wide), lambda p, b, r: (b, 0, p)),
                  pl.BlockSpec((None, c, wide), lambda p, b, r: (b, 0, p)),
                  pl.BlockSpec((None, s, wide), lambda p, b, r: (b, 0, p)),
                  pl.BlockSpec((None, s, wide), lambda p, b, r: (b, 0, p)),
                  pl.BlockSpec((None, NA_PAIRS, 2 * tq, NA_KROWS * GRID_W), lambda p, b, r: (variant(r), p, 0, 0))],
        out_specs=pl.BlockSpec((None, tq, wide), lambda p, b, r: (b, r, p)),
        compiler_params=_cparams(("parallel", "parallel", "parallel")),
        name="na_attn",
    )(qn, kcn, vcb, kn, vb, bias_tbl)


def _na_bias_kernel(rp_ref, o_ref, *, rows_total):
    n_grp = rows_total // NA_QROWS
    qc = lax.broadcasted_iota(jnp.int32, (GRID_W, LANES), 0)
    lane = lax.broadcasted_iota(jnp.int32, (GRID_W, LANES), 1)
    kc = lane % GRID_W
    cs = jnp.clip(qc - NA_COLS // 2, 0, GRID_W - NA_COLS)
    col_ok = (kc >= cs) & (kc < cs + NA_COLS)
    left = lane < GRID_W
    neg = jnp.full((GRID_W, LANES), NEG, f32)
    for v, r0 in enumerate((0, NA_QROWS, (n_grp - 1) * NA_QROWS)):
        start = min(max(r0 - NA_ROWS // 2, 0), rows_total - NA_KROWS)
        for hh in range(2):
            for i in range(NA_QROWS):
                qr = r0 + i
                ws = min(max(qr - NA_ROWS // 2, 0), rows_total - NA_ROWS)
                for jj in range(NA_KROWS // 2):
                    kra, krb = start + 2 * jj, start + 2 * jj + 1
                    ok_a, ok_b = ws <= kra < ws + NA_ROWS, ws <= krb < ws + NA_ROWS
                    r_lo = hh * NA_QROWS * GRID_W + i * GRID_W
                    dst = (v, slice(r_lo, r_lo + GRID_W), slice(jj * LANES, (jj + 1) * LANES))
                    if not (ok_a or ok_b):
                        o_ref[dst] = neg
                        continue
                    dra = min(max(kra - qr + NA_ROWS - 1, 0), 2 * NA_ROWS - 2)
                    drb = min(max(krb - qr + NA_ROWS - 1, 0), 2 * NA_ROWS - 2)
                    row = jnp.where(left[:1], rp_ref[hh, dra:dra + 1, :], rp_ref[hh, drb:drb + 1, :])
                    toe = pltpu.roll(jnp.broadcast_to(row, (GRID_W, LANES)), LANES - (NA_COLS - 1), axis=1,
                                     stride=1, stride_axis=0)
                    ok = col_ok if (ok_a and ok_b) else (col_ok & left if ok_a else col_ok & ~left)
                    o_ref[dst] = jnp.where(ok, toe * LOG2E, neg)


def na_bias_table(rpb, rows_total):
    h, ndr, ndc = rpb.shape
    rp = jnp.zeros((h, 16, LANES), f32).at[:, :ndr, :ndc].set(rpb).at[:, :ndr, GRID_W:GRID_W + ndc].set(rpb)
    return pl.pallas_call(
        functools.partial(_na_bias_kernel, rows_total=rows_total),
        out_shape=jax.ShapeDtypeStruct((3, h // 2, 2 * NA_QROWS * GRID_W, NA_KROWS * GRID_W), f32),
        grid=(h // 2,),
        in_specs=[pl.BlockSpec((2, 16, LANES), lambda p: (p, 0, 0))],
        out_specs=pl.BlockSpec((3, None, 2 * NA_QROWS * GRID_W, NA_KROWS * GRID_W), lambda p: (0, p, 0, 0)),
        compiler_params=_cparams(("parallel",)),
        name="na_bias",
    )(rp)


def _log_sigmoid(x):
    return jnp.minimum(x, 0.0) - jnp.log1p(jnp.exp(-jnp.abs(x)))


MLSTM_L = 128
MLSTM_HP = 4


def _mlstm_chunks(cx_ref, m_ref, eye, chains):
    every = lambda fn: [fn(i, ch) for i, ch in enumerate(chains)]
    cx = every(lambda i, ch: cx_ref[ch["slot"]])
    m = every(lambda i, ch: m_ref[ch["slot"]])
    b_col = every(lambda i, ch: jnp.sum(jnp.where(ch["tri"], ch["lf"], 0.0), axis=1, keepdims=True))
    b_row = every(lambda i, ch: jnp.sum(jnp.where(eye, b_col[i], 0.0), axis=0, keepdims=True))
    b_last = every(lambda i, ch: jnp.sum(ch["lf"], axis=1, keepdims=True))
    dmat = every(lambda i, ch: jnp.where(ch["tri"], b_col[i] - b_row[i] + ch["li"], -jnp.inf))
    inter = every(lambda i, ch: b_col[i] + m[i])
    m_t = every(lambda i, ch: jnp.maximum(inter[i], jnp.max(dmat[i], axis=1, keepdims=True)))
    w = every(lambda i, ch: jnp.exp(dmat[i] - m_t[i]))
    a = every(lambda i, ch: jnp.exp(inter[i] - m_t[i]))
    vx = every(lambda i, ch: jnp.concatenate([ch["v"], jnp.ones_like(ch["v"])], axis=1))
    s = every(lambda i, ch: _dot(ch["q"], ch["k_t"].astype(bf16)) * w[i])
    qc = every(lambda i, ch: _dot(ch["q"], cx[i].astype(bf16)))
    nd = every(lambda i, ch: a[i] * qc[i] + _dot(s[i].astype(bf16), vx[i]))
    h = every(lambda i, ch: nd[i][:, :A_DIM] / jnp.maximum(jnp.abs(nd[i][:, A_DIM:]), jnp.exp(-m_t[i])))
    g_row = every(lambda i, ch: b_last[i] - b_row[i] + ch["li"])
    m_new = every(lambda i, ch: jnp.maximum(b_last[i] + m[i], jnp.max(g_row[i], axis=1, keepdims=True)))
    decay = every(lambda i, ch: jnp.exp(b_last[i] + m[i] - m_new[i]))
    wk_t = every(lambda i, ch: (ch["k_t"] * jnp.exp(g_row[i] - m_new[i])).astype(bf16))
    for i, ch in enumerate(chains):
        cx_ref[ch["slot"]] = decay[i] * cx[i] + _dot(wk_t[i], vx[i])
        m_ref[ch["slot"]] = m_new[i]
    return h


def _mlstm_kernel(gb_ref, qc_ref, kc_ref, vc_ref, oc_ref, gc_ref, ql_ref, kl_ref, vl_ref, ol_ref, gl_ref, hg_ref,
                  outc_ref, outl_ref, cx_scr, m_scr, g_scr, hc_scr, hl_scr):
    hg = pl.program_id(1)
    L = MLSTM_L
    cx_scr[...] = jnp.zeros_like(cx_scr)
    m_scr[...] = jnp.zeros_like(m_scr)
    kscale = A_DIM ** -0.5
    t_idx = lax.broadcasted_iota(jnp.int32, (L, L), 0)
    s_idx = lax.broadcasted_iota(jnp.int32, (L, L), 1)
    eye = s_idx == t_idx
    tri = (s_idx <= t_idx, s_idx >= t_idx)
    n_gates = 4 * A_HEADS

    def run(q_ref, k_ref, v_ref, g_ref, h_ref):
        nc = q_ref.shape[0] // L

        def body(j, accumulate):
            chains = []
            for d in range(2):
                cj = j if d == 0 else nc - 1 - j
                rows = pl.ds(pl.multiple_of(cj * L, L), L)
                g_scr[d] = g_ref[rows, :].T[:n_gates]
                for hh in range(MLSTM_HP):
                    hd = hg * MLSTM_HP + hh
                    ig, fg = (2 * d) * A_HEADS + hd, (2 * d + 1) * A_HEADS + hd
                    cols = slice(hh * A_DIM, (hh + 1) * A_DIM)
                    chains.append(dict(
                        slot=2 * hh + d, tri=tri[d], rows=rows, cols=cols,
                        li=g_scr[d, pl.ds(ig, 1), :] + gb_ref[ig],
                        lf=_log_sigmoid(g_scr[d, pl.ds(fg, 1), :] + gb_ref[fg]),
                        q=q_ref[rows, cols].astype(bf16), v=v_ref[rows, cols].astype(bf16),
                        k_t=k_ref[cj, cols, :] * kscale))
            half = len(chains) // 2
            hs = _mlstm_chunks(cx_scr, m_scr, eye, chains[:half]) + _mlstm_chunks(cx_scr, m_scr, eye, chains[half:])
            for ch, h in zip(chains, hs):
                if accumulate:
                    h_ref[ch["rows"], ch["cols"]] += h
                else:
                    h_ref[ch["rows"], ch["cols"]] = h

        lax.fori_loop(0, nc // 2, lambda j, c: (body(j, False), c)[1], 0)
        lax.fori_loop(nc // 2, nc, lambda j, c: (body(j, True), c)[1], 0)

    run(qc_ref, kc_ref, vc_ref, gc_ref, hc_scr)
    run(ql_ref, kl_ref, vl_ref, gl_ref, hl_scr)
    for o_ref, h_ref, out_ref in ((oc_ref, hc_scr, outc_ref), (ol_ref, hl_scr, outl_ref)):
        for hh in range(MLSTM_HP):
            cols = slice(hh * A_DIM, (hh + 1) * A_DIM)
            h = h_ref[:, cols]
            hn = (h * lax.rsqrt(jnp.mean(h * h, axis=-1, keepdims=True) + EPS)) * hg_ref[:, cols]
            o = o_ref[:, cols]
            out_ref[:, cols] = (hn / (1.0 + jnp.exp(-o))).astype(out_ref.dtype)


def mlstm_mixer(y_ctx, kt_ctx, y_lat, kt_lat, gate_block, gate_b, head_g):
    bsz, tc, _ = y_ctx.shape
    tl = y_lat.shape[1]
    wide = MLSTM_HP * A_DIM
    n_grp = A_HEADS // MLSTM_HP

    def col(t, base):
        if base is None:
            return pl.BlockSpec((None, t // LANES, wide, LANES), lambda b, h: (b, 0, h, 0))
        return pl.BlockSpec((None, t, wide), functools.partial(lambda b, h, base: (b, 0, base + h), base=base))

    def gate_spec(t):
        return pl.BlockSpec((None, t, LANES), lambda b, h: (b, 0, gate_block))

    def out_spec(t):
        return pl.BlockSpec((None, t, wide), lambda b, h: (b, 0, h))

    return pl.pallas_call(
        _mlstm_kernel,
        out_shape=[jax.ShapeDtypeStruct((bsz, tc, A_HEADS * A_DIM), bf16),
                   jax.ShapeDtypeStruct((bsz, tl, A_HEADS * A_DIM), bf16)],
        grid=(bsz, n_grp),
        in_specs=[pl.BlockSpec(memory_space=pltpu.SMEM)]
        + [col(tc, base) for base in (0, None, n_grp, 2 * n_grp)] + [gate_spec(tc)]
        + [col(tl, base) for base in (0, None, n_grp, 2 * n_grp)] + [gate_spec(tl)]
        + [pl.BlockSpec((1, wide), lambda b, h: (0, h))],
        out_specs=[out_spec(tc), out_spec(tl)],
        scratch_shapes=[pltpu.VMEM((2 * MLSTM_HP, A_DIM, 2 * A_DIM), f32), pltpu.VMEM((2 * MLSTM_HP, 1, 1), f32),
                        pltpu.VMEM((2, 4 * A_HEADS, LANES), f32),
                        pltpu.VMEM((tc, wide), f32), pltpu.VMEM((tl, wide), f32)],
        compiler_params=_cparams(("parallel", "parallel")),
        name="mlstm",
    )(gate_b, y_ctx, kt_ctx, y_ctx, y_ctx, y_ctx, y_lat, kt_lat, y_lat, y_lat, y_lat,
      head_g.reshape(1, A_HEADS * A_DIM))


def _outproj_kernel(x_ref, a1_ref, a2_ref, w1_ref, w2_ref, gate_ref, o_ref):
    y = _dot(a1_ref[...], w1_ref[...]) + _dot(a2_ref[...], w2_ref[...])
    o_ref[...] = x_ref[...] + gate_ref[...] * y


def out_proj(x, a1, a2, w, modt, l, row, tm):
    bsz, t, d = x.shape
    k1, k2 = a1.shape[-1], a2.shape[-1]
    return pl.pallas_call(
        _outproj_kernel,
        out_shape=jax.ShapeDtypeStruct((bsz, t, d), f32),
        grid=(bsz, t // tm),
        in_specs=[pl.BlockSpec((None, tm, d), lambda b, i: (b, i, 0)),
                  pl.BlockSpec((None, tm, k1), lambda b, i: (b, i, 0)),
                  pl.BlockSpec((None, tm, k2), lambda b, i: (b, i, 0)),
                  pl.BlockSpec((k1, d), lambda b, i: (0, 0)),
                  pl.BlockSpec((k2, d), lambda b, i: (0, 0)),
                  _mod_spec(l, 2, row, d)],
        out_specs=pl.BlockSpec((None, tm, d), lambda b, i: (b, i, 0)),
        compiler_params=_cparams(("parallel", "parallel")),
        name="out_proj",
    )(x, a1, a2, w[:k1], w[k1:], modt)


FFN_HALO = 8


FFN_CHUNK = 256


def _ffn_kernel(x_ref, xp_ref, xn_ref, sh_ref, sc_ref, gate_ref, g_ref, wup_ref, cw_ref, cb_ref, wd_ref,
                o_ref, h_scr, u_scr, acc_scr, *, tm):
    i = pl.program_id(1)
    g, sh, sc = g_ref[...], sh_ref[...], sc_ref[...]
    hp = _modulated_norm(xp_ref[...], g, sh, sc)
    hn = _modulated_norm(xn_ref[...], g, sh, sc)
    h_scr[:FFN_HALO] = jnp.where(i > 0, hp, 0.0).astype(bf16)
    h_scr[FFN_HALO:FFN_HALO + tm] = _modulated_norm(x_ref[...], g, sh, sc).astype(bf16)
    h_scr[FFN_HALO + tm:] = jnp.where(i < pl.num_programs(1) - 1, hn, 0.0).astype(bf16)
    dff = wd_ref.shape[0]
    n_chunks = dff // FFN_CHUNK

    def cols(ref, c):
        lo = c * FFN_CHUNK
        return ref[:, lo:lo + FFN_CHUNK], ref[:, dff + lo:dff + lo + FFN_CHUNK]

    half = tm // 2
    up_rows = (slice(0, half + 2 * FFN_HALO), slice(half + 2 * FFN_HALO, tm + 2 * FFN_HALO))

    n_buf = u_scr.shape[0]

    def up_proj(c, rows):
        wg, wv = cols(wup_ref, c)
        u_scr[c % n_buf, rows, :FFN_CHUNK] = _dot(h_scr[rows, :], wg)
        u_scr[c % n_buf, rows, FFN_CHUNK:] = _dot(h_scr[rows, :], wv)

    def gated(c, r):
        u = u_scr.at[c % n_buf]
        cw = jnp.concatenate(cols(cw_ref, c), axis=1)
        cb = jnp.concatenate(cols(cb_ref, c), axis=1)
        S = FFN_HALO
        ng = half // S
        u3 = u[r * half:r * half + half + 2 * S].reshape(ng + 2, S, 2 * FFN_CHUNK)
        sub = lax.broadcasted_iota(jnp.int32, (ng, S, 2 * FFN_CHUNK), 1)
        down = pltpu.roll(u3, 1, axis=1)
        up = pltpu.roll(u3, S - 1, axis=1)
        prev = jnp.where(sub == 0, down[:ng], down[1:ng + 1])
        nxt = jnp.where(sub == S - 1, up[2:], up[1:ng + 1])
        conv = (cw[0:1] * prev + cw[1:2] * u3[1:ng + 1] + cw[2:3] * nxt + cb).reshape(half, 2 * FFN_CHUNK)
        cg, cv = conv[:, :FFN_CHUNK], conv[:, FFN_CHUNK:]
        return ((cg / (1.0 + jnp.exp(-cg))) * cv).astype(bf16)

    def down_proj(act, c0, c1, r):
        part = _dot(act, wd_ref[c0 * FFN_CHUNK:c1 * FFN_CHUNK, :])
        out_rows = slice(r * half, (r + 1) * half)
        if c0 == 0:
            acc_scr[out_rows, :] = part
        else:
            acc_scr[out_rows, :] += part

    pairs = [tuple(range(c, min(c + 2, n_chunks))) for c in range(0, n_chunks, 2)]
    for c in pairs[0]:
        for rows in up_rows:
            up_proj(c, rows)
    pending = None
    for k, pair in enumerate(pairs):
        for r in range(2):
            if k + 1 < len(pairs):
                for c in pairs[k + 1]:
                    up_proj(c, up_rows[r])
            if pending is not None:
                down_proj(*pending)
            acts = [gated(c, r) for c in pair]
            pending = (acts[0] if len(acts) == 1 else jnp.concatenate(acts, axis=1), pair[0], pair[-1] + 1, r)
    down_proj(*pending)
    o_ref[...] = x_ref[...] + gate_ref[...] * acc_scr[...]


def ffn_weights(w_up, conv_w, conv_b, w_down):
    return w_up.astype(bf16), conv_w, conv_b.reshape(1, -1), w_down.astype(bf16)


def conv_ffn(x, modt, l, row, g, weights, tm):
    bsz, t, d = x.shape
    wup, cw, cb, wd = weights
    hb = tm // FFN_HALO
    last = t // FFN_HALO - 1

    def resident(a):
        return pl.BlockSpec(a.shape, lambda b, i: (0,) * a.ndim, pipeline_mode=pl.Buffered(1))

    return pl.pallas_call(
        functools.partial(_ffn_kernel, tm=tm),
        out_shape=jax.ShapeDtypeStruct((bsz, t, d), f32),
        grid=(bsz, t // tm),
        in_specs=[pl.BlockSpec((None, tm, d), lambda b, i: (b, i, 0)),
                  pl.BlockSpec((None, FFN_HALO, d), lambda b, i: (b, jnp.maximum(i * hb - 1, 0), 0)),
                  pl.BlockSpec((None, FFN_HALO, d), lambda b, i: (b, jnp.minimum((i + 1) * hb, last), 0)),
                  _mod_spec(l, 3, row, d), _mod_spec(l, 4, row, d), _mod_spec(l, 5, row, d),
                  pl.BlockSpec((1, d), lambda b, i: (0, 0)),
                  resident(wup), resident(cw), resident(cb), resident(wd)],
        out_specs=pl.BlockSpec((None, tm, d), lambda b, i: (b, i, 0)),
        scratch_shapes=[pltpu.VMEM((tm + 2 * FFN_HALO, d), bf16),
                        pltpu.VMEM((4, tm + 2 * FFN_HALO, 2 * FFN_CHUNK), f32),
                        pltpu.VMEM((tm, d), f32)],
        compiler_params=_cparams(("parallel", "parallel")),
        name="conv_ffn",
    )(x, x, x, modt, modt, modt, g.reshape(1, d), wup, cw, cb, wd)


def _rope_tables(n):
    t = jnp.arange(n)
    row = (t // GRID_W).astype(f32)
    colp = (t % GRID_W).astype(f32)
    half = HEAD_DIM // 2
    freq = ROPE_THETA ** (-jnp.arange(0, half, 2, dtype=f32) / half)
    ang_r = row[:, None] * freq[None, :]
    ang_c = colp[:, None] * freq[None, :]
    ang = jnp.concatenate([ang_r, ang_r, ang_c, ang_c] * 2, axis=-1)
    return jnp.cos(ang), jnp.sin(ang)


def _sink_column(sink, n_kv, tq):
    n_sub = sink.shape[0] // n_kv
    return jnp.repeat((sink * LOG2E).reshape(n_kv, n_sub), tq, axis=1)[..., None].astype(f32)


def kernel(x, c, ctx, c_ctx, ada_w, ada_b, norm_g, w_out, ffn_up, ffn_conv_w, ffn_conv_b, ffn_down, even_w_in,
           mlstm_gate_b, mlstm_head_g, swa_qk_g, swa_sink, odd_w_in, gqa_qk_g, na_qk_g, na_rpb):
    bsz, seq, d = x.shape
    n_ctx = ctx.shape[1]
    depth = ada_w.shape[0]
    a_w = A_HEADS * A_DIM
    cos, sin = _rope_tables(seq)
    ones_c, zeros_c = jnp.ones((n_ctx, LANES), f32), jnp.zeros((n_ctx, LANES), f32)

    cc = jnp.zeros((16, d), f32).at[:bsz].set(c).at[bsz].set(c_ctx)
    modt = ada_modulation(cc, ada_w, ada_b).reshape(depth, 16, 6, 1, d)

    x_lat, x_ctx = x, ctx
    tm_l, tm_c = 512, 256
    for l in range(depth):
        need_ctx = l < depth - 1
        w_o = w_out[l].astype(bf16)
        if l % 2 == 0:
            e = l // 2
            wi = even_w_in[e]
            sp = np.cumsum((0, a_w, a_w, a_w, a_w, 4 * A_HEADS, B_HEADS * HEAD_DIM, B_KV * HEAD_DIM, B_KV * HEAD_DIM))
            aq, ak, av, ao, ag, bq, bk, bv = (wi[:, sp[k]:sp[k + 1]] for k in range(8))
            w = jnp.concatenate([aq, av, ao, ag, jnp.zeros((d, LANES - 4 * A_HEADS), f32), bq, bk, bv],
                                axis=1).astype(bf16)
            gate_col = 3 * a_w
            n_raw = gate_col + LANES
            gq, gk = swa_qk_g[e, 0], swa_qk_g[e, 1]
            cols = dict(n_raw=n_raw, q_cols=(n_raw,), kpair_cols=(n_raw + B_HEADS * HEAD_DIM,), kfull_cols=(),
                        vfull_cols=(), gains=(gq, gk), w_t=ak.T.astype(bf16))
            y_lat, kt_lat, qn_l, k2_l, v2_l = proj_prep(x_lat, modt, l, None, norm_g[l, 0], w, **cols,
                                                        rope=(True, True), cos=cos, sin=sin, tm=tm_l)
            y_ctx, kt_ctx, qn_c, k2_c, v2_c = proj_prep(x_ctx, modt, l, bsz, norm_g[l, 0], w, **cols,
                                                        rope=(False, False), cos=ones_c, sin=zeros_c, tm=tm_c)
            a_ctx, a_lat = mlstm_mixer(y_ctx, kt_ctx, y_lat, kt_lat, gate_col // LANES, mlstm_gate_b[e],
                                       mlstm_head_g[e])
            b_lat = window_attention(qn_l, k2_l, v2_l, k2_c, v2_c, _sink_column(swa_sink[e], B_KV, B_WIN))
            mix_l = (a_lat, b_lat)
            if need_ctx:
                b_ctx = gqa_attention(qn_c, [(k2_c, v2_c)], B_KV, 128, _sink_column(swa_sink[e], B_KV, 128))
                mix_c = (a_ctx, b_ctx)
        else:
            o = l // 2
            wi = odd_w_in[o]
            sp = np.cumsum((0, C_HEADS * HEAD_DIM, C_KV * HEAD_DIM, C_KV * HEAD_DIM) + (D_HEADS * HEAD_DIM,) * 3)
            cq, ck, cv, nq, nk, nv = (wi[:, sp[k]:sp[k + 1]] for k in range(6))
            w_ctx = jnp.concatenate([nk, nv, ck, cv], axis=1).astype(bf16)
            w = jnp.concatenate([cq.astype(bf16), nq.astype(bf16), w_ctx], axis=1)
            gq, gk = gqa_qk_g[o, 0], gqa_qk_g[o, 1]
            nq_g, nk_g = na_qk_g[o, 0], na_qk_g[o, 1]
            wide = 4 * LANES
            cqn, nqn, ck2, cv2, nkn, nvb = proj_prep(
                x_lat, modt, l, None, norm_g[l, 0], w, n_raw=0, q_cols=(0, wide), kpair_cols=(4 * wide,),
                kfull_cols=(2 * wide,), vfull_cols=(3 * wide,), gains=(gq, nq_g, gk, nk_g),
                rope=(True, False, True), cos=cos, sin=sin, tm=tm_l)
            if need_ctx:
                raise NotImplementedError("context outputs of an odd layer")
            ck2_c, cv2_c, nkn_c, nvb_c = proj_prep(
                x_ctx, modt, l, bsz, norm_g[l, 0], w_ctx, n_raw=0, q_cols=(), kpair_cols=(2 * wide,),
                kfull_cols=(0,), vfull_cols=(wide,), gains=(gk, nk_g), rope=(False,), cos=ones_c, sin=zeros_c,
                tm=tm_c)
            c_lat = gqa_attention(cqn, [(ck2_c, cv2_c), (ck2, cv2)], C_KV, 256)
            d_lat = neighbourhood_attention(nqn, nkn, nvb, nkn_c, nvb_c, na_bias_table(na_rpb[o], seq // GRID_W))
            mix_l = (c_lat, d_lat)

        ffn_w = ffn_weights(ffn_up[l], ffn_conv_w[l], ffn_conv_b[l], ffn_down[l])
        x_lat = out_proj(x_lat, mix_l[0], mix_l[1], w_o, modt, l, None, tm_l)
        x_lat = conv_ffn(x_lat, modt, l, None, norm_g[l, 1], ffn_w, tm=512)
        if need_ctx:
            x_ctx = out_proj(x_ctx, mix_c[0], mix_c[1], w_o, modt, l, bsz, tm_c)
            x_ctx = conv_ffn(x_ctx, modt, l, bsz, norm_g[l, 1], ffn_w, tm=256)
    return x_lat
```

```python
import functools

import jax
import jax.numpy as jnp
import numpy as np
from jax import lax
from jax.experimental import pallas as pl
from jax.experimental.pallas import tpu as pltpu

f32 = jnp.float32
bf16 = jnp.bfloat16

GRID_W = 64
HEAD_DIM = 64
LANES = 128
A_HEADS = 4
A_DIM = 128
B_HEADS = 8
B_KV = 2
B_WIN = 128
C_HEADS = 8
C_KV = 2
D_HEADS = 8
NA_ROWS = 8
NA_COLS = 16
NA_QROWS = 4
WIN_BLOCKS = 8
NA_PAIRS = 4
NA_KROWS = 12
ROPE_THETA = 10000.0
EPS = 1e-6
NEG = -1e30
LOG2E = 1.4426950408889634
VMEM_LIMIT = 56 * 1024 * 1024


def _cparams(sem):
    return pltpu.CompilerParams(dimension_semantics=sem, vmem_limit_bytes=VMEM_LIMIT)


def _dot(a, b):
    return jnp.dot(a, b, preferred_element_type=f32)


def _dot_nt(a, b):
    return lax.dot_general(a, b, (((1,), (1,)), ((), ())), preferred_element_type=f32)


def _modulated_norm(x, g, shift, scale):
    y = x * lax.rsqrt(jnp.mean(x * x, axis=-1, keepdims=True) + EPS)
    return (y * g) * (1.0 + scale) + shift


def _mod_spec(l, k, row, d):
    if row is None:
        return pl.BlockSpec((None, None, None, 1, d), lambda b, *_: (l, b, k, 0, 0))
    return pl.BlockSpec((None, None, None, 1, d), lambda b, *_: (l, row, k, 0, 0))


def _ada_kernel(c_ref, w_ref, b_ref, o_ref):
    c = c_ref[...]
    s = c / (1.0 + jnp.exp(-c))
    o_ref[...] = _dot(s.astype(bf16), w_ref[...].astype(bf16)) + b_ref[...]


def ada_modulation(cc, ada_w, ada_b):
    depth, d, n = ada_w.shape
    tn = 1536
    return pl.pallas_call(
        _ada_kernel,
        out_shape=jax.ShapeDtypeStruct((depth, cc.shape[0], n), f32),
        grid=(depth, n // tn),
        in_specs=[pl.BlockSpec(cc.shape, lambda l, j: (0, 0)),
                  pl.BlockSpec((None, d, tn), lambda l, j: (l, 0, j)),
                  pl.BlockSpec((None, 1, tn), lambda l, j: (l, 0, j))],
        out_specs=pl.BlockSpec((None, cc.shape[0], tn), lambda l, j: (l, 0, j)),
        compiler_params=_cparams(("parallel", "parallel")),
        name="ada",
    )(cc, ada_w, ada_b.reshape(depth, 1, n))


def _head_ms(a):
    ri = lax.broadcasted_iota(jnp.int32, (LANES, LANES), 0) // HEAD_DIM
    ci = lax.broadcasted_iota(jnp.int32, (LANES, LANES), 1) // HEAD_DIM
    bd = jnp.where(ri == ci, 1.0, 0.0).astype(bf16)
    ss = a * a
    hi = ss.astype(bf16)
    lo = (ss - hi.astype(f32)).astype(bf16)
    return (_dot(hi, bd) + _dot(lo, bd)) * (1.0 / HEAD_DIM)


def _head_norm(a, g):
    return (a * lax.rsqrt(_head_ms(a) + EPS)) * g


def _rope(a, cos, sin):
    lane = lax.broadcasted_iota(jnp.int32, a.shape, 1)
    quarter = HEAD_DIM // 4
    rot = jnp.where(lane % (2 * quarter) < quarter,
                    -pltpu.roll(a, LANES - quarter, axis=1), pltpu.roll(a, quarter, axis=1))
    return a * cos + rot * sin


def _dup_halves(a):
    lane = lax.broadcasted_iota(jnp.int32, a.shape, 1)
    sw = pltpu.roll(a, HEAD_DIM, axis=1)
    lo = lane < HEAD_DIM
    return jnp.where(lo, a, sw), jnp.where(lo, sw, a)


def _prep_kernel(*refs, n_q, n_kpair, n_kfull, n_vfull, rope, scale):
    it = iter(refs)
    q_refs = [next(it) for _ in range(n_q)]
    kp_refs = [next(it) for _ in range(n_kpair)]
    kf_refs = [next(it) for _ in range(n_kfull)]
    vf_refs = [next(it) for _ in range(n_vfull)]
    g_refs = [next(it) for _ in range(n_q + n_kpair + n_kfull)]
    cos_ref, sin_ref = next(it), next(it)
    qo_refs = [next(it) for _ in range(n_q)]
    kpo_refs = [(next(it), next(it)) for _ in range(n_kpair)]
    kfo_refs = [next(it) for _ in range(n_kfull)]
    vfo_refs = [next(it) for _ in range(n_vfull)]
    gi = iter(g_refs)
    for qi, (q_ref, qo_ref) in enumerate(zip(q_refs, qo_refs)):
        g = next(gi)[...]
        for p in range(q_ref.shape[-1] // LANES):
            a = _head_norm(q_ref[:, p * LANES:(p + 1) * LANES], g)
            if rope[qi]:
                a = _rope(a, cos_ref[...], sin_ref[...])
            qo_ref[:, p * LANES:(p + 1) * LANES] = (a * scale).astype(bf16)
    for ki, (kp_ref, (ko_ref, vo_ref)) in enumerate(zip(kp_refs, kpo_refs)):
        g = next(gi)[...]
        k = _head_norm(kp_ref[:, :LANES], g)
        if rope[n_q + ki]:
            k = _rope(k, cos_ref[...], sin_ref[...])
        k0, k1 = _dup_halves(k)
        ko_ref[0] = k0.astype(bf16)
        ko_ref[1] = k1.astype(bf16)
        v0, v1 = _dup_halves(kp_ref[:, LANES:])
        vo_ref[0] = v0.astype(bf16)
        vo_ref[1] = v1.astype(bf16)
    for kf_ref, kfo_ref in zip(kf_refs, kfo_refs):
        g = next(gi)[...]
        for p in range(kf_ref.shape[-1] // LANES):
            kfo_ref[:, p * LANES:(p + 1) * LANES] = _head_norm(kf_ref[:, p * LANES:(p + 1) * LANES], g).astype(bf16)
    for vf_ref, vfo_ref in zip(vf_refs, vfo_refs):
        vfo_ref[...] = vf_ref[...].astype(bf16)


def _proj_prep_kernel(x_ref, sh_ref, sc_ref, g_ref, w_ref, *rest, n_raw, has_t, q_cols, kpair_cols, kfull_cols,
                      vfull_cols, rope):
    if has_t:
        wt_ref, rest = rest[0], rest[1:]
    n_gain = len(q_cols) + len(kpair_cols) + len(kfull_cols)
    side = rest[:n_gain + 2]
    outs = rest[n_gain + 2:-1]
    y_scr = rest[-1]
    h = _modulated_norm(x_ref[...], g_ref[...], sh_ref[...], sc_ref[...]).astype(bf16)
    y_scr[:, n_raw:] = _dot(h, w_ref[:, n_raw:])
    if n_raw:
        outs[0][...] = _dot(h, w_ref[:, :n_raw])
        outs = outs[1:]
    if has_t:
        y_t = _dot_nt(wt_ref[...], h)
        for c in range(outs[0].shape[0]):
            outs[0][c] = y_t[:, c * LANES:(c + 1) * LANES]
        outs = outs[1:]
    wide, pair = 4 * LANES, 2 * LANES
    views = ([y_scr.at[:, c:c + wide] for c in q_cols] + [y_scr.at[:, c:c + pair] for c in kpair_cols]
             + [y_scr.at[:, c:c + wide] for c in kfull_cols + vfull_cols])
    _prep_kernel(*views, *side, *outs, n_q=len(q_cols), n_kpair=len(kpair_cols), n_kfull=len(kfull_cols),
                 n_vfull=len(vfull_cols), rope=rope, scale=HEAD_DIM ** -0.5 * LOG2E)


def proj_prep(x, modt, l, row, norm_g, w, *, n_raw, q_cols, kpair_cols, kfull_cols, vfull_cols, gains, rope, cos, sin,
              tm, w_t=None):
    bsz, t, d = x.shape
    n = w.shape[1]
    wide = 4 * LANES
    in_specs = [pl.BlockSpec((None, tm, d), lambda b, i: (b, i, 0)),
                _mod_spec(l, 0, row, d), _mod_spec(l, 1, row, d),
                pl.BlockSpec((1, d), lambda b, i: (0, 0)),
                pl.BlockSpec((d, n), lambda b, i: (0, 0))]
    args = [x, modt, modt, norm_g.reshape(1, d), w]
    if w_t is not None:
        in_specs.append(pl.BlockSpec(w_t.shape, lambda b, i: (0, 0)))
        args.append(w_t)
    for g in gains:
        in_specs.append(pl.BlockSpec((1, LANES), lambda b, i: (0, 0)))
        args.append(jnp.tile(g, 2).reshape(1, LANES))
    for tbl in (cos, sin):
        in_specs.append(pl.BlockSpec((tm, LANES), lambda b, i: (i, 0)))
        args.append(tbl)
    out_shape, out_specs = [], []
    if n_raw:
        out_shape.append(jax.ShapeDtypeStruct((bsz, t, n_raw), f32))
        out_specs.append(pl.BlockSpec((None, tm, n_raw), lambda b, i: (b, i, 0)))
    if w_t is not None:
        out_shape.append(jax.ShapeDtypeStruct((bsz, t // LANES, w_t.shape[0], LANES), f32))
        out_specs.append(pl.BlockSpec((None, tm // LANES, w_t.shape[0], LANES), lambda b, i: (b, i, 0, 0)))
    for _ in q_cols:
        out_shape.append(jax.ShapeDtypeStruct((bsz, t, wide), bf16))
        out_specs.append(pl.BlockSpec((None, tm, wide), lambda b, i: (b, i, 0)))
    for _ in kpair_cols:
        for _ in range(2):
            out_shape.append(jax.ShapeDtypeStruct((bsz, 2, t, LANES), bf16))
            out_specs.append(pl.BlockSpec((None, 2, tm, LANES), lambda b, i: (b, 0, i, 0)))
    for _ in kfull_cols + vfull_cols:
        out_shape.append(jax.ShapeDtypeStruct((bsz, t, wide), bf16))
        out_specs.append(pl.BlockSpec((None, tm, wide), lambda b, i: (b, i, 0)))
    kern = functools.partial(_proj_prep_kernel, n_raw=n_raw, has_t=w_t is not None, q_cols=q_cols,
                             kpair_cols=kpair_cols, kfull_cols=kfull_cols, vfull_cols=vfull_cols, rope=rope)
    return pl.pallas_call(
        kern, out_shape=out_shape, grid=(bsz, t // tm), in_specs=in_specs, out_specs=out_specs,
        scratch_shapes=[pltpu.VMEM((tm, n), f32)],
        compiler_params=_cparams(("parallel", "parallel")), name="proj_prep",
    )(*args)


def _stack_heads(q_ref, n_sub):
    parts = []
    for g in range(n_sub):
        blk = q_ref[:, (g // 2) * LANES:(g // 2 + 1) * LANES]
        lane = lax.broadcasted_iota(jnp.int32, blk.shape, 1)
        keep = (lane < HEAD_DIM) if g % 2 == 0 else (lane >= HEAD_DIM)
        parts.append(jnp.where(keep, blk, jnp.zeros_like(blk)))
    return jnp.concatenate(parts, axis=0)


def _scores(q, segs):
    scores = []
    for k, _, bias, mask in segs:
        s = _dot_nt(q, k)
        if bias is not None:
            s = s + bias
        if mask is not None:
            s = jnp.where(mask, s, NEG)
        scores.append(s)
    return scores


def _softmax_attend(q, segs, sink, scores=None):
    return _softmax_attend_chains([(q, segs, sink, scores)])[0]


def _softmax_attend_chains(chains):
    scores = [_scores(q, segs) if sc is None else sc for q, segs, _, sc in chains]
    maxes = [[s.max(axis=-1, keepdims=True) for s in sc] for sc in scores]
    m = []
    for (_, _, sink, _), mx in zip(chains, maxes):
        mi = functools.reduce(jnp.maximum, mx)
        m.append(mi if sink is None else jnp.maximum(mi, sink))
    p = [[jnp.exp2(s - m[i]) for s in sc] for i, sc in enumerate(scores)]
    outs = []
    for i, (_, segs, sink, _) in enumerate(chains):
        den = functools.reduce(jnp.add, [pj.sum(axis=-1, keepdims=True) for pj in p[i]])
        if sink is not None:
            den = den + jnp.exp2(sink - m[i])
        acc = functools.reduce(jnp.add, [_dot(pj.astype(bf16), seg[1]) for pj, seg in zip(p[i], segs)])
        outs.append(acc / den)
    return outs


def _unstack_heads(o, o_ref, n_sub, tq):
    lane = lax.broadcasted_iota(jnp.int32, (tq, LANES), 1)
    for p in range(n_sub // 2):
        even = o[(2 * p) * tq:(2 * p + 1) * tq]
        odd = o[(2 * p + 1) * tq:(2 * p + 2) * tq]
        o_ref[:, p * LANES:(p + 1) * LANES] = jnp.where(lane < HEAD_DIM, even, odd).astype(o_ref.dtype)


def _window_attn_kernel(q_ref, kc_ref, vc_ref, kp_ref, k0_ref, kn_ref, vp_ref, v0_ref, vn_ref, sink_ref, o_ref, *, tq):
    i = pl.program_id(2)
    nb = pl.num_programs(2)
    n_sub = B_HEADS // B_KV
    n_ctx = kc_ref.shape[0]
    rows, n_keys = n_sub * tq, n_ctx + 3 * tq
    qpos = lax.broadcasted_iota(jnp.int32, (rows, n_keys), 0) % tq
    kpos = lax.broadcasted_iota(jnp.int32, (rows, n_keys), 1) - n_ctx
    in_prev, in_next = (kpos >= 0) & (kpos < tq), kpos >= 2 * tq
    bad = (in_prev & (kpos < qpos)) | (in_next & (kpos - 2 * tq > qpos))
    own = [slice(j * tq, (j + 1) * tq) for j in range(WIN_BLOCKS)]
    k_blocks = [kp_ref[...]] + [k0_ref[r] for r in own] + [kn_ref[...]]
    v_blocks = [vp_ref[...]] + [v0_ref[r] for r in own] + [vn_ref[...]]
    chains = []
    for j in range(WIN_BLOCKS):
        invalid = bad
        if j == 0:
            invalid = invalid | (in_prev & (i == 0))
        if j == WIN_BLOCKS - 1:
            invalid = invalid | (in_next & (i == nb - 1))
        k_all = jnp.concatenate([kc_ref[...]] + k_blocks[j:j + 3], axis=0)
        v_all = jnp.concatenate([vc_ref[...]] + v_blocks[j:j + 3], axis=0)
        chains.append((_stack_heads(q_ref.at[own[j]], n_sub), [(k_all, v_all, None, ~invalid)], sink_ref[...], None))
    for j, o in enumerate(_softmax_attend_chains(chains)):
        _unstack_heads(o, o_ref.at[own[j]], n_sub, tq)


def window_attention(qn, k2, v2, kc2, vc2, sink_col):
    bsz, s, _ = qn.shape
    c = kc2.shape[2]
    tq = B_WIN
    step = WIN_BLOCKS * tq
    nb = s // step
    n_half = s // tq
    n_sub = B_HEADS // B_KV
    wq = n_sub * HEAD_DIM
    half_spec = lambda fn: pl.BlockSpec((None, None, tq, LANES), fn)
    prev = lambda b, h, i: (b, h, jnp.maximum(WIN_BLOCKS * i - 1, 0), 0)
    nxt = lambda b, h, i: (b, h, jnp.minimum(WIN_BLOCKS * (i + 1), n_half - 1), 0)
    cur_spec = pl.BlockSpec((None, None, step, LANES), lambda b, h, i: (b, h, i, 0))
    ctx_spec = pl.BlockSpec((None, None, c, LANES), lambda b, h, i: (b, h, 0, 0))
    return pl.pallas_call(
        functools.partial(_window_attn_kernel, tq=tq),
        out_shape=jax.ShapeDtypeStruct((bsz, s, B_HEADS * HEAD_DIM), bf16),
        grid=(bsz, B_KV, nb),
        in_specs=[pl.BlockSpec((None, step, wq), lambda b, h, i: (b, i, h)),
                  ctx_spec, ctx_spec,
                  half_spec(prev), cur_spec, half_spec(nxt),
                  half_spec(prev), cur_spec, half_spec(nxt),
                  pl.BlockSpec((None, n_sub * tq, 1), lambda b, h, i: (h, 0, 0))],
        out_specs=pl.BlockSpec((None, step, wq), lambda b, h, i: (b, i, h)),
        compiler_params=_cparams(("parallel", "parallel", "parallel")),
        name="window_attn",
    )(qn, kc2, vc2, k2, k2, k2, v2, v2, v2, sink_col)


def _seg_attn_kernel(*refs, n_seg, n_sub, tq, has_sink):
    q_ref = refs[0]
    kv = refs[1:1 + 2 * n_seg]
    sink = refs[1 + 2 * n_seg][...] if has_sink else None
    o_ref = refs[-1]
    segs = [(kv[2 * j][...], kv[2 * j + 1][...], None, None) for j in range(n_seg)]
    n_pair = n_sub // 2
    lanes = [slice(p * LANES, (p + 1) * LANES) for p in range(n_pair)]
    sub = min(tq, 128)
    blocks = [(p, slice(r, r + sub)) for p in range(n_pair) for r in range(0, tq, sub)]
    chains = []
    for p, rows in blocks:
        sink_p = None
        if sink is not None:
            sink_p = jnp.concatenate([sink[(2 * p + g) * tq:(2 * p + g) * tq + sub] for g in range(2)], axis=0)
        chains.append((_stack_heads(q_ref.at[rows, lanes[p]], 2), segs, sink_p, None))
    for (p, rows), o in zip(blocks, _softmax_attend_chains(chains)):
        _unstack_heads(o, o_ref.at[rows, lanes[p]], 2, sub)


def gqa_attention(qn, kv_segs, n_kv, tq, sink_col=None):
    bsz, s, width = qn.shape
    n_sub = width // HEAD_DIM // n_kv
    wq = n_sub * HEAD_DIM
    in_specs = [pl.BlockSpec((None, tq, wq), lambda b, h, i: (b, i, h))]
    args = [qn]
    for k2, v2 in kv_segs:
        n = k2.shape[2]
        spec = pl.BlockSpec((None, None, n, LANES), lambda b, h, i: (b, h, 0, 0))
        in_specs += [spec, spec]
        args += [k2, v2]
    if sink_col is not None:
        in_specs.append(pl.BlockSpec((None, n_sub * tq, 1), lambda b, h, i: (h, 0, 0)))
        args.append(sink_col)
    return pl.pallas_call(
        functools.partial(_seg_attn_kernel, n_seg=len(kv_segs), n_sub=n_sub, tq=tq, has_sink=sink_col is not None),
        out_shape=jax.ShapeDtypeStruct((bsz, s, width), bf16),
        grid=(bsz, n_kv, s // tq),
        in_specs=in_specs,
        out_specs=pl.BlockSpec((None, tq, wq), lambda b, h, i: (b, i, h)),
        compiler_params=_cparams(("parallel", "parallel", "parallel")),
        name="gqa_attn",
    )(*args)


def _na_attn_kernel(q_ref, kc_ref, vc_ref, k_ref, v_ref, bias_ref, o_ref, *, tq, n_grp):
    r = pl.program_id(2)
    rows_total = k_ref.shape[0] // GRID_W
    start = jnp.clip(r * NA_QROWS - NA_ROWS // 2, 0, rows_total - NA_KROWS)
    off = pl.multiple_of(start * GRID_W, GRID_W)
    nk = NA_KROWS * GRID_W
    lane = lax.broadcasted_iota(jnp.int32, (tq, LANES), 1)
    chains = []
    for p in range(NA_PAIRS):
        lanes = slice(p * LANES, (p + 1) * LANES)
        blk = q_ref[:, lanes]
        k_nb, v_nb = k_ref[pl.ds(off, nk), lanes], v_ref[pl.ds(off, nk), lanes]
        for half in range(2):
            keep = (lane < HEAD_DIM) if half == 0 else (lane >= HEAD_DIM)
            segs = [(kc_ref[:, lanes], vc_ref[:, lanes], None, None),
                    (k_nb, v_nb, bias_ref[p, half * tq:(half + 1) * tq, :], None)]
            chains.append((jnp.where(keep, blk, jnp.zeros_like(blk)), segs, None, None))
    outs = _softmax_attend_chains(chains)
    for p in range(NA_PAIRS):
        o_ref[:, p * LANES:(p + 1) * LANES] = jnp.where(lane < HEAD_DIM, outs[2 * p], outs[2 * p + 1]).astype(o_ref.dtype)


def neighbourhood_attention(qn, kn, vb, kcn, vcb, bias_tbl):
    bsz, s, width = qn.shape
    c = kcn.shape[1]
    tq = NA_QROWS * GRID_W
    n_grp = s // tq
    wide = NA_PAIRS * LANES
    variant = lambda r: jnp.where(r == 0, 0, jnp.where(r == n_grp - 1, 2, 1))
    return pl.pallas_call(
        functools.partial(_na_attn_kernel, tq=tq, n_grp=n_grp),
        out_shape=jax.ShapeDtypeStruct((bsz, s, width), bf16),
        grid=(width // wide, bsz, n_grp),
        in_specs=[pl.BlockSpec((None, tq, wide), lambda p, b, r: (b, r, p)),
                  pl.BlockSpec((None, c, wide), lambda p, b, r: (b, 0, p)),
                  pl.BlockSpec((None, c, wide), lambda p, b, r: (b, 0, p)),
                  pl.BlockSpec((None, s, wide), lambda p, b, r: (b, 0, p)),
                  pl.BlockSpec((None, s, wide), lambda p, b, r: (b, 0, p)),
                  pl.BlockSpec((None, NA_PAIRS, 2 * tq, NA_KROWS * GRID_W), lambda p, b, r: (variant(r), p, 0, 0))],
        out_specs=pl.BlockSpec((None, tq, wide), lambda p, b, r: (b, r, p)),
        compiler_params=_cparams(("parallel", "parallel", "parallel")),
        name="na_attn",
    )(qn, kcn, vcb, kn, vb, bias_tbl)


def _na_bias_kernel(rp_ref, o_ref, *, rows_total):
    n_grp = rows_total // NA_QROWS
    qc = lax.broadcasted_iota(jnp.int32, (GRID_W, LANES), 0)
    lane = lax.broadcasted_iota(jnp.int32, (GRID_W, LANES), 1)
    kc = lane % GRID_W
    cs = jnp.clip(qc - NA_COLS // 2, 0, GRID_W - NA_COLS)
    col_ok = (kc >= cs) & (kc < cs + NA_COLS)
    left = lane < GRID_W
    neg = jnp.full((GRID_W, LANES), NEG, f32)
    for v, r0 in enumerate((0, NA_QROWS, (n_grp - 1) * NA_QROWS)):
        start = min(max(r0 - NA_ROWS // 2, 0), rows_total - NA_KROWS)
        for hh in range(2):
            for i in range(NA_QROWS):
                qr = r0 + i
                ws = min(max(qr - NA_ROWS // 2, 0), rows_total - NA_ROWS)
                for jj in range(NA_KROWS // 2):
                    kra, krb = start + 2 * jj, start + 2 * jj + 1
                    ok_a, ok_b = ws <= kra < ws + NA_ROWS, ws <= krb < ws + NA_ROWS
                    r_lo = hh * NA_QROWS * GRID_W + i * GRID_W
                    dst = (v, slice(r_lo, r_lo + GRID_W), slice(jj * LANES, (jj + 1) * LANES))
                    if not (ok_a or ok_b):
                        o_ref[dst] = neg
                        continue
                    dra = min(max(kra - qr + NA_ROWS - 1, 0), 2 * NA_ROWS - 2)
                    drb = min(max(krb - qr + NA_ROWS - 1, 0), 2 * NA_ROWS - 2)
                    row = jnp.where(left[:1], rp_ref[hh, dra:dra + 1, :], rp_ref[hh, drb:drb + 1, :])
                    toe = pltpu.roll(jnp.broadcast_to(row, (GRID_W, LANES)), LANES - (NA_COLS - 1), axis=1,
                                     stride=1, stride_axis=0)
                    ok = col_ok if (ok_a and ok_b) else (col_ok & left if ok_a else col_ok & ~left)
                    o_ref[dst] = jnp.where(ok, toe * LOG2E, neg)


def na_bias_table(rpb, rows_total):
    h, ndr, ndc = rpb.shape
    rp = jnp.zeros((h, 16, LANES), f32).at[:, :ndr, :ndc].set(rpb).at[:, :ndr, GRID_W:GRID_W + ndc].set(rpb)
    return pl.pallas_call(
        functools.partial(_na_bias_kernel, rows_total=rows_total),
        out_shape=jax.ShapeDtypeStruct((3, h // 2, 2 * NA_QROWS * GRID_W, NA_KROWS * GRID_W), f32),
        grid=(h // 2,),
        in_specs=[pl.BlockSpec((2, 16, LANES), lambda p: (p, 0, 0))],
        out_specs=pl.BlockSpec((3, None, 2 * NA_QROWS * GRID_W, NA_KROWS * GRID_W), lambda p: (0, p, 0, 0)),
        compiler_params=_cparams(("parallel",)),
        name="na_bias",
    )(rp)


def _log_sigmoid(x):
    return jnp.minimum(x, 0.0) - jnp.log1p(jnp.exp(-jnp.abs(x)))


MLSTM_L = 128
MLSTM_HP = 4


def _mlstm_chunks(cx_ref, m_ref, eye, chains):
    every = lambda fn: [fn(i, ch) for i, ch in enumerate(chains)]
    cx = every(lambda i, ch: cx_ref[ch["slot"]])
    m = every(lambda i, ch: m_ref[ch["slot"]])
    b_col = every(lambda i, ch: jnp.sum(jnp.where(ch["tri"], ch["lf"], 0.0), axis=1, keepdims=True))
    b_row = every(lambda i, ch: jnp.sum(jnp.where(eye, b_col[i], 0.0), axis=0, keepdims=True))
    b_last = every(lambda i, ch: jnp.sum(ch["lf"], axis=1, keepdims=True))
    dmat = every(lambda i, ch: jnp.where(ch["tri"], b_col[i] - b_row[i] + ch["li"], -jnp.inf))
    inter = every(lambda i, ch: b_col[i] + m[i])
    m_t = every(lambda i, ch: jnp.maximum(inter[i], jnp.max(dmat[i], axis=1, keepdims=True)))
    w = every(lambda i, ch: jnp.exp(dmat[i] - m_t[i]))
    a = every(lambda i, ch: jnp.exp(inter[i] - m_t[i]))
    vx = every(lambda i, ch: jnp.concatenate([ch["v"], jnp.ones_like(ch["v"])], axis=1))
    s = every(lambda i, ch: _dot(ch["q"], ch["k_t"].astype(bf16)) * w[i])
    qc = every(lambda i, ch: _dot(ch["q"], cx[i].astype(bf16)))
    nd = every(lambda i, ch: a[i] * qc[i] + _dot(s[i].astype(bf16), vx[i]))
    h = every(lambda i, ch: nd[i][:, :A_DIM] / jnp.maximum(jnp.abs(nd[i][:, A_DIM:]), jnp.exp(-m_t[i])))
    g_row = every(lambda i, ch: b_last[i] - b_row[i] + ch["li"])
    m_new = every(lambda i, ch: jnp.maximum(b_last[i] + m[i], jnp.max(g_row[i], axis=1, keepdims=True)))
    decay = every(lambda i, ch: jnp.exp(b_last[i] + m[i] - m_new[i]))
    wk_t = every(lambda i, ch: (ch["k_t"] * jnp.exp(g_row[i] - m_new[i])).astype(bf16))
    for i, ch in enumerate(chains):
        cx_ref[ch["slot"]] = decay[i] * cx[i] + _dot(wk_t[i], vx[i])
        m_ref[ch["slot"]] = m_new[i]
    return h


def _mlstm_kernel(gb_ref, qc_ref, kc_ref, vc_ref, oc_ref, gc_ref, ql_ref, kl_ref, vl_ref, ol_ref, gl_ref, hg_ref,
                  outc_ref, outl_ref, cx_scr, m_scr, g_scr, hc_scr, hl_scr):
    hg = pl.program_id(1)
    L = MLSTM_L
    cx_scr[...] = jnp.zeros_like(cx_scr)
    m_scr[...] = jnp.zeros_like(m_scr)
    kscale = A_DIM ** -0.5
    t_idx = lax.broadcasted_iota(jnp.int32, (L, L), 0)
    s_idx = lax.broadcasted_iota(jnp.int32, (L, L), 1)
    eye = s_idx == t_idx
    tri = (s_idx <= t_idx, s_idx >= t_idx)
    n_gates = 4 * A_HEADS

    def run(q_ref, k_ref, v_ref, g_ref, h_ref):
        nc = q_ref.shape[0] // L

        def body(j, accumulate):
            chains = []
            for d in range(2):
                cj = j if d == 0 else nc - 1 - j
                rows = pl.ds(pl.multiple_of(cj * L, L), L)
                g_scr[d] = g_ref[rows, :].T[:n_gates]
                for hh in range(MLSTM_HP):
                    hd = hg * MLSTM_HP + hh
                    ig, fg = (2 * d) * A_HEADS + hd, (2 * d + 1) * A_HEADS + hd
                    cols = slice(hh * A_DIM, (hh + 1) * A_DIM)
                    chains.append(dict(
                        slot=2 * hh + d, tri=tri[d], rows=rows, cols=cols,
                        li=g_scr[d, pl.ds(ig, 1), :] + gb_ref[ig],
                        lf=_log_sigmoid(g_scr[d, pl.ds(fg, 1), :] + gb_ref[fg]),
                        q=q_ref[rows, cols].astype(bf16), v=v_ref[rows, cols].astype(bf16),
                        k_t=k_ref[cj, cols, :] * kscale))
            half = len(chains) // 2
            hs = _mlstm_chunks(cx_scr, m_scr, eye, chains[:half]) + _mlstm_chunks(cx_scr, m_scr, eye, chains[half:])
            for ch, h in zip(chains, hs):
                if accumulate:
                    h_ref[ch["rows"], ch["cols"]] += h
                else:
                    h_ref[ch["rows"], ch["cols"]] = h

        lax.fori_loop(0, nc // 2, lambda j, c: (body(j, False), c)[1], 0)
        lax.fori_loop(nc // 2, nc, lambda j, c: (body(j, True), c)[1], 0)

    run(qc_ref, kc_ref, vc_ref, gc_ref, hc_scr)
    run(ql_ref, kl_ref, vl_ref, gl_ref, hl_scr)
    for o_ref, h_ref, out_ref in ((oc_ref, hc_scr, outc_ref), (ol_ref, hl_scr, outl_ref)):
        for hh in range(MLSTM_HP):
            cols = slice(hh * A_DIM, (hh + 1) * A_DIM)
            h = h_ref[:, cols]
            hn = (h * lax.rsqrt(jnp.mean(h * h, axis=-1, keepdims=True) + EPS)) * hg_ref[:, cols]
            o = o_ref[:, cols]
            out_ref[:, cols] = (hn / (1.0 + jnp.exp(-o))).astype(out_ref.dtype)


def mlstm_mixer(y_ctx, kt_ctx, y_lat, kt_lat, gate_block, gate_b, head_g):
    bsz, tc, _ = y_ctx.shape
    tl = y_lat.shape[1]
    wide = MLSTM_HP * A_DIM
    n_grp = A_HEADS // MLSTM_HP

    def col(t, base):
        if base is None:
            return pl.BlockSpec((None, t // LANES, wide, LANES), lambda b, h: (b, 0, h, 0))
        return pl.BlockSpec((None, t, wide), functools.partial(lambda b, h, base: (b, 0, base + h), base=base))

    def gate_spec(t):
        return pl.BlockSpec((None, t, LANES), lambda b, h: (b, 0, gate_block))

    def out_spec(t):
        return pl.BlockSpec((None, t, wide), lambda b, h: (b, 0, h))

    return pl.pallas_call(
        _mlstm_kernel,
        out_shape=[jax.ShapeDtypeStruct((bsz, tc, A_HEADS * A_DIM), bf16),
                   jax.ShapeDtypeStruct((bsz, tl, A_HEADS * A_DIM), bf16)],
        grid=(bsz, n_grp),
        in_specs=[pl.BlockSpec(memory_space=pltpu.SMEM)]
        + [col(tc, base) for base in (0, None, n_grp, 2 * n_grp)] + [gate_spec(tc)]
        + [col(tl, base) for base in (0, None, n_grp, 2 * n_grp)] + [gate_spec(tl)]
        + [pl.BlockSpec((1, wide), lambda b, h: (0, h))],
        out_specs=[out_spec(tc), out_spec(tl)],
        scratch_shapes=[pltpu.VMEM((2 * MLSTM_HP, A_DIM, 2 * A_DIM), f32), pltpu.VMEM((2 * MLSTM_HP, 1, 1), f32),
                        pltpu.VMEM((2, 4 * A_HEADS, LANES), f32),
                        pltpu.VMEM((tc, wide), f32), pltpu.VMEM((tl, wide), f32)],
        compiler_params=_cparams(("parallel", "parallel")),
        name="mlstm",
    )(gate_b, y_ctx, kt_ctx, y_ctx, y_ctx, y_ctx, y_lat, kt_lat, y_lat, y_lat, y_lat,
      head_g.reshape(1, A_HEADS * A_DIM))


def _outproj_kernel(x_ref, a1_ref, a2_ref, w1_ref, w2_ref, gate_ref, o_ref):
    y = _dot(a1_ref[...], w1_ref[...]) + _dot(a2_ref[...], w2_ref[...])
    o_ref[...] = x_ref[...] + gate_ref[...] * y


def out_proj(x, a1, a2, w, modt, l, row, tm):
    bsz, t, d = x.shape
    k1, k2 = a1.shape[-1], a2.shape[-1]
    return pl.pallas_call(
        _outproj_kernel,
        out_shape=jax.ShapeDtypeStruct((bsz, t, d), f32),
        grid=(bsz, t // tm),
        in_specs=[pl.BlockSpec((None, tm, d), lambda b, i: (b, i, 0)),
                  pl.BlockSpec((None, tm, k1), lambda b, i: (b, i, 0)),
                  pl.BlockSpec((None, tm, k2), lambda b, i: (b, i, 0)),
                  pl.BlockSpec((k1, d), lambda b, i: (0, 0)),
                  pl.BlockSpec((k2, d), lambda b, i: (0, 0)),
                  _mod_spec(l, 2, row, d)],
        out_specs=pl.BlockSpec((None, tm, d), lambda b, i: (b, i, 0)),
        compiler_params=_cparams(("parallel", "parallel")),
        name="out_proj",
    )(x, a1, a2, w[:k1], w[k1:], modt)


FFN_HALO = 8


FFN_CHUNK = 256
FFN_GROUP = 3


def _ffn_kernel(x_ref, xp_ref, xn_ref, sh_ref, sc_ref, gate_ref, g_ref, wup_ref, cw_ref, cb_ref, wd_ref,
                o_ref, h_scr, u_scr, acc_scr, *, tm):
    i = pl.program_id(1)
    g, sh, sc = g_ref[...], sh_ref[...], sc_ref[...]
    hp = _modulated_norm(xp_ref[...], g, sh, sc)
    hn = _modulated_norm(xn_ref[...], g, sh, sc)
    h_scr[:FFN_HALO] = jnp.where(i > 0, hp, 0.0).astype(bf16)
    h_scr[FFN_HALO:FFN_HALO + tm] = _modulated_norm(x_ref[...], g, sh, sc).astype(bf16)
    h_scr[FFN_HALO + tm:] = jnp.where(i < pl.num_programs(1) - 1, hn, 0.0).astype(bf16)
    dff = wd_ref.shape[0]
    n_chunks = dff // FFN_CHUNK

    def cols(ref, c):
        lo = c * FFN_CHUNK
        return ref[:, lo:lo + FFN_CHUNK], ref[:, dff + lo:dff + lo + FFN_CHUNK]

    half = tm // 2
    up_rows = (slice(0, half + 2 * FFN_HALO), slice(half + 2 * FFN_HALO, tm + 2 * FFN_HALO))

    n_buf = u_scr.shape[0]

    def up_proj(c, rows):
        wg, wv = cols(wup_ref, c)
        u_scr[c % n_buf, rows, :FFN_CHUNK] = _dot(h_scr[rows, :], wg)
        u_scr[c % n_buf, rows, FFN_CHUNK:] = _dot(h_scr[rows, :], wv)

    def gated(c, r):
        u = u_scr.at[c % n_buf]
        cw = jnp.concatenate(cols(cw_ref, c), axis=1)
        cb = jnp.concatenate(cols(cb_ref, c), axis=1)
        S = FFN_HALO
        ng = half // S
        u3 = u[r * half:r * half + half + 2 * S].reshape(ng + 2, S, 2 * FFN_CHUNK)
        sub = lax.broadcasted_iota(jnp.int32, (ng, S, 2 * FFN_CHUNK), 1)
        down = pltpu.roll(u3, 1, axis=1)
        up = pltpu.roll(u3, S - 1, axis=1)
        prev = jnp.where(sub == 0, down[:ng], down[1:ng + 1])
        nxt = jnp.where(sub == S - 1, up[2:], up[1:ng + 1])
        conv = (cw[0:1] * prev + cw[1:2] * u3[1:ng + 1] + cw[2:3] * nxt + cb).reshape(half, 2 * FFN_CHUNK)
        cg, cv = conv[:, :FFN_CHUNK], conv[:, FFN_CHUNK:]
        return ((cg / (1.0 + jnp.exp(-cg))) * cv).astype(bf16)

    def down_proj(act, c0, c1, r):
        part = _dot(act, wd_ref[c0 * FFN_CHUNK:c1 * FFN_CHUNK, :])
        out_rows = slice(r * half, (r + 1) * half)
        if c0 == 0:
            acc_scr[out_rows, :] = part
        else:
            acc_scr[out_rows, :] += part

    groups = [tuple(range(c, min(c + FFN_GROUP, n_chunks))) for c in range(0, n_chunks, FFN_GROUP)]
    for c in groups[0]:
        for rows in up_rows:
            up_proj(c, rows)
    pending = None
    for k, group in enumerate(groups):
        for r in range(2):
            if k + 1 < len(groups):
                for c in groups[k + 1]:
                    up_proj(c, up_rows[r])
            if pending is not None:
                down_proj(*pending)
            acts = [gated(c, r) for c in group]
            pending = (acts[0] if len(acts) == 1 else jnp.concatenate(acts, axis=1), group[0], group[-1] + 1, r)
    down_proj(*pending)
    o_ref[...] = x_ref[...] + gate_ref[...] * acc_scr[...]


def ffn_weights(w_up, conv_w, conv_b, w_down):
    return w_up.astype(bf16), conv_w, conv_b.reshape(1, -1), w_down.astype(bf16)


def conv_ffn(x, modt, l, row, g, weights, tm):
    bsz, t, d = x.shape
    wup, cw, cb, wd = weights
    hb = tm // FFN_HALO
    last = t // FFN_HALO - 1

    def resident(a):
        return pl.BlockSpec(a.shape, lambda b, i: (0,) * a.ndim, pipeline_mode=pl.Buffered(1))

    return pl.pallas_call(
        functools.partial(_ffn_kernel, tm=tm),
        out_shape=jax.ShapeDtypeStruct((bsz, t, d), f32),
        grid=(bsz, t // tm),
        in_specs=[pl.BlockSpec((None, tm, d), lambda b, i: (b, i, 0)),
                  pl.BlockSpec((None, FFN_HALO, d), lambda b, i: (b, jnp.maximum(i * hb - 1, 0), 0)),
                  pl.BlockSpec((None, FFN_HALO, d), lambda b, i: (b, jnp.minimum((i + 1) * hb, last), 0)),
                  _mod_spec(l, 3, row, d), _mod_spec(l, 4, row, d), _mod_spec(l, 5, row, d),
                  pl.BlockSpec((1, d), lambda b, i: (0, 0)),
                  resident(wup), resident(cw), resident(cb), resident(wd)],
        out_specs=pl.BlockSpec((None, tm, d), lambda b, i: (b, i, 0)),
        scratch_shapes=[pltpu.VMEM((tm + 2 * FFN_HALO, d), bf16),
                        pltpu.VMEM((2 * FFN_GROUP, tm + 2 * FFN_HALO, 2 * FFN_CHUNK), f32),
                        pltpu.VMEM((tm, d), f32)],
        compiler_params=_cparams(("parallel", "parallel")),
        name="conv_ffn",
    )(x, x, x, modt, modt, modt, g.reshape(1, d), wup, cw, cb, wd)


def _rope_tables(n):
    t = jnp.arange(n)
    row = (t // GRID_W).astype(f32)
    colp = (t % GRID_W).astype(f32)
    half = HEAD_DIM // 2
    freq = ROPE_THETA ** (-jnp.arange(0, half, 2, dtype=f32) / half)
    ang_r = row[:, None] * freq[None, :]
    ang_c = colp[:, None] * freq[None, :]
    ang = jnp.concatenate([ang_r, ang_r, ang_c, ang_c] * 2, axis=-1)
    return jnp.cos(ang), jnp.sin(ang)


def _sink_column(sink, n_kv, tq):
    n_sub = sink.shape[0] // n_kv
    return jnp.repeat((sink * LOG2E).reshape(n_kv, n_sub), tq, axis=1)[..., None].astype(f32)


def kernel(x, c, ctx, c_ctx, ada_w, ada_b, norm_g, w_out, ffn_up, ffn_conv_w, ffn_conv_b, ffn_down, even_w_in,
           mlstm_gate_b, mlstm_head_g, swa_qk_g, swa_sink, odd_w_in, gqa_qk_g, na_qk_g, na_rpb):
    bsz, seq, d = x.shape
    n_ctx = ctx.shape[1]
    depth = ada_w.shape[0]
    a_w = A_HEADS * A_DIM
    cos, sin = _rope_tables(seq)
    ones_c, zeros_c = jnp.ones((n_ctx, LANES), f32), jnp.zeros((n_ctx, LANES), f32)

    cc = jnp.zeros((16, d), f32).at[:bsz].set(c).at[bsz].set(c_ctx)
    modt = ada_modulation(cc, ada_w, ada_b).reshape(depth, 16, 6, 1, d)

    x_lat, x_ctx = x, ctx
    tm_l, tm_c = 512, 256
    for l in range(depth):
        need_ctx = l < depth - 1
        w_o = w_out[l].astype(bf16)
        if l % 2 == 0:
            e = l // 2
            wi = even_w_in[e]
            sp = np.cumsum((0, a_w, a_w, a_w, a_w, 4 * A_HEADS, B_HEADS * HEAD_DIM, B_KV * HEAD_DIM, B_KV * HEAD_DIM))
            aq, ak, av, ao, ag, bq, bk, bv = (wi[:, sp[k]:sp[k + 1]] for k in range(8))
            w = jnp.concatenate([aq, av, ao, ag, jnp.zeros((d, LANES - 4 * A_HEADS), f32), bq, bk, bv],
                                axis=1).astype(bf16)
            gate_col = 3 * a_w
            n_raw = gate_col + LANES
            gq, gk = swa_qk_g[e, 0], swa_qk_g[e, 1]
            cols = dict(n_raw=n_raw, q_cols=(n_raw,), kpair_cols=(n_raw + B_HEADS * HEAD_DIM,), kfull_cols=(),
                        vfull_cols=(), gains=(gq, gk), w_t=ak.T.astype(bf16))
            y_lat, kt_lat, qn_l, k2_l, v2_l = proj_prep(x_lat, modt, l, None, norm_g[l, 0], w, **cols,
                                                        rope=(True, True), cos=cos, sin=sin, tm=tm_l)
            y_ctx, kt_ctx, qn_c, k2_c, v2_c = proj_prep(x_ctx, modt, l, bsz, norm_g[l, 0], w, **cols,
                                                        rope=(False, False), cos=ones_c, sin=zeros_c, tm=tm_c)
            a_ctx, a_lat = mlstm_mixer(y_ctx, kt_ctx, y_lat, kt_lat, gate_col // LANES, mlstm_gate_b[e],
                                       mlstm_head_g[e])
            b_lat = window_attention(qn_l, k2_l, v2_l, k2_c, v2_c, _sink_column(swa_sink[e], B_KV, B_WIN))
            mix_l = (a_lat, b_lat)
            if need_ctx:
                b_ctx = gqa_attention(qn_c, [(k2_c, v2_c)], B_KV, 128, _sink_column(swa_sink[e], B_KV, 128))
                mix_c = (a_ctx, b_ctx)
        else:
            o = l // 2
            wi = odd_w_in[o]
            sp = np.cumsum((0, C_HEADS * HEAD_DIM, C_KV * HEAD_DIM, C_KV * HEAD_DIM) + (D_HEADS * HEAD_DIM,) * 3)
            cq, ck, cv, nq, nk, nv = (wi[:, sp[k]:sp[k + 1]] for k in range(6))
            w_ctx = jnp.concatenate([nk, nv, ck, cv], axis=1).astype(bf16)
            w = jnp.concatenate([cq.astype(bf16), nq.astype(bf16), w_ctx], axis=1)
            gq, gk = gqa_qk_g[o, 0], gqa_qk_g[o, 1]
            nq_g, nk_g = na_qk_g[o, 0], na_qk_g[o, 1]
            wide = 4 * LANES
            cqn, nqn, ck2, cv2, nkn, nvb = proj_prep(
                x_lat, modt, l, None, norm_g[l, 0], w, n_raw=0, q_cols=(0, wide), kpair_cols=(4 * wide,),
                kfull_cols=(2 * wide,), vfull_cols=(3 * wide,), gains=(gq, nq_g, gk, nk_g),
                rope=(True, False, True), cos=cos, sin=sin, tm=tm_l)
            if need_ctx:
                raise NotImplementedError("context outputs of an odd layer")
            ck2_c, cv2_c, nkn_c, nvb_c = proj_prep(
                x_ctx, modt, l, bsz, norm_g[l, 0], w_ctx, n_raw=0, q_cols=(), kpair_cols=(2 * wide,),
                kfull_cols=(0,), vfull_cols=(wide,), gains=(gk, nk_g), rope=(False,), cos=ones_c, sin=zeros_c,
                tm=tm_c)
            c_lat = gqa_attention(cqn, [(ck2_c, cv2_c), (ck2, cv2)], C_KV, 256)
            d_lat = neighbourhood_attention(nqn, nkn, nvb, nkn_c, nvb_c, na_bias_table(na_rpb[o], seq // GRID_W))
            mix_l = (c_lat, d_lat)

        ffn_w = ffn_weights(ffn_up[l], ffn_conv_w[l], ffn_conv_b[l], ffn_down[l])
        x_lat = out_proj(x_lat, mix_l[0], mix_l[1], w_o, modt, l, None, tm_l)
        x_lat = conv_ffn(x_lat, modt, l, None, norm_g[l, 1], ffn_w, tm=512)
        if need_ctx:
            x_ctx = out_proj(x_ctx, mix_c[0], mix_c[1], w_o, modt, l, bsz, tm_c)
            x_ctx = conv_ffn(x_ctx, modt, l, bsz, norm_g[l, 1], ffn_w, tm=256)
    return x_lat
```
